```python
import jax, jax.numpy as jnp
from jax import lax
import numpy as np

D_MODEL = 2048
BATCH = 1
SEQ = 16384
DEPTH = 1

EPS = 1e-6
NSA_HEAD_DIM = 128
NSA_HEADS = D_MODEL // 256
NSA_KV_GROUPS = 2
NSA_HPG = NSA_HEADS // NSA_KV_GROUPS
NSA_WIDTH = NSA_HEADS * NSA_HEAD_DIM
NSA_KV_WIDTH = NSA_KV_GROUPS * NSA_HEAD_DIM
CMP_BLOCK = 32
CMP_STRIDE = 16
SLC_BLOCK = 64
SLC_TOPK = 16
WINDOW = 512
Q_BLOCK = 128
SEL_BIG = 1e6
RWKV_HEAD_DIM = 64
RWKV_HEADS = D_MODEL // 128
RWKV_WIDTH = RWKV_HEADS * RWKV_HEAD_DIM
W_RANK = 64
A_RANK = 64
G_RANK = 160
LNX_EPS = 64e-5
NSA_COLS = NSA_WIDTH + 6 * NSA_KV_WIDTH + 3 * NSA_HEADS
RWKV_COLS = 3 * RWKV_WIDTH + W_RANK + A_RANK + G_RANK
IN_COLS = NSA_COLS + RWKV_COLS
MIX_WIDTH = NSA_WIDTH + RWKV_WIDTH
MEM_TOKENS = 256
MEM_HEADS = 4
MEM_HEAD_DIM = D_MODEL // MEM_HEADS
N_GROUPS = 4
EXPERTS_PER_GROUP = 8
N_EXPERTS = N_GROUPS * EXPERTS_PER_GROUP
TOP_K_INNER = 2
EXPERT_FF = 256

kernel_name = "hymba_nsa_rwkv7_hiermoe_layer"


def rmsnorm(x, g):
    xf = x.astype(jnp.float32)
    y = xf * lax.rsqrt(jnp.mean(xf * xf, axis=-1, keepdims=True) + EPS)
    return (y * g.astype(jnp.float32)).astype(x.dtype)


def split_cols(z, sizes):
    offs = np.cumsum([0] + list(sizes))
    return [z[..., int(offs[i]):int(offs[i + 1])] for i in range(len(sizes))]


def token_shift(z):
    return jnp.pad(z, ((0, 0), (1, 0), (0, 0)))[:, :-1]


def masked_softmax(s, mask):
    s = jnp.where(mask, s.astype(jnp.float32), -1e30)
    m = jnp.max(s, axis=-1, keepdims=True)
    e = jnp.where(mask, jnp.exp(s - m), 0.0)
    return e / jnp.maximum(jnp.sum(e, axis=-1, keepdims=True), 1e-30)


def compress_kv(kv, pos, w1, w2):
    B, T, G, Dh = kv.shape
    n_cmp = (T - CMP_BLOCK) // CMP_STRIDE + 1
    idx = jnp.arange(n_cmp)[:, None] * CMP_STRIDE + jnp.arange(CMP_BLOCK)[None, :]
    blocks = kv[:, idx] + pos[None, None, :, None, :]
    blocks = jnp.transpose(blocks, (0, 1, 3, 2, 4)).reshape(B, n_cmp, G, CMP_BLOCK * Dh)
    return jax.nn.gelu(blocks @ w1) @ w2


def nsa_mixer(q, kc, vc, ks, vs, kw, vw, gate_logits,
              cmp_pos_k, cmp_w1_k, cmp_w2_k, cmp_pos_v, cmp_w1_v, cmp_w2_v):
    B, T, _ = q.shape
    G, P, Dh = NSA_KV_GROUPS, NSA_HPG, NSA_HEAD_DIM
    scale = Dh ** -0.5

    def kv4(t):
        return t.reshape(B, T, G, Dh)

    q = q.reshape(B, T, G, P, Dh)
    gates = jax.nn.sigmoid(gate_logits.reshape(B, T, 3, G, P))
    k_cmp = compress_kv(kv4(kc), cmp_pos_k, cmp_w1_k, cmp_w2_k)
    v_cmp = compress_kv(kv4(vc), cmp_pos_v, cmp_w1_v, cmp_w2_v)
    n_cmp = k_cmp.shape[1]
    cmp_start = jnp.arange(n_cmp) * CMP_STRIDE
    cmp_end = cmp_start + CMP_BLOCK - 1
    n_sel = T // SLC_BLOCK
    sel_start = jnp.arange(n_sel) * SLC_BLOCK
    overlap = (jnp.minimum(cmp_start[:, None] + CMP_BLOCK, sel_start[None, :] + SLC_BLOCK)
               - jnp.maximum(cmp_start[:, None], sel_start[None, :]))
    cmp_to_sel = jnp.clip(overlap, 0, None).astype(jnp.float32) / CMP_BLOCK
    top_n = min(SLC_TOPK, n_sel)
    k_blk = jnp.transpose(kv4(ks).reshape(B, n_sel, SLC_BLOCK, G, Dh), (0, 3, 1, 2, 4))
    v_blk = jnp.transpose(kv4(vs).reshape(B, n_sel, SLC_BLOCK, G, Dh), (0, 3, 1, 2, 4))
    pad = ((0, 0), (WINDOW, 0), (0, 0), (0, 0))
    k_win = jnp.pad(kv4(kw), pad)
    v_win = jnp.pad(kv4(vw), pad)
    b_ix = jnp.arange(B)[:, None, None, None]
    g_ix = jnp.arange(G)[None, None, :, None]
    sel_j = jnp.arange(n_sel)
    slc_off = jnp.arange(SLC_BLOCK)
    win_off = jnp.arange(WINDOW + Q_BLOCK) - WINDOW

    def query_block(qi):
        q0 = qi * Q_BLOCK
        qpos = q0 + jnp.arange(Q_BLOCK)
        qb = lax.dynamic_slice_in_dim(q, q0, Q_BLOCK, axis=1)
        gb = lax.dynamic_slice_in_dim(gates, q0, Q_BLOCK, axis=1)
        s_c = jnp.einsum("bqgpd,bcgd->bqgpc", qb, k_cmp) * scale
        p_c = masked_softmax(s_c, (cmp_end[None, :] <= qpos[:, None])[None, :, None, None, :])
        o_c = jnp.einsum("bqgpc,bcgd->bqgpd", p_c.astype(v_cmp.dtype), v_cmp)
        p_sel = jnp.einsum("bqgpc,cs->bqgs", p_c, cmp_to_sel)
        blk = qpos // SLC_BLOCK
        forced = ((sel_j[None, :] == 0) | (sel_j[None, :] == blk[:, None])
                  | (sel_j[None, :] == blk[:, None] - 1))
        future = sel_j[None, :] > blk[:, None]
        score = jnp.where(future[None, :, None, :], -SEL_BIG,
                          p_sel + SEL_BIG * forced[None, :, None, :].astype(jnp.float32))
        _, sel_idx = lax.top_k(score, top_n)
        k_s = k_blk[b_ix, g_ix, sel_idx].reshape(B, Q_BLOCK, G, top_n * SLC_BLOCK, Dh)
        v_s = v_blk[b_ix, g_ix, sel_idx].reshape(B, Q_BLOCK, G, top_n * SLC_BLOCK, Dh)
        kpos_s = (sel_idx[..., None] * SLC_BLOCK + slc_off).reshape(B, Q_BLOCK, G, top_n * SLC_BLOCK)
        s_s = jnp.einsum("bqgpd,bqgnd->bqgpn", qb, k_s) * scale
        p_s = masked_softmax(s_s, (kpos_s <= qpos[None, :, None, None])[:, :, :, None, :])
        o_s = jnp.einsum("bqgpn,bqgnd->bqgpd", p_s.astype(v_s.dtype), v_s)
        k_w = lax.dynamic_slice_in_dim(k_win, q0, WINDOW + Q_BLOCK, axis=1)
        v_w = lax.dynamic_slice_in_dim(v_win, q0, WINDOW + Q_BLOCK, axis=1)
        kpos_w = q0 + win_off
        diff = qpos[:, None] - kpos_w[None, :]
        mask_w = (diff >= 0) & (diff < WINDOW) & (kpos_w[None, :] >= 0)
        s_w = jnp.einsum("bqgpd,bkgd->bqgpk", qb, k_w) * scale
        p_w = masked_softmax(s_w, mask_w[None, :, None, None, :])
        o_w = jnp.einsum("bqgpk,bkgd->bqgpd", p_w.astype(v_w.dtype), v_w)
        o = (gb[:, :, 0, :, :, None] * o_c + gb[:, :, 1, :, :, None] * o_s
             + gb[:, :, 2, :, :, None] * o_w)
        return o.reshape(B, Q_BLOCK, NSA_WIDTH)

    out = lax.map(query_block, jnp.arange(T // Q_BLOCK))
    return jnp.transpose(out, (1, 0, 2, 3)).reshape(B, T, NSA_WIDTH)


def rwkv7_scan(r, w, k, v, kk, a):
    B, T, H, N = r.shape

    def step(S, inp):
        r_t, w_t, k_t, v_t, kk_t, a_t = inp
        sa = jnp.einsum("bhij,bhj->bhi", S, -kk_t)
        S = (S * w_t[:, :, None, :] + sa[..., None] * (kk_t * a_t)[:, :, None, :]
             + v_t[..., None] * k_t[:, :, None, :])
        return S, jnp.einsum("bhij,bhj->bhi", S, r_t)

    xs = tuple(jnp.moveaxis(t.astype(jnp.float32), 1, 0) for t in (r, w, k, v, kk, a))
    _, y = lax.scan(step, jnp.zeros((B, H, N, N), jnp.float32), xs)
    return jnp.moveaxis(y, 0, 1)


def rwkv7_mixer(z, mu, w0, w_up, a0, a_up, g_up, k_k, k_a, r_k, lnx_g, lnx_b):
    B, T, _ = z.shape
    z = z + (token_shift(z) - z) * mu
    r, k, v, wd, ad, gd = split_cols(z, [RWKV_WIDTH] * 3 + [W_RANK, A_RANK, G_RANK])
    w_log = -jax.nn.softplus(-(w0 + jnp.tanh(wd) @ w_up)) - 0.5
    decay = jnp.exp(-jnp.exp(w_log.astype(jnp.float32)))
    a = jax.nn.sigmoid(a0 + ad @ a_up)
    g = jax.nn.sigmoid(gd) @ g_up

    def heads(t):
        return t.reshape(B, T, RWKV_HEADS, RWKV_HEAD_DIM)

    kk = heads(k * k_k).astype(jnp.float32)
    kk = kk * lax.rsqrt(jnp.maximum(jnp.sum(kk * kk, axis=-1, keepdims=True), 1e-24))
    k = k * (1.0 + (a - 1.0) * k_a)
    r_h, k_h, v_h = heads(r), heads(k), heads(v)
    y = rwkv7_scan(r_h, heads(decay), k_h, v_h, kk, heads(a))
    mean = jnp.mean(y, axis=-1, keepdims=True)
    var = jnp.mean(jnp.square(y - mean), axis=-1, keepdims=True)
    y = ((y - mean) * lax.rsqrt(var + LNX_EPS)).reshape(B, T, RWKV_WIDTH) * lnx_g + lnx_b
    bonus = jnp.sum(r_h * k_h * r_k, axis=-1, keepdims=True) * v_h
    y = y + bonus.reshape(B, T, RWKV_WIDTH)
    return (y * g).astype(z.dtype)


def memory_attention(h, m, wq, wk, wv, wo):
    B, T, _ = h.shape
    M = m.shape[1]
    q = (h @ wq).reshape(B, T, MEM_HEADS, MEM_HEAD_DIM)
    k = (m @ wk).reshape(B, M, MEM_HEADS, MEM_HEAD_DIM)
    v = (m @ wv).reshape(B, M, MEM_HEADS, MEM_HEAD_DIM)
    s = jnp.einsum("bthd,bmhd->bhtm", q, k).astype(jnp.float32) * MEM_HEAD_DIM ** -0.5
    p = jax.nn.softmax(s, axis=-1)
    o = jnp.einsum("bhtm,bmhd->bthd", p.astype(v.dtype), v).reshape(B, T, MEM_HEADS * MEM_HEAD_DIM)
    return o @ wo


def hier_moe(h, wg_r, bg_r, we_r, be_r, w_gate, w_up, w_down):
    B, T, D = h.shape
    hf = h.reshape(B * T, D)
    n = hf.shape[0]
    p_group = jax.nn.softmax((hf @ wg_r + bg_r).astype(jnp.float32), axis=-1)
    pg_top, g_idx = lax.top_k(p_group, 1)
    e_logits = (hf @ we_r + be_r).reshape(n, N_GROUPS, EXPERTS_PER_GROUP)
    e_logits = jnp.take_along_axis(e_logits, g_idx[:, :, None], axis=1)[:, 0]
    p_exp = jax.nn.softmax(e_logits.astype(jnp.float32), axis=-1)
    pe_top, e_idx = lax.top_k(p_exp, TOP_K_INNER)
    wts = pg_top * pe_top / jnp.sum(pe_top, axis=-1, keepdims=True)
    flat = g_idx * EXPERTS_PER_GROUP + e_idx
    combine = jnp.sum(jax.nn.one_hot(flat, N_EXPERTS, dtype=jnp.float32) * wts[..., None],
                      axis=1).astype(h.dtype)
    y = jnp.zeros_like(hf)
    for grp in range(N_GROUPS):
        sl = slice(grp * EXPERTS_PER_GROUP, (grp + 1) * EXPERTS_PER_GROUP)
        hid = (jax.nn.silu(jnp.einsum("nd,edf->nef", hf, w_gate[sl]))
               * jnp.einsum("nd,edf->nef", hf, w_up[sl]))
        y = y + jnp.einsum("nef,efd->nd", hid * combine[:, sl, None], w_down[sl])
    return y.reshape(B, T, D)


def setup_inputs(seed: int = 0) -> dict:
    key = jax.random.key(seed)
    keys = iter(jax.random.split(key, 64))
    L = DEPTH

    def nrm(shape, scale):
        return jax.random.normal(next(keys), shape, jnp.float32) * scale

    def unif(shape, lo, hi):
        return jax.random.uniform(next(keys), shape, jnp.float32, lo, hi)

    def gain(n):
        return 1.0 + nrm((L, n), 0.02)

    return {
        "x": nrm((BATCH, SEQ, D_MODEL), 1.0),
        "mem": nrm((BATCH, MEM_TOKENS, D_MODEL), 1.0),
        "ln1_g": gain(D_MODEL),
        "w_in": nrm((L, D_MODEL, IN_COLS), D_MODEL ** -0.5),
        "cmp_pos_k": nrm((L, CMP_BLOCK, NSA_HEAD_DIM), 0.02),
        "cmp_w1_k": nrm((L, CMP_BLOCK * NSA_HEAD_DIM, NSA_HEAD_DIM), (CMP_BLOCK * NSA_HEAD_DIM) ** -0.5),
        "cmp_w2_k": nrm((L, NSA_HEAD_DIM, NSA_HEAD_DIM), NSA_HEAD_DIM ** -0.5),
        "cmp_pos_v": nrm((L, CMP_BLOCK, NSA_HEAD_DIM), 0.02),
        "cmp_w1_v": nrm((L, CMP_BLOCK * NSA_HEAD_DIM, NSA_HEAD_DIM), (CMP_BLOCK * NSA_HEAD_DIM) ** -0.5),
        "cmp_w2_v": nrm((L, NSA_HEAD_DIM, NSA_HEAD_DIM), NSA_HEAD_DIM ** -0.5),
        "nsa_norm_g": gain(NSA_WIDTH),
        "rwkv_mu": unif((L, RWKV_COLS), 0.0, 1.0),
        "rwkv_w0": unif((L, RWKV_WIDTH), -5.0, 1.0),
        "rwkv_w_up": nrm((L, W_RANK, RWKV_WIDTH), 0.5 * W_RANK ** -0.5),
        "rwkv_a0": nrm((L, RWKV_WIDTH), 0.1),
        "rwkv_a_up": nrm((L, A_RANK, RWKV_WIDTH), 0.5 * A_RANK ** -0.5),
        "rwkv_g_up": nrm((L, G_RANK, RWKV_WIDTH), G_RANK ** -0.5),
        "rwkv_k_k": 0.85 + nrm((L, RWKV_WIDTH), 0.02),
        "rwkv_k_a": 1.0 + nrm((L, RWKV_WIDTH), 0.02),
        "rwkv_r_k": nrm((L, RWKV_HEADS, RWKV_HEAD_DIM), 0.1),
        "rwkv_lnx_g": gain(RWKV_WIDTH),
        "rwkv_lnx_b": nrm((L, RWKV_WIDTH), 0.01),
        "w_out": nrm((L, MIX_WIDTH, D_MODEL), MIX_WIDTH ** -0.5),
        "ln_mem_g": gain(D_MODEL),
        "ln2_g": gain(D_MODEL),
        "wq_mem": nrm((L, D_MODEL, MEM_HEADS * MEM_HEAD_DIM), D_MODEL ** -0.5),
        "wk_mem": nrm((L, D_MODEL, MEM_HEADS * MEM_HEAD_DIM), D_MODEL ** -0.5),
        "wv_mem": nrm((L, D_MODEL, MEM_HEADS * MEM_HEAD_DIM), D_MODEL ** -0.5),
        "wo_mem": nrm((L, MEM_HEADS * MEM_HEAD_DIM, D_MODEL), (MEM_HEADS * MEM_HEAD_DIM) ** -0.5),
        "ln3_g": gain(D_MODEL),
        "router_group_w": nrm((L, D_MODEL, N_GROUPS), D_MODEL ** -0.5),
        "router_group_b": nrm((L, N_GROUPS), 0.01),
        "router_expert_w": nrm((L, D_MODEL, N_EXPERTS), D_MODEL ** -0.5),
        "router_expert_b": nrm((L, N_EXPERTS), 0.01),
        "moe_w_gate": nrm((L, N_EXPERTS, D_MODEL, EXPERT_FF), D_MODEL ** -0.5),
        "moe_w_up": nrm((L, N_EXPERTS, D_MODEL, EXPERT_FF), D_MODEL ** -0.5),
        "moe_w_down": nrm((L, N_EXPERTS, EXPERT_FF, D_MODEL), EXPERT_FF ** -0.5),
        "lnf_g": 1.0 + nrm((D_MODEL,), 0.02),
    }


def reference(x, mem, ln1_g, w_in, cmp_pos_k, cmp_w1_k, cmp_w2_k, cmp_pos_v, cmp_w1_v, cmp_w2_v,
              nsa_norm_g, rwkv_mu, rwkv_w0, rwkv_w_up, rwkv_a0, rwkv_a_up, rwkv_g_up, rwkv_k_k,
              rwkv_k_a, rwkv_r_k, rwkv_lnx_g, rwkv_lnx_b, w_out, ln_mem_g, ln2_g, wq_mem, wk_mem,
              wv_mem, wo_mem, ln3_g, router_group_w, router_group_b, router_expert_w,
              router_expert_b, moe_w_gate, moe_w_up, moe_w_down, lnf_g):
    for l in range(DEPTH):
        h = rmsnorm(x, ln1_g[l])
        z = h @ w_in[l]
        z_nsa, z_rwkv = z[..., :NSA_COLS], z[..., NSA_COLS:]
        q, kc, vc, ks, vs, kw, vw, gl = split_cols(
            z_nsa, [NSA_WIDTH] + [NSA_KV_WIDTH] * 6 + [3 * NSA_HEADS])
        y_a = nsa_mixer(q, kc, vc, ks, vs, kw, vw, gl, cmp_pos_k[l], cmp_w1_k[l], cmp_w2_k[l],
                        cmp_pos_v[l], cmp_w1_v[l], cmp_w2_v[l])
        y_a = rmsnorm(y_a, nsa_norm_g[l])
        y_b = rwkv7_mixer(z_rwkv, rwkv_mu[l], rwkv_w0[l], rwkv_w_up[l], rwkv_a0[l], rwkv_a_up[l],
                          rwkv_g_up[l], rwkv_k_k[l], rwkv_k_a[l], rwkv_r_k[l], rwkv_lnx_g[l],
                          rwkv_lnx_b[l])
        x = x + jnp.concatenate([y_a.astype(x.dtype), y_b.astype(x.dtype)], axis=-1) @ w_out[l]
        x = x + memory_attention(rmsnorm(x, ln2_g[l]), rmsnorm(mem, ln_mem_g[l]),
                                 wq_mem[l], wk_mem[l], wv_mem[l], wo_mem[l])
        x = x + hier_moe(rmsnorm(x, ln3_g[l]), router_group_w[l], router_group_b[l],
                         router_expert_w[l], router_expert_b[l], moe_w_gate[l], moe_w_up[l],
                         moe_w_down[l])
    return rmsnorm(x, lnf_g)
```

```python
import functools

import numpy as np
import jax
import jax.numpy as jnp
from jax import lax
from jax.experimental import pallas as pl
from jax.experimental.pallas import tpu as pltpu

F32 = jnp.float32
BF16 = jnp.bfloat16

LANES = 128
D_MODEL = 2048
EPS = 1e-6
NSA_HEAD_DIM = 128
NSA_HEADS = 8
NSA_KV_GROUPS = 2
NSA_HPG = NSA_HEADS // NSA_KV_GROUPS
NSA_WIDTH = NSA_HEADS * NSA_HEAD_DIM
NSA_KV_WIDTH = NSA_KV_GROUPS * NSA_HEAD_DIM
CMP_BLOCK = 32
CMP_STRIDE = 16
SLC_BLOCK = 64
SLC_TOPK = 16
WINDOW = 512
NEG = -1e30
RWKV_HEAD_DIM = 64
RWKV_HEADS = 16
RWKV_WIDTH = RWKV_HEADS * RWKV_HEAD_DIM
RWKV_PAIRS = RWKV_WIDTH // LANES
W_RANK = 64
A_RANK = 64
G_RANK = 160
LNX_EPS = 64e-5
RWKV_CHUNK = 64
MEM_HEADS = 4
MEM_HEAD_DIM = D_MODEL // MEM_HEADS
N_GROUPS = 4
EXPERTS_PER_GROUP = 8
N_EXPERTS = N_GROUPS * EXPERTS_PER_GROUP
EXPERT_FF = 256

NSA_TILES = 20
RW_TILES = 28
IN_TILE_BLOCK = 4

VMEM_LIMIT = 56 * 1024 * 1024


def _cparams(sem):
    return pltpu.CompilerParams(dimension_semantics=sem, vmem_limit_bytes=VMEM_LIMIT)


def _dot(a, b):
    return jnp.dot(a, b, preferred_element_type=F32)


def _dot_nt(a, b):
    return lax.dot_general(a, b, (((1,), (1,)), ((), ())), preferred_element_type=F32)


def _dot_tn(a, b):
    return lax.dot_general(a, b, (((0,), (0,)), ((), ())), preferred_element_type=F32)


def _split2(x):
    hi = x.astype(BF16)
    lo = (x - hi.astype(F32)).astype(BF16)
    return hi, lo


def _split3(x):
    hi = x.astype(BF16)
    r1 = x - hi.astype(F32)
    mid = r1.astype(BF16)
    lo = (r1 - mid.astype(F32)).astype(BF16)
    return hi, mid, lo


def _dot_x2(x, w):
    hi, lo = _split2(x)
    return _dot(hi, w) + _dot(lo, w)


def _dot_x3(x, w):
    hi, mid, lo = _split3(x)
    return _dot(hi, w) + _dot(mid, w) + _dot(lo, w)


def _dot_xw3(x, w_hi, w_lo):
    hi, lo = _split2(x)
    return _dot(hi, w_hi) + _dot(lo, w_hi) + _dot(hi, w_lo)


def _rms(x, g):
    ms = jnp.mean(x * x, axis=-1, keepdims=True)
    return x * lax.rsqrt(ms + EPS) * g


def _in_proj_kernel(x_ref, g_ref, w_ref, zn_ref, zr_ref, h_scr, *, nsa_blocks, q_blocks, q_scale):
    n = pl.program_id(1)

    @pl.when(n == 0)
    def _():
        h_scr[...] = _rms(x_ref[...], g_ref[...]).astype(BF16)

    z = _dot(h_scr[...], w_ref[...])

    @pl.when(n < nsa_blocks)
    def _():
        zs = z * jnp.where(n < q_blocks, q_scale, 1.0)
        for t in range(IN_TILE_BLOCK):
            zn_ref[t] = zs[:, t * LANES:(t + 1) * LANES].astype(BF16)

    @pl.when(n >= nsa_blocks)
    def _():
        for t in range(IN_TILE_BLOCK):
            zr_ref[t] = z[:, t * LANES:(t + 1) * LANES]


def _in_proj(x, g, w, tm):
    T = x.shape[0]
    nb = IN_TILE_BLOCK * LANES
    nsa_blocks = NSA_TILES // IN_TILE_BLOCK
    n_blocks = (NSA_TILES + RW_TILES) // IN_TILE_BLOCK
    kern = functools.partial(_in_proj_kernel, nsa_blocks=nsa_blocks,
                             q_blocks=NSA_HEADS // IN_TILE_BLOCK, q_scale=NSA_HEAD_DIM ** -0.5)
    return pl.pallas_call(
        kern,
        grid=(T // tm, n_blocks),
        in_specs=[
            pl.BlockSpec((tm, D_MODEL), lambda m, n: (m, 0)),
            pl.BlockSpec((1, D_MODEL), lambda m, n: (0, 0)),
            pl.BlockSpec((D_MODEL, nb), lambda m, n: (0, n)),
        ],
        out_specs=[
            pl.BlockSpec((IN_TILE_BLOCK, tm, LANES), lambda m, n: (jnp.minimum(n, nsa_blocks - 1), m, 0)),
            pl.BlockSpec((IN_TILE_BLOCK, tm, LANES), lambda m, n: (jnp.maximum(n - nsa_blocks, 0), m, 0)),
        ],
        out_shape=[
            jax.ShapeDtypeStruct((NSA_TILES, T, LANES), BF16),
            jax.ShapeDtypeStruct((RW_TILES, T, LANES), F32),
        ],
        scratch_shapes=[pltpu.VMEM((tm, D_MODEL), BF16)],
        compiler_params=_cparams(("parallel", "arbitrary")),
        name="in_proj",
    )(x, g, w)


def _compress_kernel(x_ref, pos_ref, w1_ref, w2_ref, o_ref):
    half = CMP_STRIDE * NSA_HEAD_DIM
    x = x_ref[0]
    w1 = w1_ref[0]
    h_first = _dot(x, w1[:half])
    h_second = _dot(x, w1[half:])
    n = x.shape[0]
    h_next = pltpu.roll(h_second, n - 1, axis=0)
    bias = _dot(pos_ref[0], w1)[0:1]
    hid = jax.nn.gelu(h_first + h_next + bias)
    o_ref[0] = _dot(hid.astype(BF16), w2_ref[0]).astype(o_ref.dtype)


def _compress(zn16, pos, w1, w2):
    n = zn16.shape[1]
    width = CMP_BLOCK * NSA_HEAD_DIM
    return pl.pallas_call(
        _compress_kernel,
        grid=(4,),
        in_specs=[
            pl.BlockSpec((1, n, CMP_STRIDE * NSA_HEAD_DIM), lambda s: (NSA_HEADS + s, 0, 0)),
            pl.BlockSpec((1, 8, width), lambda s: (s // 2, 0, 0)),
            pl.BlockSpec((1, width, NSA_HEAD_DIM), lambda s: (s // 2, 0, 0)),
            pl.BlockSpec((1, NSA_HEAD_DIM, NSA_HEAD_DIM), lambda s: (s // 2, 0, 0)),
        ],
        out_specs=pl.BlockSpec((1, n, NSA_HEAD_DIM), lambda s: (s, 0, 0)),
        out_shape=jax.ShapeDtypeStruct((4, n, NSA_HEAD_DIM), BF16),
        compiler_params=_cparams(("parallel",)),
        name="nsa_compress",
    )(zn16, pos, w1, w2)


def _cmp_select_kernel(q_ref, kc_ref, vc_ref, m_ref, oc_ref, sel_ref, *, tq, top_n):
    q0 = pl.program_id(1) * tq
    rows = NSA_HPG * tq
    ncp = kc_ref.shape[1]
    ns = m_ref.shape[1]
    q = q_ref[...].reshape(rows, NSA_HEAD_DIM)
    s = _dot_nt(q, kc_ref[0])
    qpos = q0 + (lax.broadcasted_iota(jnp.int32, (rows, 1), 0) & (tq - 1))
    cmp_end = lax.broadcasted_iota(jnp.int32, (1, ncp), 1) * CMP_STRIDE + (CMP_BLOCK - 1)
    mask = cmp_end <= qpos
    s = jnp.where(mask, s, NEG)
    mx = jnp.max(s, axis=-1, keepdims=True)
    e = jnp.where(mask, jnp.exp(s - mx), 0.0)
    p = e / jnp.maximum(jnp.sum(e, axis=-1, keepdims=True), 1e-30)
    o = _dot(p.astype(BF16), vc_ref[0])
    for h in range(NSA_HPG):
        oc_ref[:, h * NSA_HEAD_DIM:(h + 1) * NSA_HEAD_DIM] = o[h * tq:(h + 1) * tq]
    psum = p[0:tq]
    for h in range(1, NSA_HPG):
        psum = psum + p[h * tq:(h + 1) * tq]
    p_sel = _dot_x2(psum, m_ref[...])
    qp = q0 + lax.broadcasted_iota(jnp.int32, (tq, 1), 0)
    blk = qp >> 6
    j = lax.broadcasted_iota(jnp.int32, (tq, ns), 1)
    future = j > blk
    forced = (j == 0) | (j == blk) | (j == blk - 1)
    score = jnp.where(future, -1.0, jnp.where(forced, 1e6, p_sel))
    sel = jnp.zeros((tq, ns), F32)
    for _ in range(top_n):
        best = jnp.max(score, axis=-1, keepdims=True)
        idx = jnp.min(jnp.where(score == best, j, ns), axis=-1, keepdims=True)
        hit = j == idx
        sel = jnp.where(hit & (best >= 0.0), 1.0, sel)
        score = jnp.where(hit, -2.0, score)
    sel_ref[0] = sel.astype(sel_ref.dtype)


def _cmp_select(zn, kv_cmp, cmp_to_sel, tq):
    T = zn.shape[1]
    ncp = kv_cmp.shape[1]
    ns = cmp_to_sel.shape[1]
    kern = functools.partial(_cmp_select_kernel, tq=tq, top_n=min(SLC_TOPK, ns))
    return pl.pallas_call(
        kern,
        grid=(NSA_KV_GROUPS, T // tq),
        in_specs=[
            pl.BlockSpec((NSA_HPG, tq, NSA_HEAD_DIM), lambda g, i: (g, i, 0)),
            pl.BlockSpec((1, ncp, NSA_HEAD_DIM), lambda g, i: (g, 0, 0)),
            pl.BlockSpec((1, ncp, NSA_HEAD_DIM), lambda g, i: (NSA_KV_GROUPS + g, 0, 0)),
            pl.BlockSpec((ncp, ns), lambda g, i: (0, 0)),
        ],
        out_specs=[
            pl.BlockSpec((tq, NSA_HPG * NSA_HEAD_DIM), lambda g, i: (i, g)),
            pl.BlockSpec((1, tq, ns), lambda g, i: (g, i, 0)),
        ],
        out_shape=[
            jax.ShapeDtypeStruct((T, NSA_WIDTH), F32),
            jax.ShapeDtypeStruct((NSA_KV_GROUPS, T, ns), BF16),
        ],
        compiler_params=_cparams(("parallel", "parallel")),
        name="nsa_cmp_select",
    )(zn, kv_cmp, kv_cmp, cmp_to_sel)


def _sel_attn_kernel(q_ref, k_ref, v_ref, sel_ref, o_ref, m_scr, l_scr, acc_scr, *, tq, tk):
    i = pl.program_id(1)
    kj = pl.program_id(2)
    q0 = i * tq
    k0 = kj * tk
    ns = sel_ref.shape[2]

    @pl.when(kj == 0)
    def _():
        m_scr[...] = jnp.full(m_scr.shape, NEG, F32)
        l_scr[...] = jnp.zeros(l_scr.shape, F32)
        acc_scr[...] = jnp.zeros(acc_scr.shape, F32)

    @pl.when(k0 <= q0 + tq - 1)
    def _():
        blk_of_key = (k0 + lax.broadcasted_iota(jnp.int32, (ns, tk), 1)) >> 6
        member = jnp.where(blk_of_key == lax.broadcasted_iota(jnp.int32, (ns, tk), 0), 1.0, 0.0).astype(BF16)
        chosen = _dot(sel_ref[0], member)
        qpos = q0 + lax.broadcasted_iota(jnp.int32, (tq, 1), 0)
        kpos = k0 + lax.broadcasted_iota(jnp.int32, (1, tk), 1)
        ok = (chosen > 0.5) & (kpos <= qpos)
        k = k_ref[0]
        v = v_ref[0]
        for h in range(NSA_HPG):
            s = jnp.where(ok, _dot_nt(q_ref[h], k), NEG)
            m_old = m_scr[h]
            m_new = jnp.maximum(m_old, jnp.max(s, axis=-1, keepdims=True))
            p = jnp.where(ok, jnp.exp(s - m_new), 0.0)
            alpha = jnp.exp(m_old - m_new)
            l_scr[h] = alpha * l_scr[h] + jnp.sum(p, axis=-1, keepdims=True)
            acc_scr[h] = alpha * acc_scr[h] + _dot(p.astype(BF16), v)
            m_scr[h] = m_new

    @pl.when(kj == pl.num_programs(2) - 1)
    def _():
        for h in range(NSA_HPG):
            o_ref[:, h * NSA_HEAD_DIM:(h + 1) * NSA_HEAD_DIM] = acc_scr[h] / jnp.maximum(l_scr[h], 1e-30)


def _sel_attn(zn, sel, tq, tk):
    T = zn.shape[1]
    ns = sel.shape[2]
    k_tile0 = NSA_HEADS + 2 * NSA_KV_GROUPS
    v_tile0 = k_tile0 + NSA_KV_GROUPS

    def kv_block(i, kj):
        return jnp.minimum(kj, (i * tq + tq - 1) // tk)

    kern = functools.partial(_sel_attn_kernel, tq=tq, tk=tk)
    return pl.pallas_call(
        kern,
        grid=(NSA_KV_GROUPS, T // tq, T // tk),
        in_specs=[
            pl.BlockSpec((NSA_HPG, tq, NSA_HEAD_DIM), lambda g, i, kj: (g, i, 0)),
            pl.BlockSpec((1, tk, NSA_HEAD_DIM), lambda g, i, kj: (k_tile0 + g, kv_block(i, kj), 0)),
            pl.BlockSpec((1, tk, NSA_HEAD_DIM), lambda g, i, kj: (v_tile0 + g, kv_block(i, kj), 0)),
            pl.BlockSpec((1, tq, ns), lambda g, i, kj: (g, i, 0)),
        ],
        out_specs=pl.BlockSpec((tq, NSA_HPG * NSA_HEAD_DIM), lambda g, i, kj: (i, g)),
        out_shape=jax.ShapeDtypeStruct((T, NSA_WIDTH), F32),
        scratch_shapes=[
            pltpu.VMEM((NSA_HPG, tq, 1), F32),
            pltpu.VMEM((NSA_HPG, tq, 1), F32),
            pltpu.VMEM((NSA_HPG, tq, NSA_HEAD_DIM), F32),
        ],
        compiler_params=_cparams(("parallel", "parallel", "arbitrary")),
        name="nsa_sel_attn",
    )(zn, zn, zn, sel)


def _win_kernel(*refs, tq, nwb):
    q_ref = refs[0]
    k_refs = refs[1:1 + nwb]
    v_refs = refs[1 + nwb:1 + 2 * nwb]
    gl_ref, oc_ref, os_ref, o_ref = refs[1 + 2 * nwb:]
    g = pl.program_id(0)
    i = pl.program_id(1)
    q0 = i * tq
    qpos = q0 + lax.broadcasted_iota(jnp.int32, (tq, 1), 0)
    kcat = jnp.concatenate([r[0] for r in k_refs], axis=0)
    vcat = jnp.concatenate([r[0] for r in v_refs], axis=0)
    kpos = q0 - (nwb - 1) * tq + lax.broadcasted_iota(jnp.int32, (1, nwb * tq), 1)
    diff = qpos - kpos
    ok = (diff >= 0) & (diff < WINDOW) & (kpos >= 0)
    gates = jax.nn.sigmoid(gl_ref[0])
    lane = lax.broadcasted_iota(jnp.int32, gates.shape, 1)
    for h in range(NSA_HPG):
        s = jnp.where(ok, _dot_nt(q_ref[h], kcat), NEG)
        mx = jnp.max(s, axis=-1, keepdims=True)
        e = jnp.where(ok, jnp.exp(s - mx), 0.0)
        p = e / jnp.maximum(jnp.sum(e, axis=-1, keepdims=True), 1e-30)
        o_w = _dot(p.astype(BF16), vcat)
        col = g * NSA_HPG + h

        def gate(branch):
            return jnp.sum(jnp.where(lane == branch * NSA_HEADS + col, gates, 0.0), axis=-1, keepdims=True)

        sl = slice(h * NSA_HEAD_DIM, (h + 1) * NSA_HEAD_DIM)
        o_ref[:, sl] = gate(0) * oc_ref[:, sl] + gate(1) * os_ref[:, sl] + gate(2) * o_w


def _win_combine(zn, zr, o_c, o_s, tq):
    T = zn.shape[1]
    nwb = WINDOW // tq + 1
    k_tile0 = NSA_HEADS + 4 * NSA_KV_GROUPS
    v_tile0 = k_tile0 + NSA_KV_GROUPS

    def kv_spec(tile0, d):
        return pl.BlockSpec((1, tq, NSA_HEAD_DIM),
                            lambda g, i: (tile0 + g, jnp.maximum(i - (nwb - 1) + d, 0), 0))

    wide = pl.BlockSpec((tq, NSA_HPG * NSA_HEAD_DIM), lambda g, i: (i, g))
    kern = functools.partial(_win_kernel, tq=tq, nwb=nwb)
    return pl.pallas_call(
        kern,
        grid=(NSA_KV_GROUPS, T // tq),
        in_specs=([pl.BlockSpec((NSA_HPG, tq, NSA_HEAD_DIM), lambda g, i: (g, i, 0))]
                  + [kv_spec(k_tile0, d) for d in range(nwb)]
                  + [kv_spec(v_tile0, d) for d in range(nwb)]
                  + [pl.BlockSpec((1, tq, LANES), lambda g, i: (RW_TILES - 1, i, 0)), wide, wide]),
        out_specs=wide,
        out_shape=jax.ShapeDtypeStruct((T, NSA_WIDTH), F32),
        compiler_params=_cparams(("parallel", "parallel")),
        name="nsa_window_combine",
    )(*([zn] * (1 + 2 * nwb)), zr, o_c, o_s)


def _cmp_to_sel_matrix(ncp, ns):
    cmp_start = np.arange(ncp)[:, None] * CMP_STRIDE
    sel_start = np.arange(ns)[None, :] * SLC_BLOCK
    overlap = np.minimum(cmp_start + CMP_BLOCK, sel_start + SLC_BLOCK) - np.maximum(cmp_start, sel_start)
    return jnp.asarray(np.clip(overlap, 0, None).astype(np.float32) / CMP_BLOCK, dtype=BF16)


def _nsa(zn, zr, cmp_pos_k, cmp_w1_k, cmp_w2_k, cmp_pos_v, cmp_w1_v, cmp_w2_v, tq, tk):
    T = zn.shape[1]
    zn16 = zn.reshape(NSA_TILES, T // CMP_STRIDE, CMP_STRIDE * NSA_HEAD_DIM)
    width = CMP_BLOCK * NSA_HEAD_DIM
    pos = jnp.stack([cmp_pos_k.reshape(1, width), cmp_pos_v.reshape(1, width)])
    pos = jnp.broadcast_to(pos, (2, 8, width)).astype(BF16)
    w1 = jnp.stack([cmp_w1_k, cmp_w1_v]).astype(BF16)
    w2 = jnp.stack([cmp_w2_k, cmp_w2_v]).astype(BF16)
    kv_cmp = _compress(zn16, pos, w1, w2)
    ns = T // SLC_BLOCK
    o_c, sel = _cmp_select(zn, kv_cmp, _cmp_to_sel_matrix(T // CMP_STRIDE, ns), tq)
    o_s = _sel_attn(zn, sel, tq, tk)
    return _win_combine(zn, zr, o_c, o_s, tq)


def _softplus(y):
    return jnp.maximum(y, 0.0) + jnp.log(1.0 + jnp.exp(-jnp.abs(y)))


def _rwkv_kernel(z_ref, prev_ref, mu_ref, pv_ref, wup_hi, wup_lo, aup_hi, aup_lo, gup_hi, gup_lo, o_ref,
                 rt_s, at_s, kh_s, bh_s, kb_s, bb_s, v_s, gc_s, g_s, bonus_s, y_s, state_s, *, tb):
    C = RWKV_CHUNK
    step = pl.program_id(0)

    @pl.when(step == 0)
    def _():
        state_s[...] = jnp.zeros(state_s.shape, F32)

    has_prev = jnp.where(step > 0, 1.0, 0.0)
    row = lax.broadcasted_iota(jnp.int32, (tb, LANES), 0)

    def shifted_mix(t):
        z = z_ref[t]
        zp = jnp.where(row == 0, prev_ref[t, 7:8, :] * has_prev, pltpu.roll(z, 1, axis=0))
        return z + (zp - z) * mu_ref[t]

    ri = lax.broadcasted_iota(jnp.int32, (tb, tb), 0)
    ci = lax.broadcasted_iota(jnp.int32, (tb, tb), 1)
    same_chunk = (ri >> 6) == (ci >> 6)
    tri = jnp.where(same_chunk & (ci <= ri), 1.0, 0.0).astype(BF16)
    tot = jnp.where(same_chunk, 1.0, 0.0).astype(BF16)
    hr = lax.broadcasted_iota(jnp.int32, (LANES, LANES), 0)
    hc = lax.broadcasted_iota(jnp.int32, (LANES, LANES), 1)
    same_head = (hr >> 6) == (hc >> 6)
    head_sum = jnp.where(same_head, 1.0, 0.0).astype(BF16)
    head_mean = jnp.where(same_head, 1.0 / RWKV_HEAD_DIM, 0.0).astype(BF16)

    wa = shifted_mix(3 * RWKV_PAIRS)
    dw = _dot_xw3(jnp.tanh(wa), wup_hi[...], wup_lo[...])
    da = _dot_xw3(wa, aup_hi[...], aup_lo[...])
    g0 = jax.nn.sigmoid(shifted_mix(3 * RWKV_PAIRS + 1))
    g1 = jax.nn.sigmoid(shifted_mix(3 * RWKV_PAIRS + 2))
    g_s[...] = (_dot_xw3(g0, gup_hi[:LANES], gup_lo[:LANES]) + _dot_xw3(g1, gup_hi[LANES:], gup_lo[LANES:]))

    for p in range(RWKV_PAIRS):
        sl = slice(p * LANES, (p + 1) * LANES)
        r = shifted_mix(p)
        k = shifted_mix(RWKV_PAIRS + p)
        v = shifted_mix(2 * RWKV_PAIRS + p)
        w_log = -_softplus(-(pv_ref[0:1, sl] + dw[:, sl])) - 0.5
        lw = -jnp.exp(w_log)
        icl = jax.nn.sigmoid(pv_ref[1:2, sl] + da[:, sl])
        kk = k * pv_ref[2:3, sl]
        kk = kk * lax.rsqrt(jnp.maximum(_dot_x2(kk * kk, head_sum), 1e-24))
        k2 = k * (1.0 + (icl - 1.0) * pv_ref[3:4, sl])
        bonus_s[:, sl] = _dot_x2(r * k2 * pv_ref[4:5, sl], head_sum) * v
        hi, mid, lo = _split3(lw)
        cum = _dot(tri, hi) + _dot(tri, mid) + _dot(tri, lo)
        end = _dot(tot, hi) + _dot(tot, mid) + _dot(tot, lo)
        b = kk * icl
        inv_decay = jnp.exp(-cum)
        to_end = jnp.exp(end - cum)
        rt_s[:, sl] = (r * jnp.exp(cum)).astype(BF16)
        at_s[:, sl] = (-kk * jnp.exp(cum - lw)).astype(BF16)
        kh_s[:, sl] = (k2 * inv_decay).astype(BF16)
        bh_s[:, sl] = (b * inv_decay).astype(BF16)
        kb_s[:, sl] = (k2 * to_end).astype(BF16)
        bb_s[:, sl] = (b * to_end).astype(BF16)
        v_s[:, sl] = v.astype(BF16)
        gc_s[:, sl] = jnp.exp(end)

    t_r = hr & (C - 1)
    t_c = hc & (C - 1)
    strict = same_head & (t_c < t_r)
    incl = same_head & (t_c <= t_r)
    eye = jnp.where(hr == hc, 1.0, 0.0)
    lane_head = lax.broadcasted_iota(jnp.int32, (C, LANES), 1) >> 6

    def stack(zc):
        return jnp.concatenate([jnp.where(lane_head == 0, zc, jnp.zeros_like(zc)),
                                jnp.where(lane_head == 1, zc, jnp.zeros_like(zc))], axis=0)

    def chunk(c, carry):
        t0 = pl.multiple_of(c * C, C)
        rows = pl.ds(t0, C)
        for p in range(RWKV_PAIRS):
            sl = slice(p * LANES, (p + 1) * LANES)
            v_c = v_s[rows, sl]
            lhs = jnp.concatenate([stack(at_s[rows, sl]), stack(rt_s[rows, sl])], axis=0)
            rhs = jnp.concatenate([stack(kh_s[rows, sl]), stack(bh_s[rows, sl])], axis=0)
            aa = _dot_nt(lhs, rhs)
            a_ak = jnp.where(strict, aa[:2 * C, :2 * C], 0.0).astype(BF16)
            n_pow = jnp.where(strict, aa[:2 * C, 2 * C:], 0.0)
            a_rk = jnp.where(incl, aa[2 * C:, :2 * C], 0.0).astype(BF16)
            a_rb = jnp.where(incl, aa[2 * C:, 2 * C:], 0.0).astype(BF16)
            t_inv = eye + n_pow
            for _ in range(5):
                nb = n_pow.astype(BF16)
                n_pow = _dot(nb, nb)
                t_inv = t_inv + _dot(t_inv.astype(BF16), n_pow.astype(BF16))
            state = state_s[p]
            xs = _dot_nt(lhs, state.astype(BF16))
            v_st = stack(v_c)
            sa = _dot(t_inv.astype(BF16), (xs[:2 * C] + _dot(a_ak, v_st)).astype(BF16))
            sa_b = sa.astype(BF16)
            ys = xs[2 * C:] + _dot(jnp.concatenate([a_rk, a_rb], axis=1), jnp.concatenate([v_st, sa_b], axis=0))
            y_s[rows, sl] = ys[:C] + ys[C:]
            sa_t = (sa[:C] + sa[C:]).astype(BF16)
            upd = _dot_tn(jnp.concatenate([v_c, sa_t], axis=0),
                          jnp.concatenate([kb_s[rows, sl], bb_s[rows, sl]], axis=0))
            state_s[p] = state * gc_s[pl.ds(t0, 1), sl] + jnp.where(same_head, upd, 0.0)
        return carry

    lax.fori_loop(0, tb // C, chunk, 0)

    for p in range(RWKV_PAIRS):
        sl = slice(p * LANES, (p + 1) * LANES)
        y = y_s[:, sl]
        d = y - _dot_x2(y, head_mean)
        var = _dot_x2(d * d, head_mean)
        yn = d * lax.rsqrt(var + LNX_EPS) * pv_ref[5:6, sl] + pv_ref[6:7, sl]
        o_ref[:, sl] = (yn + bonus_s[:, sl]) * g_s[:, sl]


def _hi_lo(w):
    hi = w.astype(BF16)
    return hi, (w - hi.astype(F32)).astype(BF16)


def _rwkv(zr, mu_t, w0, w_up, a0, a_up, g_up, k_k, k_a, r_k, lnx_g, lnx_b, tb):
    T = zr.shape[1]
    n_in = RW_TILES - 1
    pv = jnp.stack([w0, a0, k_k, k_a, r_k.reshape(-1), lnx_g, lnx_b, jnp.zeros_like(w0)])
    wup = _hi_lo(jnp.pad(w_up, ((0, LANES - W_RANK), (0, 0))))
    aup = _hi_lo(jnp.pad(a_up, ((W_RANK, LANES - W_RANK - A_RANK), (0, 0))))
    gup = _hi_lo(jnp.pad(g_up, ((0, 2 * LANES - G_RANK), (0, 0))))
    full = lambda shape: pl.BlockSpec(shape, lambda s: (0,) * len(shape))
    kern = functools.partial(_rwkv_kernel, tb=tb)
    bf = lambda: pltpu.VMEM((tb, RWKV_WIDTH), BF16)
    ff = lambda: pltpu.VMEM((tb, RWKV_WIDTH), F32)
    return pl.pallas_call(
        kern,
        grid=(T // tb,),
        in_specs=[
            pl.BlockSpec((n_in, tb, LANES), lambda s: (0, s, 0)),
            pl.BlockSpec((n_in, 8, LANES), lambda s: (0, jnp.maximum(s * (tb // 8) - 1, 0), 0)),
            full((n_in, 1, LANES)),
            full((8, RWKV_WIDTH)),
            full((LANES, RWKV_WIDTH)), full((LANES, RWKV_WIDTH)),
            full((LANES, RWKV_WIDTH)), full((LANES, RWKV_WIDTH)),
            full((2 * LANES, RWKV_WIDTH)), full((2 * LANES, RWKV_WIDTH)),
        ],
        out_specs=pl.BlockSpec((tb, RWKV_WIDTH), lambda s: (s, 0)),
        out_shape=jax.ShapeDtypeStruct((T, RWKV_WIDTH), F32),
        scratch_shapes=[bf(), bf(), bf(), bf(), bf(), bf(), bf(), ff(), ff(), ff(), ff(),
                        pltpu.VMEM((RWKV_PAIRS, LANES, LANES), F32)],
        compiler_params=_cparams(("arbitrary",)),
        name="rwkv7",
    )(zr, zr, mu_t, pv, *wup, *aup, *gup)


def _mix_out_kernel(ya_ref, yb_ref, g_ref, w_ref, x_ref, o_ref, h_scr):
    @pl.when(pl.program_id(1) == 0)
    def _():
        h_scr[:, :NSA_WIDTH] = _rms(ya_ref[...], g_ref[...]).astype(BF16)
        h_scr[:, NSA_WIDTH:] = yb_ref[...].astype(BF16)

    o_ref[...] = x_ref[...] + _dot(h_scr[...], w_ref[...])


def _mix_out(y_a, y_b, g, w, x, tm, tn):
    T = x.shape[0]
    return pl.pallas_call(
        _mix_out_kernel,
        grid=(T // tm, D_MODEL // tn),
        in_specs=[
            pl.BlockSpec((tm, NSA_WIDTH), lambda m, n: (m, 0)),
            pl.BlockSpec((tm, RWKV_WIDTH), lambda m, n: (m, 0)),
            pl.BlockSpec((1, NSA_WIDTH), lambda m, n: (0, 0)),
            pl.BlockSpec((NSA_WIDTH + RWKV_WIDTH, tn), lambda m, n: (0, n)),
            pl.BlockSpec((tm, tn), lambda m, n: (m, n)),
        ],
        out_specs=pl.BlockSpec((tm, tn), lambda m, n: (m, n)),
        out_shape=jax.ShapeDtypeStruct((T, D_MODEL), F32),
        scratch_shapes=[pltpu.VMEM((tm, NSA_WIDTH + RWKV_WIDTH), BF16)],
        compiler_params=_cparams(("parallel", "arbitrary")),
        name="mix_out_proj",
    )(y_a, y_b, g, w, x)


def _norm_mm_kernel(x_ref, g_ref, w_ref, o_ref, h_scr):
    @pl.when(pl.program_id(1) == 0)
    def _():
        h_scr[...] = _rms(x_ref[...], g_ref[...]).astype(BF16)

    o_ref[...] = _dot(h_scr[...], w_ref[...]).astype(o_ref.dtype)


def _norm_mm(x, g, w, tm, tn, name):
    M, K = x.shape
    N = w.shape[1]
    return pl.pallas_call(
        _norm_mm_kernel,
        grid=(M // tm, N // tn),
        in_specs=[
            pl.BlockSpec((tm, K), lambda m, n: (m, 0)),
            pl.BlockSpec((1, K), lambda m, n: (0, 0)),
            pl.BlockSpec((K, tn), lambda m, n: (0, n)),
        ],
        out_specs=pl.BlockSpec((tm, tn), lambda m, n: (m, n)),
        out_shape=jax.ShapeDtypeStruct((M, N), BF16),
        scratch_shapes=[pltpu.VMEM((tm, K), BF16)],
        compiler_params=_cparams(("parallel", "arbitrary")),
        name=name,
    )(x, g, w)


def _mm_res_kernel(a_ref, w_ref, x_ref, o_ref):
    o_ref[...] = x_ref[...] + _dot(a_ref[...], w_ref[...])


def _mm_res(a, w, x, tm, tn, name):
    M, K = a.shape
    N = w.shape[1]
    return pl.pallas_call(
        _mm_res_kernel,
        grid=(M // tm, N // tn),
        in_specs=[
            pl.BlockSpec((tm, K), lambda m, n: (m, 0)),
            pl.BlockSpec((K, tn), lambda m, n: (0, n)),
            pl.BlockSpec((tm, tn), lambda m, n: (m, n)),
        ],
        out_specs=pl.BlockSpec((tm, tn), lambda m, n: (m, n)),
        out_shape=jax.ShapeDtypeStruct((M, N), F32),
        compiler_params=_cparams(("parallel", "arbitrary")),
        name=name,
    )(a, w, x)


def _mem_attn_kernel(q_ref, k_ref, v_ref, o_ref):
    scale = MEM_HEAD_DIM ** -0.5
    for h in range(MEM_HEADS):
        sl = slice(h * MEM_HEAD_DIM, (h + 1) * MEM_HEAD_DIM)
        s = _dot_nt(q_ref[:, sl], k_ref[:, sl]) * scale
        e = jnp.exp(s - jnp.max(s, axis=-1, keepdims=True))
        p = e / jnp.sum(e, axis=-1, keepdims=True)
        o_ref[:, sl] = _dot(p.astype(BF16), v_ref[:, sl]).astype(o_ref.dtype)


def _mem_attn(q, k, v, tm):
    T = q.shape[0]
    M = k.shape[0]
    return pl.pallas_call(
        _mem_attn_kernel,
        grid=(T // tm,),
        in_specs=[
            pl.BlockSpec((tm, D_MODEL), lambda m: (m, 0)),
            pl.BlockSpec((M, D_MODEL), lambda m: (0, 0)),
            pl.BlockSpec((M, D_MODEL), lambda m: (0, 0)),
        ],
        out_specs=pl.BlockSpec((tm, D_MODEL), lambda m: (m, 0)),
        out_shape=jax.ShapeDtypeStruct((T, D_MODEL), BF16),
        compiler_params=_cparams(("parallel",)),
        name="mem_attn",
    )(q, k, v)


def _router_kernel(x_ref, g_ref, w_hi, w_lo, b_ref, h_ref, c_ref):
    h = _rms(x_ref[...], g_ref[...])
    h_ref[...] = h.astype(BF16)
    logits = _dot_xw3(h, w_hi[...], w_lo[...]) + b_ref[...]
    lane = lax.broadcasted_iota(jnp.int32, logits.shape, 1)
    big = jnp.int32(LANES)
    is_grp = (lane >= N_EXPERTS) & (lane < N_EXPERTS + N_GROUPS)
    lg = jnp.where(is_grp, logits, NEG)
    eg = jnp.where(is_grp, jnp.exp(lg - jnp.max(lg, axis=-1, keepdims=True)), 0.0)
    pg = eg / jnp.sum(eg, axis=-1, keepdims=True)
    pg_top = jnp.max(pg, axis=-1, keepdims=True)
    g_idx = jnp.min(jnp.where(is_grp & (pg == pg_top), lane - N_EXPERTS, big), axis=-1, keepdims=True)
    in_grp = (lane < N_EXPERTS) & ((lane >> 3) == g_idx)
    le = jnp.where(in_grp, logits, NEG)
    ee = jnp.where(in_grp, jnp.exp(le - jnp.max(le, axis=-1, keepdims=True)), 0.0)
    pe = jnp.where(in_grp, ee / jnp.sum(ee, axis=-1, keepdims=True), -1.0)
    p1 = jnp.max(pe, axis=-1, keepdims=True)
    hit1 = lane == jnp.min(jnp.where(pe == p1, lane, big), axis=-1, keepdims=True)
    pe2 = jnp.where(hit1, -1.0, pe)
    p2 = jnp.max(pe2, axis=-1, keepdims=True)
    hit2 = lane == jnp.min(jnp.where(pe2 == p2, lane, big), axis=-1, keepdims=True)
    denom = p1 + p2
    c_ref[...] = jnp.where(hit1, pg_top * p1 / denom, 0.0) + jnp.where(hit2, pg_top * p2 / denom, 0.0)


def _router(x, g, w_hi, w_lo, b, tm):
    T = x.shape[0]
    return pl.pallas_call(
        _router_kernel,
        grid=(T // tm,),
        in_specs=[
            pl.BlockSpec((tm, D_MODEL), lambda m: (m, 0)),
            pl.BlockSpec((1, D_MODEL), lambda m: (0, 0)),
            pl.BlockSpec((D_MODEL, LANES), lambda m: (0, 0)),
            pl.BlockSpec((D_MODEL, LANES), lambda m: (0, 0)),
            pl.BlockSpec((1, LANES), lambda m: (0, 0)),
        ],
        out_specs=[
            pl.BlockSpec((tm, D_MODEL), lambda m: (m, 0)),
            pl.BlockSpec((tm, LANES), lambda m: (m, 0)),
        ],
        out_shape=[
            jax.ShapeDtypeStruct((T, D_MODEL), BF16),
            jax.ShapeDtypeStruct((T, LANES), F32),
        ],
        compiler_params=_cparams(("parallel",)),
        name="moe_router",
    )(x, g, w_hi, w_lo, b)


def _moe_kernel(h_ref, c_ref, wg_ref, wu_ref, wd_ref, x_ref, gf_ref, o_ref, acc_scr):
    e = pl.program_id(1)

    @pl.when(e == 0)
    def _():
        acc_scr[...] = jnp.zeros(acc_scr.shape, F32)

    h = h_ref[...]
    comb = c_ref[...]
    lane = lax.broadcasted_iota(jnp.int32, comb.shape, 1)
    c_e = jnp.sum(jnp.where(lane == e, comb, 0.0), axis=-1, keepdims=True)
    hid = jax.nn.silu(_dot(h, wg_ref[0])) * _dot(h, wu_ref[0])
    acc_scr[...] += _dot((hid * c_e).astype(BF16), wd_ref[0])

    @pl.when(e == pl.num_programs(1) - 1)
    def _():
        o_ref[...] = _rms(x_ref[...] + acc_scr[...], gf_ref[...])


def _moe(h, comb, wg, wu, wd, x, gf, tm):
    T = x.shape[0]
    return pl.pallas_call(
        _moe_kernel,
        grid=(T // tm, N_EXPERTS),
        in_specs=[
            pl.BlockSpec((tm, D_MODEL), lambda m, e: (m, 0)),
            pl.BlockSpec((tm, LANES), lambda m, e: (m, 0)),
            pl.BlockSpec((1, D_MODEL, EXPERT_FF), lambda m, e: (e, 0, 0)),
            pl.BlockSpec((1, D_MODEL, EXPERT_FF), lambda m, e: (e, 0, 0)),
            pl.BlockSpec((1, EXPERT_FF, D_MODEL), lambda m, e: (e, 0, 0)),
            pl.BlockSpec((tm, D_MODEL), lambda m, e: (m, 0)),
            pl.BlockSpec((1, D_MODEL), lambda m, e: (0, 0)),
        ],
        out_specs=pl.BlockSpec((tm, D_MODEL), lambda m, e: (m, 0)),
        out_shape=jax.ShapeDtypeStruct((T, D_MODEL), F32),
        scratch_shapes=[pltpu.VMEM((tm, D_MODEL), F32)],
        compiler_params=_cparams(("parallel", "arbitrary")),
        name="moe_experts",
    )(h, comb, wg, wu, wd, x, gf)


def _pack_w_in(w_in, rwkv_mu):
    nsa_cols = NSA_WIDTH + 6 * NSA_KV_WIDTH
    gl = w_in[:, nsa_cols:nsa_cols + 3 * NSA_HEADS]
    rw0 = nsa_cols + 3 * NSA_HEADS
    rkv = w_in[:, rw0:rw0 + 3 * RWKV_WIDTH + W_RANK + A_RANK]
    gd = w_in[:, rw0 + 3 * RWKV_WIDTH + W_RANK + A_RANK:]

    def pad(a, n):
        return jnp.pad(a, ((0, 0), (0, n - a.shape[1])))

    w = jnp.concatenate([w_in[:, :nsa_cols], rkv, pad(gd, 2 * LANES), pad(gl, LANES)], axis=1)
    mu = rwkv_mu.reshape(1, -1)
    mu_rkv = mu[:, :3 * RWKV_WIDTH + W_RANK + A_RANK]
    mu_gd = pad(mu[:, 3 * RWKV_WIDTH + W_RANK + A_RANK:], 2 * LANES)
    mu_t = jnp.concatenate([mu_rkv, mu_gd], axis=1).reshape(RW_TILES - 1, 1, LANES)
    return w.astype(BF16), mu_t


def kernel(x, mem, ln1_g, w_in, cmp_pos_k, cmp_w1_k, cmp_w2_k, cmp_pos_v, cmp_w1_v, cmp_w2_v, nsa_norm_g, rwkv_mu, rwkv_w0, rwkv_w_up, rwkv_a0, rwkv_a_up, rwkv_g_up, rwkv_k_k, rwkv_k_a, rwkv_r_k, rwkv_lnx_g, rwkv_lnx_b, w_out, ln_mem_g, ln2_g, wq_mem, wk_mem, wv_mem, wo_mem, ln3_g, router_group_w, router_group_b, router_expert_w, router_expert_b, moe_w_gate, moe_w_up, moe_w_down, lnf_g):
    B, T, _ = x.shape
    assert B == 1 and T % 512 == 0 and w_in.shape[0] == 1
    tm = 512
    row = lambda a: a.reshape(1, -1)
    xs = x[0]
    for l in range(w_in.shape[0]):
        wp, mu_t = _pack_w_in(w_in[l], rwkv_mu[l])
        zn, zr = _in_proj(xs, row(ln1_g[l]), wp, tm)
        y_a = _nsa(zn, zr, cmp_pos_k[l], cmp_w1_k[l], cmp_w2_k[l], cmp_pos_v[l], cmp_w1_v[l], cmp_w2_v[l],
                   128, 512)
        y_b = _rwkv(zr, mu_t, rwkv_w0[l], rwkv_w_up[l], rwkv_a0[l], rwkv_a_up[l], rwkv_g_up[l], rwkv_k_k[l],
                    rwkv_k_a[l], rwkv_r_k[l], rwkv_lnx_g[l], rwkv_lnx_b[l], 256)
        xs = _mix_out(y_a, y_b, row(nsa_norm_g[l]), w_out[l].astype(BF16), xs, tm, 512)
        m_tok = mem[0]
        k_mem = _norm_mm(m_tok, row(ln_mem_g[l]), wk_mem[l].astype(BF16), m_tok.shape[0], 512, "mem_k")
        v_mem = _norm_mm(m_tok, row(ln_mem_g[l]), wv_mem[l].astype(BF16), m_tok.shape[0], 512, "mem_v")
        q_mem = _norm_mm(xs, row(ln2_g[l]), wq_mem[l].astype(BF16), tm, 512, "mem_q")
        o_mem = _mem_attn(q_mem, k_mem, v_mem, tm)
        xs = _mm_res(o_mem, wo_mem[l].astype(BF16), xs, tm, 512, "mem_out_proj")
        w_r = jnp.pad(jnp.concatenate([router_expert_w[l], router_group_w[l]], axis=1),
                      ((0, 0), (0, LANES - N_EXPERTS - N_GROUPS)))
        b_r = jnp.pad(jnp.concatenate([router_expert_b[l], router_group_b[l]]), (0, LANES - N_EXPERTS - N_GROUPS))
        h3, comb = _router(xs, row(ln3_g[l]), *_hi_lo(w_r), row(b_r), tm)
        xs = _moe(h3, comb, moe_w_gate[l].astype(BF16), moe_w_up[l].astype(BF16), moe_w_down[l].astype(BF16),
                  xs, row(lnf_g), tm)
    return xs[None]
```

```python
import functools

import numpy as np
import jax
import jax.numpy as jnp
from jax import lax
from jax.experimental import pallas as pl
from jax.experimental.pallas import tpu as pltpu

F32 = jnp.float32
BF16 = jnp.bfloat16

LANES = 128
D_MODEL = 2048
EPS = 1e-6
NSA_HEAD_DIM = 128
NSA_HEADS = 8
NSA_KV_GROUPS = 2
NSA_HPG = NSA_HEADS // NSA_KV_GROUPS
NSA_WIDTH = NSA_HEADS * NSA_HEAD_DIM
NSA_KV_WIDTH = NSA_KV_GROUPS * NSA_HEAD_DIM
CMP_BLOCK = 32
CMP_STRIDE = 16
SLC_BLOCK = 64
SLC_TOPK = 16
WINDOW = 512
NEG = -1e30
RWKV_HEAD_DIM = 64
RWKV_HEADS = 16
RWKV_WIDTH = RWKV_HEADS * RWKV_HEAD_DIM
RWKV_PAIRS = RWKV_WIDTH // LANES
W_RANK = 64
A_RANK = 64
G_RANK = 160
LNX_EPS = 64e-5
RWKV_CHUNK = 64
MEM_HEADS = 4
MEM_HEAD_DIM = D_MODEL // MEM_HEADS
N_GROUPS = 4
EXPERTS_PER_GROUP = 8
N_EXPERTS = N_GROUPS * EXPERTS_PER_GROUP
EXPERT_FF = 256

NSA_TILES = 20
RW_TILES = 28
IN_TILE_BLOCK = 4

VMEM_LIMIT = 56 * 1024 * 1024


def _cparams(sem):
    return pltpu.CompilerParams(dimension_semantics=sem, vmem_limit_bytes=VMEM_LIMIT)


def _dot(a, b):
    return jnp.dot(a, b, preferred_element_type=F32)


def _dot_nt(a, b):
    return lax.dot_general(a, b, (((1,), (1,)), ((), ())), preferred_element_type=F32)


def _dot_tn(a, b):
    return lax.dot_general(a, b, (((0,), (0,)), ((), ())), preferred_element_type=F32)


def _split2(x):
    hi = x.astype(BF16)
    lo = (x - hi.astype(F32)).astype(BF16)
    return hi, lo


def _split3(x):
    hi = x.astype(BF16)
    r1 = x - hi.astype(F32)
    mid = r1.astype(BF16)
    lo = (r1 - mid.astype(F32)).astype(BF16)
    return hi, mid, lo


def _dot_x2(x, w):
    hi, lo = _split2(x)
    return _dot(hi, w) + _dot(lo, w)


def _dot_x3(x, w):
    hi, mid, lo = _split3(x)
    return _dot(hi, w) + _dot(mid, w) + _dot(lo, w)


def _dot_xw3(x, w_hi, w_lo):
    hi, lo = _split2(x)
    return _dot(hi, w_hi) + _dot(lo, w_hi) + _dot(hi, w_lo)


def _rms(x, g):
    ms = jnp.mean(x * x, axis=-1, keepdims=True)
    return x * lax.rsqrt(ms + EPS) * g


def _in_proj_kernel(x_ref, g_ref, w_ref, zn_ref, zr_ref, h_scr, *, nsa_blocks, q_blocks, q_scale):
    n = pl.program_id(1)

    @pl.when(n == 0)
    def _():
        h_scr[...] = _rms(x_ref[...], g_ref[...]).astype(BF16)

    z = _dot(h_scr[...], w_ref[...])

    @pl.when(n < nsa_blocks)
    def _():
        zs = z * jnp.where(n < q_blocks, q_scale, 1.0)
        for t in range(IN_TILE_BLOCK):
            zn_ref[t] = zs[:, t * LANES:(t + 1) * LANES].astype(BF16)

    @pl.when(n >= nsa_blocks)
    def _():
        for t in range(IN_TILE_BLOCK):
            zr_ref[t] = z[:, t * LANES:(t + 1) * LANES]


def _in_proj(x, g, w, tm):
    T = x.shape[0]
    nb = IN_TILE_BLOCK * LANES
    nsa_blocks = NSA_TILES // IN_TILE_BLOCK
    n_blocks = (NSA_TILES + RW_TILES) // IN_TILE_BLOCK
    kern = functools.partial(_in_proj_kernel, nsa_blocks=nsa_blocks,
                             q_blocks=NSA_HEADS // IN_TILE_BLOCK, q_scale=NSA_HEAD_DIM ** -0.5)
    return pl.pallas_call(
        kern,
        grid=(T // tm, n_blocks),
        in_specs=[
            pl.BlockSpec((tm, D_MODEL), lambda m, n: (m, 0)),
            pl.BlockSpec((1, D_MODEL), lambda m, n: (0, 0)),
            pl.BlockSpec((D_MODEL, nb), lambda m, n: (0, n)),
        ],
        out_specs=[
            pl.BlockSpec((IN_TILE_BLOCK, tm, LANES), lambda m, n: (jnp.minimum(n, nsa_blocks - 1), m, 0)),
            pl.BlockSpec((IN_TILE_BLOCK, tm, LANES), lambda m, n: (jnp.maximum(n - nsa_blocks, 0), m, 0)),
        ],
        out_shape=[
            jax.ShapeDtypeStruct((NSA_TILES, T, LANES), BF16),
            jax.ShapeDtypeStruct((RW_TILES, T, LANES), F32),
        ],
        scratch_shapes=[pltpu.VMEM((tm, D_MODEL), BF16)],
        compiler_params=_cparams(("parallel", "arbitrary")),
        name="in_proj",
    )(x, g, w)


def _compress_kernel(x_ref, pos_ref, w1_ref, w2_ref, o_ref):
    half = CMP_STRIDE * NSA_HEAD_DIM
    x = x_ref[0]
    w1 = w1_ref[0]
    h_first = _dot(x, w1[:half])
    h_second = _dot(x, w1[half:])
    n = x.shape[0]
    h_next = pltpu.roll(h_second, n - 1, axis=0)
    bias = _dot(pos_ref[0], w1)[0:1]
    hid = jax.nn.gelu(h_first + h_next + bias)
    o_ref[0] = _dot(hid.astype(BF16), w2_ref[0]).astype(o_ref.dtype)


def _compress(zn16, pos, w1, w2):
    n = zn16.shape[1]
    width = CMP_BLOCK * NSA_HEAD_DIM
    return pl.pallas_call(
        _compress_kernel,
        grid=(4,),
        in_specs=[
            pl.BlockSpec((1, n, CMP_STRIDE * NSA_HEAD_DIM), lambda s: (NSA_HEADS + s, 0, 0)),
            pl.BlockSpec((1, 8, width), lambda s: (s // 2, 0, 0)),
            pl.BlockSpec((1, width, NSA_HEAD_DIM), lambda s: (s // 2, 0, 0)),
            pl.BlockSpec((1, NSA_HEAD_DIM, NSA_HEAD_DIM), lambda s: (s // 2, 0, 0)),
        ],
        out_specs=pl.BlockSpec((1, n, NSA_HEAD_DIM), lambda s: (s, 0, 0)),
        out_shape=jax.ShapeDtypeStruct((4, n, NSA_HEAD_DIM), BF16),
        compiler_params=_cparams(("parallel",)),
        name="nsa_compress",
    )(zn16, pos, w1, w2)


def _cmp_select_kernel(q_ref, kc_ref, vc_ref, m_ref, oc_ref, sel_ref, *, tq, top_n):
    q0 = pl.program_id(1) * tq
    rows = NSA_HPG * tq
    ncp = kc_ref.shape[1]
    ns = m_ref.shape[0]
    q = q_ref[...].reshape(rows, NSA_HEAD_DIM)
    s = _dot_nt(q, kc_ref[0])
    qpos = q0 + (lax.broadcasted_iota(jnp.int32, (rows, 1), 0) & (tq - 1))
    cmp_end = lax.broadcasted_iota(jnp.int32, (1, ncp), 1) * CMP_STRIDE + (CMP_BLOCK - 1)
    mask = cmp_end <= qpos
    s = jnp.where(mask, s, NEG)
    mx = jnp.max(s, axis=-1, keepdims=True)
    e = jnp.where(mask, jnp.exp(s - mx), 0.0)
    p = e / jnp.maximum(jnp.sum(e, axis=-1, keepdims=True), 1e-30)
    o = _dot(p.astype(BF16), vc_ref[0])
    for h in range(NSA_HPG):
        oc_ref[:, h * NSA_HEAD_DIM:(h + 1) * NSA_HEAD_DIM] = o[h * tq:(h + 1) * tq]
    psum = p[0:tq]
    for h in range(1, NSA_HPG):
        psum = psum + p[h * tq:(h + 1) * tq]
    ps_hi, ps_lo = _split2(psum)
    p_sel = _dot_nt(m_ref[...], ps_hi) + _dot_nt(m_ref[...], ps_lo)
    blk = (q0 + lax.broadcasted_iota(jnp.int32, (1, tq), 1)) >> 6
    j = lax.broadcasted_iota(jnp.int32, (ns, tq), 0)
    future = j > blk
    forced = (j == 0) | (j == blk) | (j == blk - 1)
    score = jnp.where(future, -1.0, jnp.where(forced, 1e6, p_sel))
    sel = jnp.zeros((ns, tq), F32)
    for _ in range(top_n):
        best = jnp.max(score, axis=0, keepdims=True)
        idx = jnp.min(jnp.where(score == best, j, ns), axis=0, keepdims=True)
        hit = j == idx
        sel = jnp.where(hit & (best >= 0.0), 1.0, sel)
        score = jnp.where(hit, -2.0, score)
    sel_ref[0] = sel.astype(sel_ref.dtype)


def _cmp_select(zn, kv_cmp, cmp_to_sel, tq):
    T = zn.shape[1]
    ncp = kv_cmp.shape[1]
    ns = cmp_to_sel.shape[0]
    kern = functools.partial(_cmp_select_kernel, tq=tq, top_n=min(SLC_TOPK, ns))
    return pl.pallas_call(
        kern,
        grid=(NSA_KV_GROUPS, T // tq),
        in_specs=[
            pl.BlockSpec((NSA_HPG, tq, NSA_HEAD_DIM), lambda g, i: (g, i, 0)),
            pl.BlockSpec((1, ncp, NSA_HEAD_DIM), lambda g, i: (g, 0, 0)),
            pl.BlockSpec((1, ncp, NSA_HEAD_DIM), lambda g, i: (NSA_KV_GROUPS + g, 0, 0)),
            pl.BlockSpec((ns, ncp), lambda g, i: (0, 0)),
        ],
        out_specs=[
            pl.BlockSpec((tq, NSA_HPG * NSA_HEAD_DIM), lambda g, i: (i, g)),
            pl.BlockSpec((1, ns, tq), lambda g, i: (g, 0, i)),
        ],
        out_shape=[
            jax.ShapeDtypeStruct((T, NSA_WIDTH), F32),
            jax.ShapeDtypeStruct((NSA_KV_GROUPS, ns, T), BF16),
        ],
        compiler_params=_cparams(("parallel", "parallel")),
        name="nsa_cmp_select",
    )(zn, kv_cmp, kv_cmp, cmp_to_sel)


SEL_ONES_ROWS = 16


def _sel_attn_kernel(q_ref, k_ref, vt_ref, sel_ref, o_ref, bias_scr, s_a, s_b, p_a, p_b, acc_scr, m_scr,
                     *, tq, tk, wb):
    q0 = pl.program_id(1) * tq
    cols = NSA_HPG * tq
    tpw = wb * SLC_BLOCK // tk
    not_chosen = (sel_ref[0].astype(F32) - 1.0) * (-NEG)
    q_t = []
    for h in range(NSA_HPG):
        bias_scr[:, h * tq:(h + 1) * tq] = not_chosen
        q_t.append(q_ref[h].astype(F32).T.astype(BF16))
    q_t = jnp.concatenate(q_t, axis=1)

    def q_window(w):
        rows = pl.ds(pl.multiple_of(w * wb, wb), wb)
        return jnp.concatenate([q_t, bias_scr[rows, :].astype(BF16)], axis=0)

    def scores(j, q_aug):
        return _dot(k_ref[0, pl.ds(pl.multiple_of(j * tk, tk), tk), :], q_aug)

    def pv(j, p_ref):
        return _dot(vt_ref[0, :, pl.ds(pl.multiple_of(j * tk, tk), tk)], p_ref[...])

    def softmax_tile(s_ref, p_ref):
        s = s_ref[...]
        m_old = m_scr[...]
        m_new = jnp.maximum(m_old, jnp.max(s, axis=0, keepdims=True))
        m_scr[...] = m_new
        p_ref[...] = jnp.exp((s - m_new).astype(BF16))
        return jnp.exp(m_old - m_new)

    n_full = q0 // tk
    s = scores(n_full, q_window(n_full // tpw))
    kpos = n_full * tk + lax.broadcasted_iota(jnp.int32, (tk, 1), 0)
    qpos = q0 + (lax.broadcasted_iota(jnp.int32, (1, cols), 1) & (tq - 1))
    s = jnp.where(kpos <= qpos, s, NEG)
    m_first = jnp.max(s, axis=0, keepdims=True)
    m_scr[...] = m_first
    p_a[...] = jnp.exp((s - m_first).astype(BF16))
    acc_scr[...] = pv(n_full, p_a)

    def window(w, carry):
        lo = w * tpw
        cnt = jnp.minimum(n_full - lo, tpw)
        q_aug = q_window(w)
        s_a[...] = scores(lo, q_aug)
        p_b[...] = jnp.zeros(p_b.shape, BF16)

        def pair(t, alpha_prev):
            a = lo + 2 * t
            s_b[...] = scores(a + 1, q_aug)
            acc_scr[...] = alpha_prev * acc_scr[...] + pv(jnp.maximum(a - 1, 0), p_b)
            alpha_a = softmax_tile(s_a, p_a)
            s_a[...] = scores(a + 2, q_aug)
            acc_scr[...] = alpha_a * acc_scr[...] + pv(a, p_a)
            return softmax_tile(s_b, p_b)

        pairs = cnt // 2
        alpha_last = lax.fori_loop(0, pairs, pair, jnp.ones((1, cols), F32))
        acc_scr[...] = alpha_last * acc_scr[...] + pv(jnp.maximum(lo + 2 * pairs - 1, 0), p_b)

        @pl.when(cnt % 2 == 1)
        def _():
            alpha = softmax_tile(s_a, p_a)
            acc_scr[...] = alpha * acc_scr[...] + pv(lo + cnt - 1, p_a)

        return carry

    lax.fori_loop(0, (n_full + tpw - 1) // tpw, window, 0)
    acc = acc_scr[...]
    o_t = acc[:NSA_HEAD_DIM] / acc[NSA_HEAD_DIM:NSA_HEAD_DIM + 1]
    for h in range(NSA_HPG):
        o_ref[:, h * NSA_HEAD_DIM:(h + 1) * NSA_HEAD_DIM] = o_t[:, h * tq:(h + 1) * tq].T


def _sel_attn(zn, k_aug, vt_aug, sel, tq, tk):
    T = zn.shape[1]
    ns = sel.shape[1]
    wb = k_aug.shape[2] - NSA_HEAD_DIM
    assert tk % tq == 0 and (wb * SLC_BLOCK) % (2 * tk) == 0 and ns % wb == 0
    kern = functools.partial(_sel_attn_kernel, tq=tq, tk=tk, wb=wb)
    cols = NSA_HPG * tq
    return pl.pallas_call(
        kern,
        grid=(NSA_KV_GROUPS, T // tq),
        in_specs=[
            pl.BlockSpec((NSA_HPG, tq, NSA_HEAD_DIM), lambda g, i: (g, i, 0)),
            pl.BlockSpec((1, T, NSA_HEAD_DIM + wb), lambda g, i: (g, 0, 0)),
            pl.BlockSpec((1, NSA_HEAD_DIM + SEL_ONES_ROWS, T), lambda g, i: (g, 0, 0)),
            pl.BlockSpec((1, ns, tq), lambda g, i: (g, 0, i)),
        ],
        out_specs=pl.BlockSpec((tq, NSA_HPG * NSA_HEAD_DIM), lambda g, i: (i, g)),
        out_shape=jax.ShapeDtypeStruct((T, NSA_WIDTH), F32),
        scratch_shapes=[
            pltpu.VMEM((ns, cols), F32),
            pltpu.VMEM((tk, cols), F32), pltpu.VMEM((tk, cols), F32),
            pltpu.VMEM((tk, cols), BF16), pltpu.VMEM((tk, cols), BF16),
            pltpu.VMEM((NSA_HEAD_DIM + SEL_ONES_ROWS, cols), F32),
            pltpu.VMEM((1, cols), F32),
        ],
        compiler_params=_cparams(("parallel", "arbitrary")),
        name="nsa_sel_attn",
    )(zn, k_aug, vt_aug, sel)


def _win_kernel(*refs, tq, nwb):
    q_ref = refs[0]
    k_refs = refs[1:1 + nwb]
    v_refs = refs[1 + nwb:1 + 2 * nwb]
    gl_ref, oc_ref, os_ref, o_ref = refs[1 + 2 * nwb:]
    g = pl.program_id(0)
    i = pl.program_id(1)
    q0 = i * tq
    qpos = q0 + lax.broadcasted_iota(jnp.int32, (tq, 1), 0)
    kcat = jnp.concatenate([r[0] for r in k_refs], axis=0)
    vcat = jnp.concatenate([r[0] for r in v_refs], axis=0)
    kpos = q0 - (nwb - 1) * tq + lax.broadcasted_iota(jnp.int32, (1, nwb * tq), 1)
    diff = qpos - kpos
    ok = (diff >= 0) & (diff < WINDOW) & (kpos >= 0)
    gates = jax.nn.sigmoid(gl_ref[0])
    lane = lax.broadcasted_iota(jnp.int32, gates.shape, 1)
    for h in range(NSA_HPG):
        s = jnp.where(ok, _dot_nt(q_ref[h], kcat), NEG)
        mx = jnp.max(s, axis=-1, keepdims=True)
        e = jnp.where(ok, jnp.exp(s - mx), 0.0)
        p = e / jnp.maximum(jnp.sum(e, axis=-1, keepdims=True), 1e-30)
        o_w = _dot(p.astype(BF16), vcat)
        col = g * NSA_HPG + h

        def gate(branch):
            return jnp.sum(jnp.where(lane == branch * NSA_HEADS + col, gates, 0.0), axis=-1, keepdims=True)

        sl = slice(h * NSA_HEAD_DIM, (h + 1) * NSA_HEAD_DIM)
        o_ref[:, sl] = gate(0) * oc_ref[:, sl] + gate(1) * os_ref[:, sl] + gate(2) * o_w


def _win_combine(zn, zr, o_c, o_s, tq):
    T = zn.shape[1]
    nwb = WINDOW // tq + 1
    k_tile0 = NSA_HEADS + 4 * NSA_KV_GROUPS
    v_tile0 = k_tile0 + NSA_KV_GROUPS

    def kv_spec(tile0, d):
        return pl.BlockSpec((1, tq, NSA_HEAD_DIM),
                            lambda g, i: (tile0 + g, jnp.maximum(i - (nwb - 1) + d, 0), 0))

    wide = pl.BlockSpec((tq, NSA_HPG * NSA_HEAD_DIM), lambda g, i: (i, g))
    kern = functools.partial(_win_kernel, tq=tq, nwb=nwb)
    return pl.pallas_call(
        kern,
        grid=(NSA_KV_GROUPS, T // tq),
        in_specs=([pl.BlockSpec((NSA_HPG, tq, NSA_HEAD_DIM), lambda g, i: (g, i, 0))]
                  + [kv_spec(k_tile0, d) for d in range(nwb)]
                  + [kv_spec(v_tile0, d) for d in range(nwb)]
                  + [pl.BlockSpec((1, tq, LANES), lambda g, i: (RW_TILES - 1, i, 0)), wide, wide]),
        out_specs=wide,
        out_shape=jax.ShapeDtypeStruct((T, NSA_WIDTH), F32),
        compiler_params=_cparams(("parallel", "parallel")),
        name="nsa_window_combine",
    )(*([zn] * (1 + 2 * nwb)), zr, o_c, o_s)


def _cmp_to_sel_matrix(ncp, ns):
    cmp_start = np.arange(ncp)[:, None] * CMP_STRIDE
    sel_start = np.arange(ns)[None, :] * SLC_BLOCK
    overlap = np.minimum(cmp_start + CMP_BLOCK, sel_start + SLC_BLOCK) - np.maximum(cmp_start, sel_start)
    return jnp.asarray(np.clip(overlap, 0, None).astype(np.float32).T / CMP_BLOCK, dtype=BF16)


def _nsa(zn, zr, cmp_pos_k, cmp_w1_k, cmp_w2_k, cmp_pos_v, cmp_w1_v, cmp_w2_v, tq, tk, sel_window=LANES):
    T = zn.shape[1]
    zn16 = zn.reshape(NSA_TILES, T // CMP_STRIDE, CMP_STRIDE * NSA_HEAD_DIM)
    width = CMP_BLOCK * NSA_HEAD_DIM
    pos = jnp.stack([cmp_pos_k.reshape(1, width), cmp_pos_v.reshape(1, width)])
    pos = jnp.broadcast_to(pos, (2, 8, width)).astype(BF16)
    w1 = jnp.stack([cmp_w1_k, cmp_w1_v]).astype(BF16)
    w2 = jnp.stack([cmp_w2_k, cmp_w2_v]).astype(BF16)
    kv_cmp = _compress(zn16, pos, w1, w2)
    ns = T // SLC_BLOCK
    o_c, sel = _cmp_select(zn, kv_cmp, _cmp_to_sel_matrix(T // CMP_STRIDE, ns), tq)
    ks0 = NSA_HEADS + 2 * NSA_KV_GROUPS
    vs0 = ks0 + NSA_KV_GROUPS
    wb = min(sel_window, ns)
    blk_onehot = (jnp.arange(T)[:, None] // SLC_BLOCK % wb == jnp.arange(wb)[None, :]).astype(BF16)
    k_aug = jnp.concatenate([zn[ks0:vs0], jnp.broadcast_to(blk_onehot, (NSA_KV_GROUPS, T, wb))], axis=2)
    vt_aug = jnp.concatenate([jnp.swapaxes(zn[vs0:vs0 + NSA_KV_GROUPS], 1, 2),
                              jnp.ones((NSA_KV_GROUPS, SEL_ONES_ROWS, T), BF16)], axis=1)
    o_s = _sel_attn(zn, k_aug, vt_aug, sel, tq, tk)
    return _win_combine(zn, zr, o_c, o_s, tq)


def _softplus(y):
    return jnp.maximum(y, 0.0) + jnp.log(1.0 + jnp.exp(-jnp.abs(y)))


def _rwkv_kernel(z_ref, prev_ref, mu_ref, pv_ref, wup_hi, wup_lo, aup_hi, aup_lo, gup_hi, gup_lo, o_ref,
                 rt_s, at_s, kh_s, bh_s, kb_s, bb_s, v_s, gc_s, g_s, bonus_s, y_s, state_s, *, tb):
    C = RWKV_CHUNK
    step = pl.program_id(0)

    @pl.when(step == 0)
    def _():
        state_s[...] = jnp.zeros(state_s.shape, F32)

    has_prev = jnp.where(step > 0, 1.0, 0.0)
    row = lax.broadcasted_iota(jnp.int32, (tb, LANES), 0)

    def shifted_mix(t):
        z = z_ref[t]
        zp = jnp.where(row == 0, prev_ref[t, 7:8, :] * has_prev, pltpu.roll(z, 1, axis=0))
        return z + (zp - z) * mu_ref[t]

    ri = lax.broadcasted_iota(jnp.int32, (tb, tb), 0)
    ci = lax.broadcasted_iota(jnp.int32, (tb, tb), 1)
    same_chunk = (ri >> 6) == (ci >> 6)
    tri = jnp.where(same_chunk & (ci <= ri), 1.0, 0.0).astype(BF16)
    tot = jnp.where(same_chunk, 1.0, 0.0).astype(BF16)
    hr = lax.broadcasted_iota(jnp.int32, (LANES, LANES), 0)
    hc = lax.broadcasted_iota(jnp.int32, (LANES, LANES), 1)
    same_head = (hr >> 6) == (hc >> 6)
    head_sum = jnp.where(same_head, 1.0, 0.0).astype(BF16)
    head_mean = jnp.where(same_head, 1.0 / RWKV_HEAD_DIM, 0.0).astype(BF16)

    wa = shifted_mix(3 * RWKV_PAIRS)
    dw = _dot_xw3(jnp.tanh(wa), wup_hi[...], wup_lo[...])
    da = _dot_xw3(wa, aup_hi[...], aup_lo[...])
    g0 = jax.nn.sigmoid(shifted_mix(3 * RWKV_PAIRS + 1))
    g1 = jax.nn.sigmoid(shifted_mix(3 * RWKV_PAIRS + 2))
    g_s[...] = (_dot_xw3(g0, gup_hi[:LANES], gup_lo[:LANES]) + _dot_xw3(g1, gup_hi[LANES:], gup_lo[LANES:]))

    for p in range(RWKV_PAIRS):
        sl = slice(p * LANES, (p + 1) * LANES)
        r = shifted_mix(p)
        k = shifted_mix(RWKV_PAIRS + p)
        v = shifted_mix(2 * RWKV_PAIRS + p)
        w_log = -_softplus(-(pv_ref[0:1, sl] + dw[:, sl])) - 0.5
        lw = -jnp.exp(w_log)
        icl = jax.nn.sigmoid(pv_ref[1:2, sl] + da[:, sl])
        kk = k * pv_ref[2:3, sl]
        kk = kk * lax.rsqrt(jnp.maximum(_dot_x2(kk * kk, head_sum), 1e-24))
        k2 = k * (1.0 + (icl - 1.0) * pv_ref[3:4, sl])
        bonus_s[:, sl] = _dot_x2(r * k2 * pv_ref[4:5, sl], head_sum) * v
        hi, mid, lo = _split3(lw)
        cum = _dot(tri, hi) + _dot(tri, mid) + _dot(tri, lo)
        end = _dot(tot, hi) + _dot(tot, mid) + _dot(tot, lo)
        b = kk * icl
        inv_decay = jnp.exp(-cum)
        to_end = jnp.exp(end - cum)
        rt_s[:, sl] = (r * jnp.exp(cum)).astype(BF16)
        at_s[:, sl] = (-kk * jnp.exp(cum - lw)).astype(BF16)
        kh_s[:, sl] = (k2 * inv_decay).astype(BF16)
        bh_s[:, sl] = (b * inv_decay).astype(BF16)
        kb_s[:, sl] = (k2 * to_end).astype(BF16)
        bb_s[:, sl] = (b * to_end).astype(BF16)
        v_s[:, sl] = v.astype(BF16)
        gc_s[:, sl] = jnp.exp(end)

    t_r = hr & (C - 1)
    t_c = hc & (C - 1)
    strict = same_head & (t_c < t_r)
    incl = same_head & (t_c <= t_r)
    eye = jnp.where(hr == hc, 1.0, 0.0)
    lane_head = lax.broadcasted_iota(jnp.int32, (C, LANES), 1) >> 6

    def stack(zc):
        return jnp.concatenate([jnp.where(lane_head == 0, zc, jnp.zeros_like(zc)),
                                jnp.where(lane_head == 1, zc, jnp.zeros_like(zc))], axis=0)

    def chunk(c, carry):
        t0 = pl.multiple_of(c * C, C)
        rows = pl.ds(t0, C)
        for p in range(RWKV_PAIRS):
            sl = slice(p * LANES, (p + 1) * LANES)
            v_c = v_s[rows, sl]
            lhs = jnp.concatenate([stack(at_s[rows, sl]), stack(rt_s[rows, sl])], axis=0)
            rhs = jnp.concatenate([stack(kh_s[rows, sl]), stack(bh_s[rows, sl])], axis=0)
            aa = _dot_nt(lhs, rhs)
            a_ak = jnp.where(strict, aa[:2 * C, :2 * C], 0.0).astype(BF16)
            n_pow = jnp.where(strict, aa[:2 * C, 2 * C:], 0.0)
            a_rk = jnp.where(incl, aa[2 * C:, :2 * C], 0.0).astype(BF16)
            a_rb = jnp.where(incl, aa[2 * C:, 2 * C:], 0.0).astype(BF16)
            t_inv = eye + n_pow
            for _ in range(5):
                nb = n_pow.astype(BF16)
                n_pow = _dot(nb, nb)
                t_inv = t_inv + _dot(t_inv.astype(BF16), n_pow.astype(BF16))
            state = state_s[p]
            xs = _dot_nt(lhs, state.astype(BF16))
            v_st = stack(v_c)
            sa = _dot(t_inv.astype(BF16), (xs[:2 * C] + _dot(a_ak, v_st)).astype(BF16))
            sa_b = sa.astype(BF16)
            ys = xs[2 * C:] + _dot(jnp.concatenate([a_rk, a_rb], axis=1), jnp.concatenate([v_st, sa_b], axis=0))
            y_s[rows, sl] = ys[:C] + ys[C:]
            sa_t = (sa[:C] + sa[C:]).astype(BF16)
            upd = _dot_tn(jnp.concatenate([v_c, sa_t], axis=0),
                          jnp.concatenate([kb_s[rows, sl], bb_s[rows, sl]], axis=0))
            state_s[p] = state * gc_s[pl.ds(t0, 1), sl] + jnp.where(same_head, upd, 0.0)
        return carry

    lax.fori_loop(0, tb // C, chunk, 0)

    for p in range(RWKV_PAIRS):
        sl = slice(p * LANES, (p + 1) * LANES)
        y = y_s[:, sl]
        d = y - _dot_x2(y, head_mean)
        var = _dot_x2(d * d, head_mean)
        yn = d * lax.rsqrt(var + LNX_EPS) * pv_ref[5:6, sl] + pv_ref[6:7, sl]
        o_ref[:, sl] = (yn + bonus_s[:, sl]) * g_s[:, sl]


def _hi_lo(w):
    hi = w.astype(BF16)
    return hi, (w - hi.astype(F32)).astype(BF16)


def _rwkv(zr, mu_t, w0, w_up, a0, a_up, g_up, k_k, k_a, r_k, lnx_g, lnx_b, tb):
    T = zr.shape[1]
    n_in = RW_TILES - 1
    pv = jnp.stack([w0, a0, k_k, k_a, r_k.reshape(-1), lnx_g, lnx_b, jnp.zeros_like(w0)])
    wup = _hi_lo(jnp.pad(w_up, ((0, LANES - W_RANK), (0, 0))))
    aup = _hi_lo(jnp.pad(a_up, ((W_RANK, LANES - W_RANK - A_RANK), (0, 0))))
    gup = _hi_lo(jnp.pad(g_up, ((0, 2 * LANES - G_RANK), (0, 0))))
    full = lambda shape: pl.BlockSpec(shape, lambda s: (0,) * len(shape))
    kern = functools.partial(_rwkv_kernel, tb=tb)
    bf = lambda: pltpu.VMEM((tb, RWKV_WIDTH), BF16)
    ff = lambda: pltpu.VMEM((tb, RWKV_WIDTH), F32)
    return pl.pallas_call(
        kern,
        grid=(T // tb,),
        in_specs=[
            pl.BlockSpec((n_in, tb, LANES), lambda s: (0, s, 0)),
            pl.BlockSpec((n_in, 8, LANES), lambda s: (0, jnp.maximum(s * (tb // 8) - 1, 0), 0)),
            full((n_in, 1, LANES)),
            full((8, RWKV_WIDTH)),
            full((LANES, RWKV_WIDTH)), full((LANES, RWKV_WIDTH)),
            full((LANES, RWKV_WIDTH)), full((LANES, RWKV_WIDTH)),
            full((2 * LANES, RWKV_WIDTH)), full((2 * LANES, RWKV_WIDTH)),
        ],
        out_specs=pl.BlockSpec((tb, RWKV_WIDTH), lambda s: (s, 0)),
        out_shape=jax.ShapeDtypeStruct((T, RWKV_WIDTH), F32),
        scratch_shapes=[bf(), bf(), bf(), bf(), bf(), bf(), bf(), ff(), ff(), ff(), ff(),
                        pltpu.VMEM((RWKV_PAIRS, LANES, LANES), F32)],
        compiler_params=_cparams(("arbitrary",)),
        name="rwkv7",
    )(zr, zr, mu_t, pv, *wup, *aup, *gup)


def _mix_out_kernel(ya_ref, yb_ref, g_ref, w_ref, x_ref, o_ref, h_scr):
    @pl.when(pl.program_id(1) == 0)
    def _():
        h_scr[:, :NSA_WIDTH] = _rms(ya_ref[...], g_ref[...]).astype(BF16)
        h_scr[:, NSA_WIDTH:] = yb_ref[...].astype(BF16)

    o_ref[...] = x_ref[...] + _dot(h_scr[...], w_ref[...])


def _mix_out(y_a, y_b, g, w, x, tm, tn):
    T = x.shape[0]
    return pl.pallas_call(
        _mix_out_kernel,
        grid=(T // tm, D_MODEL // tn),
        in_specs=[
            pl.BlockSpec((tm, NSA_WIDTH), lambda m, n: (m, 0)),
            pl.BlockSpec((tm, RWKV_WIDTH), lambda m, n: (m, 0)),
            pl.BlockSpec((1, NSA_WIDTH), lambda m, n: (0, 0)),
            pl.BlockSpec((NSA_WIDTH + RWKV_WIDTH, tn), lambda m, n: (0, n)),
            pl.BlockSpec((tm, tn), lambda m, n: (m, n)),
        ],
        out_specs=pl.BlockSpec((tm, tn), lambda m, n: (m, n)),
        out_shape=jax.ShapeDtypeStruct((T, D_MODEL), F32),
        scratch_shapes=[pltpu.VMEM((tm, NSA_WIDTH + RWKV_WIDTH), BF16)],
        compiler_params=_cparams(("parallel", "arbitrary")),
        name="mix_out_proj",
    )(y_a, y_b, g, w, x)


def _norm_mm_kernel(x_ref, g_ref, w_ref, o_ref, h_scr):
    @pl.when(pl.program_id(1) == 0)
    def _():
        h_scr[...] = _rms(x_ref[...], g_ref[...]).astype(BF16)

    o_ref[...] = _dot(h_scr[...], w_ref[...]).astype(o_ref.dtype)


def _norm_mm(x, g, w, tm, tn, name):
    M, K = x.shape
    N = w.shape[1]
    return pl.pallas_call(
        _norm_mm_kernel,
        grid=(M // tm, N // tn),
        in_specs=[
            pl.BlockSpec((tm, K), lambda m, n: (m, 0)),
            pl.BlockSpec((1, K), lambda m, n: (0, 0)),
            pl.BlockSpec((K, tn), lambda m, n: (0, n)),
        ],
        out_specs=pl.BlockSpec((tm, tn), lambda m, n: (m, n)),
        out_shape=jax.ShapeDtypeStruct((M, N), BF16),
        scratch_shapes=[pltpu.VMEM((tm, K), BF16)],
        compiler_params=_cparams(("parallel", "arbitrary")),
        name=name,
    )(x, g, w)


def _mm_res_kernel(a_ref, w_ref, x_ref, o_ref):
    o_ref[...] = x_ref[...] + _dot(a_ref[...], w_ref[...])


def _mm_res(a, w, x, tm, tn, name):
    M, K = a.shape
    N = w.shape[1]
    return pl.pallas_call(
        _mm_res_kernel,
        grid=(M // tm, N // tn),
        in_specs=[
            pl.BlockSpec((tm, K), lambda m, n: (m, 0)),
            pl.BlockSpec((K, tn), lambda m, n: (0, n)),
            pl.BlockSpec((tm, tn), lambda m, n: (m, n)),
        ],
        out_specs=pl.BlockSpec((tm, tn), lambda m, n: (m, n)),
        out_shape=jax.ShapeDtypeStruct((M, N), F32),
        compiler_params=_cparams(("parallel", "arbitrary")),
        name=name,
    )(a, w, x)


def _mem_attn_kernel(q_ref, k_ref, v_ref, o_ref):
    scale = MEM_HEAD_DIM ** -0.5
    for h in range(MEM_HEADS):
        sl = slice(h * MEM_HEAD_DIM, (h + 1) * MEM_HEAD_DIM)
        s = _dot_nt(q_ref[:, sl], k_ref[:, sl]) * scale
        e = jnp.exp(s - jnp.max(s, axis=-1, keepdims=True))
        p = e / jnp.sum(e, axis=-1, keepdims=True)
        o_ref[:, sl] = _dot(p.astype(BF16), v_ref[:, sl]).astype(o_ref.dtype)


def _mem_attn(q, k, v, tm):
    T = q.shape[0]
    M = k.shape[0]
    return pl.pallas_call(
        _mem_attn_kernel,
        grid=(T // tm,),
        in_specs=[
            pl.BlockSpec((tm, D_MODEL), lambda m: (m, 0)),
            pl.BlockSpec((M, D_MODEL), lambda m: (0, 0)),
            pl.BlockSpec((M, D_MODEL), lambda m: (0, 0)),
        ],
        out_specs=pl.BlockSpec((tm, D_MODEL), lambda m: (m, 0)),
        out_shape=jax.ShapeDtypeStruct((T, D_MODEL), BF16),
        compiler_params=_cparams(("parallel",)),
        name="mem_attn",
    )(q, k, v)


def _router_kernel(x_ref, g_ref, w_hi, w_lo, b_ref, h_ref, c_ref):
    h = _rms(x_ref[...], g_ref[...])
    h_ref[...] = h.astype(BF16)
    logits = _dot_xw3(h, w_hi[...], w_lo[...]) + b_ref[...]
    lane = lax.broadcasted_iota(jnp.int32, logits.shape, 1)
    big = jnp.int32(LANES)
    is_grp = (lane >= N_EXPERTS) & (lane < N_EXPERTS + N_GROUPS)
    lg = jnp.where(is_grp, logits, NEG)
    eg = jnp.where(is_grp, jnp.exp(lg - jnp.max(lg, axis=-1, keepdims=True)), 0.0)
    pg = eg / jnp.sum(eg, axis=-1, keepdims=True)
    pg_top = jnp.max(pg, axis=-1, keepdims=True)
    g_idx = jnp.min(jnp.where(is_grp & (pg == pg_top), lane - N_EXPERTS, big), axis=-1, keepdims=True)
    in_grp = (lane < N_EXPERTS) & ((lane >> 3) == g_idx)
    le = jnp.where(in_grp, logits, NEG)
    ee = jnp.where(in_grp, jnp.exp(le - jnp.max(le, axis=-1, keepdims=True)), 0.0)
    pe = jnp.where(in_grp, ee / jnp.sum(ee, axis=-1, keepdims=True), -1.0)
    p1 = jnp.max(pe, axis=-1, keepdims=True)
    hit1 = lane == jnp.min(jnp.where(pe == p1, lane, big), axis=-1, keepdims=True)
    pe2 = jnp.where(hit1, -1.0, pe)
    p2 = jnp.max(pe2, axis=-1, keepdims=True)
    hit2 = lane == jnp.min(jnp.where(pe2 == p2, lane, big), axis=-1, keepdims=True)
    denom = p1 + p2
    c_ref[...] = jnp.where(hit1, pg_top * p1 / denom, 0.0) + jnp.where(hit2, pg_top * p2 / denom, 0.0)


def _router(x, g, w_hi, w_lo, b, tm):
    T = x.shape[0]
    return pl.pallas_call(
        _router_kernel,
        grid=(T // tm,),
        in_specs=[
            pl.BlockSpec((tm, D_MODEL), lambda m: (m, 0)),
            pl.BlockSpec((1, D_MODEL), lambda m: (0, 0)),
            pl.BlockSpec((D_MODEL, LANES), lambda m: (0, 0)),
            pl.BlockSpec((D_MODEL, LANES), lambda m: (0, 0)),
            pl.BlockSpec((1, LANES), lambda m: (0, 0)),
        ],
        out_specs=[
            pl.BlockSpec((tm, D_MODEL), lambda m: (m, 0)),
            pl.BlockSpec((tm, LANES), lambda m: (m, 0)),
        ],
        out_shape=[
            jax.ShapeDtypeStruct((T, D_MODEL), BF16),
            jax.ShapeDtypeStruct((T, LANES), F32),
        ],
        compiler_params=_cparams(("parallel",)),
        name="moe_router",
    )(x, g, w_hi, w_lo, b)


def _moe_kernel(h_ref, c_ref, wg_ref, wu_ref, wd_ref, x_ref, gf_ref, o_ref, acc_scr):
    e = pl.program_id(1)

    @pl.when(e == 0)
    def _():
        acc_scr[...] = jnp.zeros(acc_scr.shape, F32)

    h = h_ref[...]
    comb = c_ref[...]
    lane = lax.broadcasted_iota(jnp.int32, comb.shape, 1)
    c_e = jnp.sum(jnp.where(lane == e, comb, 0.0), axis=-1, keepdims=True)
    hid = jax.nn.silu(_dot(h, wg_ref[0])) * _dot(h, wu_ref[0])
    acc_scr[...] += _dot((hid * c_e).astype(BF16), wd_ref[0])

    @pl.when(e == pl.num_programs(1) - 1)
    def _():
        o_ref[...] = _rms(x_ref[...] + acc_scr[...], gf_ref[...])


def _moe(h, comb, wg, wu, wd, x, gf, tm):
    T = x.shape[0]
    return pl.pallas_call(
        _moe_kernel,
        grid=(T // tm, N_EXPERTS),
        in_specs=[
            pl.BlockSpec((tm, D_MODEL), lambda m, e: (m, 0)),
            pl.BlockSpec((tm, LANES), lambda m, e: (m, 0)),
            pl.BlockSpec((1, D_MODEL, EXPERT_FF), lambda m, e: (e, 0, 0)),
            pl.BlockSpec((1, D_MODEL, EXPERT_FF), lambda m, e: (e, 0, 0)),
            pl.BlockSpec((1, EXPERT_FF, D_MODEL), lambda m, e: (e, 0, 0)),
            pl.BlockSpec((tm, D_MODEL), lambda m, e: (m, 0)),
            pl.BlockSpec((1, D_MODEL), lambda m, e: (0, 0)),
        ],
        out_specs=pl.BlockSpec((tm, D_MODEL), lambda m, e: (m, 0)),
        out_shape=jax.ShapeDtypeStruct((T, D_MODEL), F32),
        scratch_shapes=[pltpu.VMEM((tm, D_MODEL), F32)],
        compiler_params=_cparams(("parallel", "arbitrary")),
        name="moe_experts",
    )(h, comb, wg, wu, wd, x, gf)


def _pack_w_in(w_in, rwkv_mu):
    nsa_cols = NSA_WIDTH + 6 * NSA_KV_WIDTH
    gl = w_in[:, nsa_cols:nsa_cols + 3 * NSA_HEADS]
    rw0 = nsa_cols + 3 * NSA_HEADS
    rkv = w_in[:, rw0:rw0 + 3 * RWKV_WIDTH + W_RANK + A_RANK]
    gd = w_in[:, rw0 + 3 * RWKV_WIDTH + W_RANK + A_RANK:]

    def pad(a, n):
        return jnp.pad(a, ((0, 0), (0, n - a.shape[1])))

    w = jnp.concatenate([w_in[:, :nsa_cols], rkv, pad(gd, 2 * LANES), pad(gl, LANES)], axis=1)
    mu = rwkv_mu.reshape(1, -1)
    mu_rkv = mu[:, :3 * RWKV_WIDTH + W_RANK + A_RANK]
    mu_gd = pad(mu[:, 3 * RWKV_WIDTH + W_RANK + A_RANK:], 2 * LANES)
    mu_t = jnp.concatenate([mu_rkv, mu_gd], axis=1).reshape(RW_TILES - 1, 1, LANES)
    return w.astype(BF16), mu_t


def kernel(x, mem, ln1_g, w_in, cmp_pos_k, cmp_w1_k, cmp_w2_k, cmp_pos_v, cmp_w1_v, cmp_w2_v, nsa_norm_g, rwkv_mu, rwkv_w0, rwkv_w_up, rwkv_a0, rwkv_a_up, rwkv_g_up, rwkv_k_k, rwkv_k_a, rwkv_r_k, rwkv_lnx_g, rwkv_lnx_b, w_out, ln_mem_g, ln2_g, wq_mem, wk_mem, wv_mem, wo_mem, ln3_g, router_group_w, router_group_b, router_expert_w, router_expert_b, moe_w_gate, moe_w_up, moe_w_down, lnf_g):
    B, T, _ = x.shape
    assert B == 1 and T % 512 == 0 and w_in.shape[0] == 1
    tm = 512
    row = lambda a: a.reshape(1, -1)
    xs = x[0]
    for l in range(w_in.shape[0]):
        wp, mu_t = _pack_w_in(w_in[l], rwkv_mu[l])
        zn, zr = _in_proj(xs, row(ln1_g[l]), wp, tm)
        y_a = _nsa(zn, zr, cmp_pos_k[l], cmp_w1_k[l], cmp_w2_k[l], cmp_pos_v[l], cmp_w1_v[l], cmp_w2_v[l],
                   128, 256)
        y_b = _rwkv(zr, mu_t, rwkv_w0[l], rwkv_w_up[l], rwkv_a0[l], rwkv_a_up[l], rwkv_g_up[l], rwkv_k_k[l],
                    rwkv_k_a[l], rwkv_r_k[l], rwkv_lnx_g[l], rwkv_lnx_b[l], 256)
        xs = _mix_out(y_a, y_b, row(nsa_norm_g[l]), w_out[l].astype(BF16), xs, tm, 512)
        m_tok = mem[0]
        k_mem = _norm_mm(m_tok, row(ln_mem_g[l]), wk_mem[l].astype(BF16), m_tok.shape[0], 512, "mem_k")
        v_mem = _norm_mm(m_tok, row(ln_mem_g[l]), wv_mem[l].astype(BF16), m_tok.shape[0], 512, "mem_v")
        q_mem = _norm_mm(xs, row(ln2_g[l]), wq_mem[l].astype(BF16), tm, 512, "mem_q")
        o_mem = _mem_attn(q_mem, k_mem, v_mem, tm)
        xs = _mm_res(o_mem, wo_mem[l].astype(BF16), xs, tm, 512, "mem_out_proj")
        w_r = jnp.pad(jnp.concatenate([router_expert_w[l], router_group_w[l]], axis=1),
                      ((0, 0), (0, LANES - N_EXPERTS - N_GROUPS)))
        b_r = jnp.pad(jnp.concatenate([router_expert_b[l], router_group_b[l]]), (0, LANES - N_EXPERTS - N_GROUPS))
        h3, comb = _router(xs, row(ln3_g[l]), *_hi_lo(w_r), row(b_r), tm)
        xs = _moe(h3, comb, moe_w_gate[l].astype(BF16), moe_w_up[l].astype(BF16), moe_w_down[l].astype(BF16),
                  xs, row(lnf_g), tm)
    return xs[None]
```

```python
import functools

import numpy as np
import jax
import jax.numpy as jnp
from jax import lax
from jax.experimental import pallas as pl
from jax.experimental.pallas import tpu as pltpu

F32 = jnp.float32
BF16 = jnp.bfloat16

LANES = 128
D_MODEL = 2048
EPS = 1e-6
NSA_HEAD_DIM = 128
NSA_HEADS = 8
NSA_KV_GROUPS = 2
NSA_HPG = NSA_HEADS // NSA_KV_GROUPS
NSA_WIDTH = NSA_HEADS * NSA_HEAD_DIM
NSA_KV_WIDTH = NSA_KV_GROUPS * NSA_HEAD_DIM
CMP_BLOCK = 32
CMP_STRIDE = 16
SLC_BLOCK = 64
SLC_TOPK = 16
WINDOW = 512
NEG = -1e30
RWKV_HEAD_DIM = 64
RWKV_HEADS = 16
RWKV_WIDTH = RWKV_HEADS * RWKV_HEAD_DIM
RWKV_PAIRS = RWKV_WIDTH // LANES
W_RANK = 64
A_RANK = 64
G_RANK = 160
LNX_EPS = 64e-5
RWKV_CHUNK = 64
MEM_HEADS = 4
MEM_HEAD_DIM = D_MODEL // MEM_HEADS
N_GROUPS = 4
EXPERTS_PER_GROUP = 8
N_EXPERTS = N_GROUPS * EXPERTS_PER_GROUP
EXPERT_FF = 256

NSA_TILES = 20
RW_TILES = 28
IN_TILE_BLOCK = 4

VMEM_LIMIT = 56 * 1024 * 1024

TM_PROJ = 256
TQ_NSA = 128
TK_SEL = 512
TB_RWKV = 256
TM_MOE = 512


def _cparams(sem):
    return pltpu.CompilerParams(dimension_semantics=sem, vmem_limit_bytes=VMEM_LIMIT)


def _dot(a, b):
    return jnp.dot(a, b, preferred_element_type=F32)


def _dot_nt(a, b):
    return lax.dot_general(a, b, (((1,), (1,)), ((), ())), preferred_element_type=F32)


def _dot_tn(a, b):
    return lax.dot_general(a, b, (((0,), (0,)), ((), ())), preferred_element_type=F32)


def _split2(x):
    hi = x.astype(BF16)
    lo = (x - hi.astype(F32)).astype(BF16)
    return hi, lo


def _split3(x):
    hi = x.astype(BF16)
    r1 = x - hi.astype(F32)
    mid = r1.astype(BF16)
    lo = (r1 - mid.astype(F32)).astype(BF16)
    return hi, mid, lo


def _dot_x2(x, w):
    hi, lo = _split2(x)
    return _dot(hi, w) + _dot(lo, w)


def _dot_x3(x, w):
    hi, mid, lo = _split3(x)
    return _dot(hi, w) + _dot(mid, w) + _dot(lo, w)


def _dot_xw3(x, w_hi, w_lo):
    hi, lo = _split2(x)
    return _dot(hi, w_hi) + _dot(lo, w_hi) + _dot(hi, w_lo)


def _rms(x, g):
    ms = jnp.mean(x * x, axis=-1, keepdims=True)
    return x * lax.rsqrt(ms + EPS) * g


def _resident(shape):
    return pl.BlockSpec(shape, lambda *_: (0,) * len(shape), pipeline_mode=pl.Buffered(1))


def _in_proj_kernel(x_ref, g_ref, w_ref, zn_ref, zr_ref, *, q_scale):
    h = _rms(x_ref[...], g_ref[...]).astype(BF16)
    nb = IN_TILE_BLOCK * LANES
    for n in range((NSA_TILES + RW_TILES) // IN_TILE_BLOCK):
        z = _dot(h, w_ref[:, n * nb:(n + 1) * nb])
        for t in range(IN_TILE_BLOCK):
            tile = n * IN_TILE_BLOCK + t
            zt = z[:, t * LANES:(t + 1) * LANES]
            if tile < NSA_HEADS:
                zn_ref[tile] = (zt * q_scale).astype(BF16)
            elif tile < NSA_TILES:
                zn_ref[tile] = zt.astype(BF16)
            else:
                zr_ref[tile - NSA_TILES] = zt


def _in_proj(x, g, w, tm):
    T = x.shape[0]
    kern = functools.partial(_in_proj_kernel, q_scale=NSA_HEAD_DIM ** -0.5)
    return pl.pallas_call(
        kern,
        grid=(T // tm,),
        in_specs=[
            pl.BlockSpec((tm, D_MODEL), lambda m: (m, 0)),
            _resident((1, D_MODEL)),
            _resident((D_MODEL, (NSA_TILES + RW_TILES) * LANES)),
        ],
        out_specs=[
            pl.BlockSpec((NSA_TILES, tm, LANES), lambda m: (0, m, 0)),
            pl.BlockSpec((RW_TILES, tm, LANES), lambda m: (0, m, 0)),
        ],
        out_shape=[
            jax.ShapeDtypeStruct((NSA_TILES, T, LANES), BF16),
            jax.ShapeDtypeStruct((RW_TILES, T, LANES), F32),
        ],
        compiler_params=_cparams(("parallel",)),
        name="in_proj",
    )(x, g, w)


def _compress_kernel(x_ref, pos_ref, w1_ref, w2_ref, o_ref):
    half = CMP_STRIDE * NSA_HEAD_DIM
    x = x_ref[0]
    w1 = w1_ref[0]
    h_first = _dot(x, w1[:half])
    h_second = _dot(x, w1[half:])
    n = x.shape[0]
    h_next = pltpu.roll(h_second, n - 1, axis=0)
    bias = _dot(pos_ref[0], w1)[0:1]
    hid = jax.nn.gelu(h_first + h_next + bias)
    o_ref[0] = _dot(hid.astype(BF16), w2_ref[0]).astype(o_ref.dtype)


def _compress(zn16, pos, w1, w2):
    n = zn16.shape[1]
    width = CMP_BLOCK * NSA_HEAD_DIM
    return pl.pallas_call(
        _compress_kernel,
        grid=(4,),
        in_specs=[
            pl.BlockSpec((1, n, CMP_STRIDE * NSA_HEAD_DIM), lambda s: (NSA_HEADS + s, 0, 0)),
            pl.BlockSpec((1, 8, width), lambda s: (s // 2, 0, 0)),
            pl.BlockSpec((1, width, NSA_HEAD_DIM), lambda s: (s // 2, 0, 0)),
            pl.BlockSpec((1, NSA_HEAD_DIM, NSA_HEAD_DIM), lambda s: (s // 2, 0, 0)),
        ],
        out_specs=pl.BlockSpec((1, n, NSA_HEAD_DIM), lambda s: (s, 0, 0)),
        out_shape=jax.ShapeDtypeStruct((4, n, NSA_HEAD_DIM), BF16),
        compiler_params=_cparams(("parallel",)),
        name="nsa_compress",
    )(zn16, pos, w1, w2)


def _cmp_select_kernel(q_ref, kc_ref, vc_ref, m_ref, oc_ref, sel_ref, *, tq, top_n):
    q0 = pl.program_id(1) * tq
    rows = NSA_HPG * tq
    ncp = kc_ref.shape[1]
    ns = m_ref.shape[0]
    q = q_ref[...].reshape(rows, NSA_HEAD_DIM)
    s = _dot_nt(q, kc_ref[0])
    qpos = q0 + (lax.broadcasted_iota(jnp.int32, (rows, 1), 0) & (tq - 1))
    cmp_end = lax.broadcasted_iota(jnp.int32, (1, ncp), 1) * CMP_STRIDE + (CMP_BLOCK - 1)
    mask = cmp_end <= qpos
    s = jnp.where(mask, s, NEG)
    mx = jnp.max(s, axis=-1, keepdims=True)
    e = jnp.where(mask, jnp.exp(s - mx), 0.0)
    p = e / jnp.maximum(jnp.sum(e, axis=-1, keepdims=True), 1e-30)
    o = _dot(p.astype(BF16), vc_ref[0])
    for h in range(NSA_HPG):
        oc_ref[:, h * NSA_HEAD_DIM:(h + 1) * NSA_HEAD_DIM] = o[h * tq:(h + 1) * tq]
    psum = p[0:tq]
    for h in range(1, NSA_HPG):
        psum = psum + p[h * tq:(h + 1) * tq]
    ps_hi, ps_lo = _split2(psum)
    p_sel = _dot_nt(m_ref[...], ps_hi) + _dot_nt(m_ref[...], ps_lo)
    blk = (q0 + lax.broadcasted_iota(jnp.int32, (1, tq), 1)) >> 6
    j = lax.broadcasted_iota(jnp.int32, (ns, tq), 0)
    future = j > blk
    forced = (j == 0) | (j == blk) | (j == blk - 1)
    score = jnp.where(future, -1.0, jnp.where(forced, 1e6, p_sel))
    sel = jnp.zeros((ns, tq), F32)
    for _ in range(top_n):
        best = jnp.max(score, axis=0, keepdims=True)
        idx = jnp.min(jnp.where(score == best, j, ns), axis=0, keepdims=True)
        hit = j == idx
        sel = jnp.where(hit & (best >= 0.0), 1.0, sel)
        score = jnp.where(hit, -2.0, score)
    sel_ref[0] = sel.astype(sel_ref.dtype)


def _cmp_select(zn, kv_cmp, cmp_to_sel, tq):
    T = zn.shape[1]
    ncp = kv_cmp.shape[1]
    ns = cmp_to_sel.shape[0]
    kern = functools.partial(_cmp_select_kernel, tq=tq, top_n=min(SLC_TOPK, ns))
    return pl.pallas_call(
        kern,
        grid=(NSA_KV_GROUPS, T // tq),
        in_specs=[
            pl.BlockSpec((NSA_HPG, tq, NSA_HEAD_DIM), lambda g, i: (g, i, 0)),
            pl.BlockSpec((1, ncp, NSA_HEAD_DIM), lambda g, i: (g, 0, 0)),
            pl.BlockSpec((1, ncp, NSA_HEAD_DIM), lambda g, i: (NSA_KV_GROUPS + g, 0, 0)),
            pl.BlockSpec((ns, ncp), lambda g, i: (0, 0)),
        ],
        out_specs=[
            pl.BlockSpec((tq, NSA_HPG * NSA_HEAD_DIM), lambda g, i: (i, g)),
            pl.BlockSpec((1, ns, tq), lambda g, i: (g, 0, i)),
        ],
        out_shape=[
            jax.ShapeDtypeStruct((T, NSA_WIDTH), F32),
            jax.ShapeDtypeStruct((NSA_KV_GROUPS, ns, T), BF16),
        ],
        compiler_params=_cparams(("parallel", "parallel")),
        name="nsa_cmp_select",
    )(zn, kv_cmp, kv_cmp, cmp_to_sel)


SEL_ONES_ROWS = 16


def _sel_attn_kernel(q_ref, k_ref, vt_ref, sel_ref, o_ref, bias_scr, s_a, s_b, p_a, p_b, acc_scr, m_scr,
                     *, tq, tk, wb):
    q0 = pl.program_id(1) * tq
    cols = NSA_HPG * tq
    tpw = wb * SLC_BLOCK // tk
    not_chosen = (sel_ref[0].astype(F32) - 1.0) * (-NEG)
    q_t = []
    for h in range(NSA_HPG):
        bias_scr[:, h * tq:(h + 1) * tq] = not_chosen
        q_t.append(q_ref[h].astype(F32).T.astype(BF16))
    q_t = jnp.concatenate(q_t, axis=1)

    def q_window(w):
        rows = pl.ds(pl.multiple_of(w * wb, wb), wb)
        return jnp.concatenate([q_t, bias_scr[rows, :].astype(BF16)], axis=0)

    def scores(j, q_aug):
        return _dot(k_ref[0, pl.ds(pl.multiple_of(j * tk, tk), tk), :], q_aug)

    def pv(j, p_ref):
        return _dot(vt_ref[0, :, pl.ds(pl.multiple_of(j * tk, tk), tk)], p_ref[...])

    def softmax_tile(s_ref, p_ref):
        s = s_ref[...]
        m_old = m_scr[...]
        m_new = jnp.maximum(m_old, jnp.max(s, axis=0, keepdims=True))
        m_scr[...] = m_new
        p_ref[...] = jnp.exp((s - m_new).astype(BF16))
        return jnp.exp(m_old - m_new)

    n_full = q0 // tk
    s = scores(n_full, q_window(n_full // tpw))
    kpos = n_full * tk + lax.broadcasted_iota(jnp.int32, (tk, 1), 0)
    qpos = q0 + (lax.broadcasted_iota(jnp.int32, (1, cols), 1) & (tq - 1))
    s = jnp.where(kpos <= qpos, s, NEG)
    m_first = jnp.max(s, axis=0, keepdims=True)
    m_scr[...] = m_first
    p_a[...] = jnp.exp((s - m_first).astype(BF16))
    acc_scr[...] = pv(n_full, p_a)

    def window(w, carry):
        lo = w * tpw
        cnt = jnp.minimum(n_full - lo, tpw)
        q_aug = q_window(w)
        s_a[...] = scores(lo, q_aug)
        p_b[...] = jnp.zeros(p_b.shape, BF16)

        def pair(t, alpha_prev):
            a = lo + 2 * t
            s_b[...] = scores(a + 1, q_aug)
            acc_scr[...] = alpha_prev * acc_scr[...] + pv(jnp.maximum(a - 1, 0), p_b)
            alpha_a = softmax_tile(s_a, p_a)
            s_a[...] = scores(a + 2, q_aug)
            acc_scr[...] = alpha_a * acc_scr[...] + pv(a, p_a)
            return softmax_tile(s_b, p_b)

        pairs = cnt // 2
        alpha_last = lax.fori_loop(0, pairs, pair, jnp.ones((1, cols), F32))
        acc_scr[...] = alpha_last * acc_scr[...] + pv(jnp.maximum(lo + 2 * pairs - 1, 0), p_b)

        @pl.when(cnt % 2 == 1)
        def _():
            alpha = softmax_tile(s_a, p_a)
            acc_scr[...] = alpha * acc_scr[...] + pv(lo + cnt - 1, p_a)

        return carry

    lax.fori_loop(0, (n_full + tpw - 1) // tpw, window, 0)
    acc = acc_scr[...]
    o_t = acc[:NSA_HEAD_DIM] / acc[NSA_HEAD_DIM:NSA_HEAD_DIM + 1]
    for h in range(NSA_HPG):
        o_ref[:, h * NSA_HEAD_DIM:(h + 1) * NSA_HEAD_DIM] = o_t[:, h * tq:(h + 1) * tq].T


def _sel_attn(zn, k_aug, vt_aug, sel, tq, tk):
    T = zn.shape[1]
    ns = sel.shape[1]
    wb = k_aug.shape[2] - NSA_HEAD_DIM
    assert tk % tq == 0 and (wb * SLC_BLOCK) % (2 * tk) == 0 and ns % wb == 0
    kern = functools.partial(_sel_attn_kernel, tq=tq, tk=tk, wb=wb)
    cols = NSA_HPG * tq
    return pl.pallas_call(
        kern,
        grid=(NSA_KV_GROUPS, T // tq),
        in_specs=[
            pl.BlockSpec((NSA_HPG, tq, NSA_HEAD_DIM), lambda g, i: (g, i, 0)),
            pl.BlockSpec((1, T, NSA_HEAD_DIM + wb), lambda g, i: (g, 0, 0)),
            pl.BlockSpec((1, NSA_HEAD_DIM + SEL_ONES_ROWS, T), lambda g, i: (g, 0, 0)),
            pl.BlockSpec((1, ns, tq), lambda g, i: (g, 0, i)),
        ],
        out_specs=pl.BlockSpec((tq, NSA_HPG * NSA_HEAD_DIM), lambda g, i: (i, g)),
        out_shape=jax.ShapeDtypeStruct((T, NSA_WIDTH), F32),
        scratch_shapes=[
            pltpu.VMEM((ns, cols), F32),
            pltpu.VMEM((tk, cols), F32), pltpu.VMEM((tk, cols), F32),
            pltpu.VMEM((tk, cols), BF16), pltpu.VMEM((tk, cols), BF16),
            pltpu.VMEM((NSA_HEAD_DIM + SEL_ONES_ROWS, cols), F32),
            pltpu.VMEM((1, cols), F32),
        ],
        compiler_params=_cparams(("parallel", "arbitrary")),
        name="nsa_sel_attn",
    )(zn, k_aug, vt_aug, sel)


def _win_kernel(*refs, tq, nwb):
    q_ref = refs[0]
    k_refs = refs[1:1 + nwb]
    v_refs = refs[1 + nwb:1 + 2 * nwb]
    gl_ref, oc_ref, os_ref, o_ref = refs[1 + 2 * nwb:]
    g = pl.program_id(0)
    i = pl.program_id(1)
    q0 = i * tq
    qpos = q0 + lax.broadcasted_iota(jnp.int32, (tq, 1), 0)
    kcat = jnp.concatenate([r[0] for r in k_refs], axis=0)
    vcat = jnp.concatenate([r[0] for r in v_refs], axis=0)
    kpos = q0 - (nwb - 1) * tq + lax.broadcasted_iota(jnp.int32, (1, nwb * tq), 1)
    diff = qpos - kpos
    ok = (diff >= 0) & (diff < WINDOW) & (kpos >= 0)
    gates = jax.nn.sigmoid(gl_ref[0])
    lane = lax.broadcasted_iota(jnp.int32, gates.shape, 1)
    for h in range(NSA_HPG):
        s = jnp.where(ok, _dot_nt(q_ref[h], kcat), NEG)
        mx = jnp.max(s, axis=-1, keepdims=True)
        e = jnp.where(ok, jnp.exp(s - mx), 0.0)
        p = e / jnp.maximum(jnp.sum(e, axis=-1, keepdims=True), 1e-30)
        o_w = _dot(p.astype(BF16), vcat)
        col = g * NSA_HPG + h

        def gate(branch):
            return jnp.sum(jnp.where(lane == branch * NSA_HEADS + col, gates, 0.0), axis=-1, keepdims=True)

        sl = slice(h * NSA_HEAD_DIM, (h + 1) * NSA_HEAD_DIM)
        o_ref[:, sl] = gate(0) * oc_ref[:, sl] + gate(1) * os_ref[:, sl] + gate(2) * o_w


def _win_combine(zn, zr, o_c, o_s, tq):
    T = zn.shape[1]
    nwb = WINDOW // tq + 1
    k_tile0 = NSA_HEADS + 4 * NSA_KV_GROUPS
    v_tile0 = k_tile0 + NSA_KV_GROUPS

    def kv_spec(tile0, d):
        return pl.BlockSpec((1, tq, NSA_HEAD_DIM),
                            lambda g, i: (tile0 + g, jnp.maximum(i - (nwb - 1) + d, 0), 0))

    wide = pl.BlockSpec((tq, NSA_HPG * NSA_HEAD_DIM), lambda g, i: (i, g))
    kern = functools.partial(_win_kernel, tq=tq, nwb=nwb)
    return pl.pallas_call(
        kern,
        grid=(NSA_KV_GROUPS, T // tq),
        in_specs=([pl.BlockSpec((NSA_HPG, tq, NSA_HEAD_DIM), lambda g, i: (g, i, 0))]
                  + [kv_spec(k_tile0, d) for d in range(nwb)]
                  + [kv_spec(v_tile0, d) for d in range(nwb)]
                  + [pl.BlockSpec((1, tq, LANES), lambda g, i: (RW_TILES - 1, i, 0)), wide, wide]),
        out_specs=wide,
        out_shape=jax.ShapeDtypeStruct((T, NSA_WIDTH), F32),
        compiler_params=_cparams(("parallel", "parallel")),
        name="nsa_window_combine",
    )(*([zn] * (1 + 2 * nwb)), zr, o_c, o_s)


def _cmp_to_sel_matrix(ncp, ns):
    cmp_start = np.arange(ncp)[:, None] * CMP_STRIDE
    sel_start = np.arange(ns)[None, :] * SLC_BLOCK
    overlap = np.minimum(cmp_start + CMP_BLOCK, sel_start + SLC_BLOCK) - np.maximum(cmp_start, sel_start)
    return jnp.asarray(np.clip(overlap, 0, None).astype(np.float32).T / CMP_BLOCK, dtype=BF16)


def _nsa(zn, zr, cmp_pos_k, cmp_w1_k, cmp_w2_k, cmp_pos_v, cmp_w1_v, cmp_w2_v, tq, tk, sel_window=LANES):
    T = zn.shape[1]
    zn16 = zn.reshape(NSA_TILES, T // CMP_STRIDE, CMP_STRIDE * NSA_HEAD_DIM)
    width = CMP_BLOCK * NSA_HEAD_DIM
    pos = jnp.stack([cmp_pos_k.reshape(1, width), cmp_pos_v.reshape(1, width)])
    pos = jnp.broadcast_to(pos, (2, 8, width)).astype(BF16)
    w1 = jnp.stack([cmp_w1_k, cmp_w1_v]).astype(BF16)
    w2 = jnp.stack([cmp_w2_k, cmp_w2_v]).astype(BF16)
    kv_cmp = _compress(zn16, pos, w1, w2)
    ns = T // SLC_BLOCK
    o_c, sel = _cmp_select(zn, kv_cmp, _cmp_to_sel_matrix(T // CMP_STRIDE, ns), tq)
    ks0 = NSA_HEADS + 2 * NSA_KV_GROUPS
    vs0 = ks0 + NSA_KV_GROUPS
    wb = min(sel_window, ns)
    blk_onehot = (jnp.arange(T)[:, None] // SLC_BLOCK % wb == jnp.arange(wb)[None, :]).astype(BF16)
    k_aug = jnp.concatenate([zn[ks0:vs0], jnp.broadcast_to(blk_onehot, (NSA_KV_GROUPS, T, wb))], axis=2)
    vt_aug = jnp.concatenate([jnp.swapaxes(zn[vs0:vs0 + NSA_KV_GROUPS], 1, 2),
                              jnp.ones((NSA_KV_GROUPS, SEL_ONES_ROWS, T), BF16)], axis=1)
    o_s = _sel_attn(zn, k_aug, vt_aug, sel, tq, tk)
    return _win_combine(zn, zr, o_c, o_s, tq)


def _softplus(y):
    return jnp.maximum(y, 0.0) + jnp.log(1.0 + jnp.exp(-jnp.abs(y)))


def _rwkv_kernel(z_ref, prev_ref, mu_ref, pv_ref, wup_ref, aup_ref, gup_ref, o_ref,
                 rt_s, at_s, kh_s, bh_s, kb_s, bb_s, v_s, gc_s, g_s, bonus_s, y_s, state_s, *, tb):
    C = RWKV_CHUNK
    step = pl.program_id(0)

    @pl.when(step == 0)
    def _():
        state_s[...] = jnp.zeros(state_s.shape, F32)

    has_prev = jnp.where(step > 0, 1.0, 0.0)
    row = lax.broadcasted_iota(jnp.int32, (tb, LANES), 0)

    def shifted_mix(t):
        z = z_ref[t]
        zp = jnp.where(row == 0, prev_ref[t, 7:8, :] * has_prev, pltpu.roll(z, 1, axis=0))
        return z + (zp - z) * mu_ref[t]

    ri = lax.broadcasted_iota(jnp.int32, (tb, tb), 0)
    ci = lax.broadcasted_iota(jnp.int32, (tb, tb), 1)
    tri = jnp.where(((ri >> 6) == (ci >> 6)) & (ci <= ri), 1.0, 0.0).astype(BF16)
    wide = 2 * LANES
    wr = lax.broadcasted_iota(jnp.int32, (wide, wide), 0)
    wc = lax.broadcasted_iota(jnp.int32, (wide, wide), 1)
    head_sum = jnp.where((wr >> 6) == (wc >> 6), 1.0, 0.0).astype(BF16)
    head_mean = jnp.where((wr >> 6) == (wc >> 6), 1.0 / RWKV_HEAD_DIM, 0.0).astype(BF16)

    wa = shifted_mix(3 * RWKV_PAIRS)
    dw = _dot(jnp.tanh(wa).astype(BF16), wup_ref[...])
    da = _dot(wa.astype(BF16), aup_ref[...])
    g0 = jax.nn.sigmoid(shifted_mix(3 * RWKV_PAIRS + 1)).astype(BF16)
    g1 = jax.nn.sigmoid(shifted_mix(3 * RWKV_PAIRS + 2)).astype(BF16)
    g_s[...] = _dot(g0, gup_ref[:LANES]) + _dot(g1, gup_ref[LANES:])

    def mixed(first_tile, q):
        return jnp.concatenate([shifted_mix(first_tile + 2 * q), shifted_mix(first_tile + 2 * q + 1)], axis=1)

    for q in range(RWKV_PAIRS // 2):
        sl = slice(q * wide, (q + 1) * wide)
        r = mixed(0, q)
        k = mixed(RWKV_PAIRS, q)
        v = mixed(2 * RWKV_PAIRS, q)
        w_log = -_softplus(-(pv_ref[0:1, sl] + dw[:, sl])) - 0.5
        lw = -jnp.exp(w_log)
        icl = jax.nn.sigmoid(pv_ref[1:2, sl] + da[:, sl])
        kk = k * pv_ref[2:3, sl]
        kk = kk * lax.rsqrt(jnp.maximum(_dot((kk * kk).astype(BF16), head_sum), 1e-24))
        k2 = k * (1.0 + (icl - 1.0) * pv_ref[3:4, sl])
        bonus_s[:, sl] = _dot((r * k2 * pv_ref[4:5, sl]).astype(BF16), head_sum) * v
        hi, mid, lo = _split3(lw)
        cum = _dot(tri, hi) + _dot(tri, mid) + _dot(tri, lo)
        b = kk * icl
        inv_decay = jnp.exp(-cum)
        to_end = []
        for c in range(tb // C):
            e_end = jnp.exp(cum[c * C + C - 1:c * C + C])
            gc_s[8 * c:8 * c + 1, sl] = e_end
            to_end.append(inv_decay[c * C:(c + 1) * C] * e_end)
        to_end = jnp.concatenate(to_end, axis=0)
        rt_s[:, sl] = (r * jnp.exp(cum)).astype(BF16)
        at_s[:, sl] = (-kk * jnp.exp(cum - lw)).astype(BF16)
        kh_s[:, sl] = (k2 * inv_decay).astype(BF16)
        bh_s[:, sl] = (b * inv_decay).astype(BF16)
        kb_s[:, sl] = (k2 * to_end).astype(BF16)
        bb_s[:, sl] = (b * to_end).astype(BF16)
        v_s[:, sl] = v.astype(BF16)

    hr = lax.broadcasted_iota(jnp.int32, (LANES, LANES), 0)
    hc = lax.broadcasted_iota(jnp.int32, (LANES, LANES), 1)
    same_head = (hr >> 6) == (hc >> 6)
    t_r = hr & (C - 1)
    t_c = hc & (C - 1)
    strict = same_head & (t_c < t_r)
    incl = same_head & (t_c <= t_r)
    eye = jnp.where(hr == hc, 1.0, 0.0)
    lane_head = lax.broadcasted_iota(jnp.int32, (C, LANES), 1) >> 6

    def stack(zc):
        return jnp.concatenate([jnp.where(lane_head == 0, zc, jnp.zeros_like(zc)),
                                jnp.where(lane_head == 1, zc, jnp.zeros_like(zc))], axis=0)

    def chunk(c, carry):
        t0 = pl.multiple_of(c * C, C)
        rows = pl.ds(t0, C)
        pairs = range(RWKV_PAIRS)
        sls = [slice(p * LANES, (p + 1) * LANES) for p in pairs]
        lhs = [jnp.concatenate([stack(at_s[rows, sl]), stack(rt_s[rows, sl])], axis=0) for sl in sls]
        rhs = [jnp.concatenate([stack(kh_s[rows, sl]), stack(bh_s[rows, sl])], axis=0) for sl in sls]
        aa = [_dot_nt(lhs[p], rhs[p]) for p in pairs]
        a_ak = [jnp.where(strict, aa[p][:2 * C, :2 * C], 0.0).astype(BF16) for p in pairs]
        n_pow = [jnp.where(strict, aa[p][:2 * C, 2 * C:], 0.0) for p in pairs]
        a_r = [jnp.concatenate([jnp.where(incl, aa[p][2 * C:, :2 * C], 0.0).astype(BF16),
                                jnp.where(incl, aa[p][2 * C:, 2 * C:], 0.0).astype(BF16)], axis=1) for p in pairs]
        t_inv = [eye + n_pow[p] for p in pairs]
        for _ in range(5):
            nb = [n_pow[p].astype(BF16) for p in pairs]
            n_pow = [_dot(nb[p], nb[p]) for p in pairs]
            t_inv = [t_inv[p] + _dot(t_inv[p].astype(BF16), n_pow[p].astype(BF16)) for p in pairs]
        state = [state_s[p] for p in pairs]
        xs = [_dot_nt(lhs[p], state[p].astype(BF16)) for p in pairs]
        v_c = [v_s[rows, sl] for sl in sls]
        v_st = [stack(v_c[p]) for p in pairs]
        av = [_dot(a_ak[p], v_st[p]) for p in pairs]
        sa = [_dot(t_inv[p].astype(BF16), (xs[p][:2 * C] + av[p]).astype(BF16)) for p in pairs]
        ys = [xs[p][2 * C:] + _dot(a_r[p], jnp.concatenate([v_st[p], sa[p].astype(BF16)], axis=0)) for p in pairs]
        upd = [_dot_tn(jnp.concatenate([v_c[p], (sa[p][:C] + sa[p][C:]).astype(BF16)], axis=0),
                       jnp.concatenate([kb_s[rows, sls[p]], bb_s[rows, sls[p]]], axis=0)) for p in pairs]
        for p in pairs:
            y_s[rows, sls[p]] = ys[p][:C] + ys[p][C:]
            state_s[p] = (state[p] * gc_s[pl.ds(pl.multiple_of(c * 8, 8), 1), sls[p]]
                          + jnp.where(same_head, upd[p], 0.0))
        return carry

    lax.fori_loop(0, tb // C, chunk, 0)

    for q in range(RWKV_PAIRS // 2):
        sl = slice(q * wide, (q + 1) * wide)
        y = y_s[:, sl]
        d = y - _dot(y.astype(BF16), head_mean)
        var = _dot((d * d).astype(BF16), head_mean)
        yn = d * lax.rsqrt(var + LNX_EPS) * pv_ref[5:6, sl] + pv_ref[6:7, sl]
        o_ref[:, sl] = (yn + bonus_s[:, sl]) * g_s[:, sl]


def _hi_lo(w):
    hi = w.astype(BF16)
    return hi, (w - hi.astype(F32)).astype(BF16)


def _rwkv(zr, mu_t, w0, w_up, a0, a_up, g_up, k_k, k_a, r_k, lnx_g, lnx_b, tb):
    T = zr.shape[1]
    n_in = RW_TILES - 1
    pv = jnp.stack([w0, a0, k_k, k_a, r_k.reshape(-1), lnx_g, lnx_b, jnp.zeros_like(w0)])
    wup = jnp.pad(w_up, ((0, LANES - W_RANK), (0, 0))).astype(BF16)
    aup = jnp.pad(a_up, ((W_RANK, LANES - W_RANK - A_RANK), (0, 0))).astype(BF16)
    gup = jnp.pad(g_up, ((0, 2 * LANES - G_RANK), (0, 0))).astype(BF16)
    full = lambda shape: pl.BlockSpec(shape, lambda s: (0,) * len(shape))
    kern = functools.partial(_rwkv_kernel, tb=tb)
    bf = lambda: pltpu.VMEM((tb, RWKV_WIDTH), BF16)
    ff = lambda: pltpu.VMEM((tb, RWKV_WIDTH), F32)
    return pl.pallas_call(
        kern,
        grid=(T // tb,),
        in_specs=[
            pl.BlockSpec((n_in, tb, LANES), lambda s: (0, s, 0)),
            pl.BlockSpec((n_in, 8, LANES), lambda s: (0, jnp.maximum(s * (tb // 8) - 1, 0), 0)),
            full((n_in, 1, LANES)),
            full((8, RWKV_WIDTH)),
            full((LANES, RWKV_WIDTH)), full((LANES, RWKV_WIDTH)), full((2 * LANES, RWKV_WIDTH)),
        ],
        out_specs=pl.BlockSpec((tb, RWKV_WIDTH), lambda s: (s, 0)),
        out_shape=jax.ShapeDtypeStruct((T, RWKV_WIDTH), F32),
        scratch_shapes=[bf(), bf(), bf(), bf(), bf(), bf(), bf(),
                        pltpu.VMEM((8 * (tb // RWKV_CHUNK), RWKV_WIDTH), F32), ff(), ff(), ff(),
                        pltpu.VMEM((RWKV_PAIRS, LANES, LANES), F32)],
        compiler_params=_cparams(("arbitrary",)),
        name="rwkv7",
    )(zr, zr, mu_t, pv, wup, aup, gup)


PROJ_COLS = 512


def _mix_out_kernel(ya_ref, yb_ref, g_ref, w_ref, x_ref, o_ref):
    h = jnp.concatenate([_rms(ya_ref[...], g_ref[...]).astype(BF16), yb_ref[...].astype(BF16)], axis=1)
    for n in range(D_MODEL // PROJ_COLS):
        sl = slice(n * PROJ_COLS, (n + 1) * PROJ_COLS)
        o_ref[:, sl] = x_ref[:, sl] + _dot(h, w_ref[:, sl])


def _mix_out(y_a, y_b, g, w, x, tm):
    T = x.shape[0]
    return pl.pallas_call(
        _mix_out_kernel,
        grid=(T // tm,),
        in_specs=[
            pl.BlockSpec((tm, NSA_WIDTH), lambda m: (m, 0)),
            pl.BlockSpec((tm, RWKV_WIDTH), lambda m: (m, 0)),
            _resident((1, NSA_WIDTH)),
            _resident((NSA_WIDTH + RWKV_WIDTH, D_MODEL)),
            pl.BlockSpec((tm, D_MODEL), lambda m: (m, 0)),
        ],
        out_specs=pl.BlockSpec((tm, D_MODEL), lambda m: (m, 0)),
        out_shape=jax.ShapeDtypeStruct((T, D_MODEL), F32),
        compiler_params=_cparams(("parallel",)),
        name="mix_out_proj",
    )(y_a, y_b, g, w, x)


def _norm_mm_kernel(x_ref, g_ref, w_ref, o_ref, h_scr):
    @pl.when(pl.program_id(1) == 0)
    def _():
        h_scr[...] = _rms(x_ref[...], g_ref[...]).astype(BF16)

    o_ref[...] = _dot(h_scr[...], w_ref[...]).astype(o_ref.dtype)


def _norm_mm(x, g, w, tm, tn, name):
    M, K = x.shape
    N = w.shape[1]
    return pl.pallas_call(
        _norm_mm_kernel,
        grid=(M // tm, N // tn),
        in_specs=[
            pl.BlockSpec((tm, K), lambda m, n: (m, 0)),
            pl.BlockSpec((1, K), lambda m, n: (0, 0)),
            pl.BlockSpec((K, tn), lambda m, n: (0, n)),
        ],
        out_specs=pl.BlockSpec((tm, tn), lambda m, n: (m, n)),
        out_shape=jax.ShapeDtypeStruct((M, N), BF16),
        scratch_shapes=[pltpu.VMEM((tm, K), BF16)],
        compiler_params=_cparams(("parallel", "arbitrary")),
        name=name,
    )(x, g, w)


def _mem_router_kernel(x_ref, km_ref, vm_ref, wq_ref, wo_ref, g2_ref, g3_ref, w_hi, w_lo, b_ref,
                       x2_ref, h_ref, c_ref):
    scale = MEM_HEAD_DIM ** -0.5
    hq = _rms(x_ref[...], g2_ref[...]).astype(BF16)
    heads = []
    for hd in range(MEM_HEADS):
        sl = slice(hd * MEM_HEAD_DIM, (hd + 1) * MEM_HEAD_DIM)
        q = _dot(hq, wq_ref[:, sl]).astype(BF16)
        s = _dot_nt(q, km_ref[:, sl]) * scale
        e = jnp.exp(s - jnp.max(s, axis=-1, keepdims=True))
        p = e / jnp.sum(e, axis=-1, keepdims=True)
        heads.append(_dot(p.astype(BF16), vm_ref[:, sl]).astype(BF16))
    o = jnp.concatenate(heads, axis=1)
    for n in range(D_MODEL // PROJ_COLS):
        sl = slice(n * PROJ_COLS, (n + 1) * PROJ_COLS)
        x2_ref[:, sl] = x_ref[:, sl] + _dot(o, wo_ref[:, sl])
    _route(x2_ref[...], g3_ref, w_hi, w_lo, b_ref, h_ref, c_ref)


def _mem_router(x, k_mem, v_mem, wq, wo, g2, g3, w_hi, w_lo, b, tm):
    T = x.shape[0]
    M = k_mem.shape[0]
    row_block = lambda width: pl.BlockSpec((tm, width), lambda m: (m, 0))
    return pl.pallas_call(
        _mem_router_kernel,
        grid=(T // tm,),
        in_specs=[
            row_block(D_MODEL),
            _resident((M, D_MODEL)), _resident((M, D_MODEL)),
            _resident((D_MODEL, D_MODEL)), _resident((D_MODEL, D_MODEL)),
            _resident((1, D_MODEL)), _resident((1, D_MODEL)),
            _resident((D_MODEL, LANES)), _resident((D_MODEL, LANES)), _resident((1, LANES)),
        ],
        out_specs=[row_block(D_MODEL), row_block(D_MODEL), row_block(LANES)],
        out_shape=[
            jax.ShapeDtypeStruct((T, D_MODEL), F32),
            jax.ShapeDtypeStruct((T, D_MODEL), BF16),
            jax.ShapeDtypeStruct((T, LANES), F32),
        ],
        compiler_params=_cparams(("parallel",)),
        name="mem_attn_router",
    )(x, k_mem, v_mem, wq, wo, g2, g3, w_hi, w_lo, b)


def _route(x, g_ref, w_hi, w_lo, b_ref, h_ref, c_ref):
    h = _rms(x, g_ref[...])
    h_ref[...] = h.astype(BF16)
    logits = _dot_xw3(h, w_hi[...], w_lo[...]) + b_ref[...]
    lane = lax.broadcasted_iota(jnp.int32, logits.shape, 1)
    big = jnp.int32(LANES)
    is_grp = (lane >= N_EXPERTS) & (lane < N_EXPERTS + N_GROUPS)
    lg = jnp.where(is_grp, logits, NEG)
    eg = jnp.where(is_grp, jnp.exp(lg - jnp.max(lg, axis=-1, keepdims=True)), 0.0)
    pg = eg / jnp.sum(eg, axis=-1, keepdims=True)
    pg_top = jnp.max(pg, axis=-1, keepdims=True)
    g_idx = jnp.min(jnp.where(is_grp & (pg == pg_top), lane - N_EXPERTS, big), axis=-1, keepdims=True)
    in_grp = (lane < N_EXPERTS) & ((lane >> 3) == g_idx)
    le = jnp.where(in_grp, logits, NEG)
    ee = jnp.where(in_grp, jnp.exp(le - jnp.max(le, axis=-1, keepdims=True)), 0.0)
    pe = jnp.where(in_grp, ee / jnp.sum(ee, axis=-1, keepdims=True), -1.0)
    p1 = jnp.max(pe, axis=-1, keepdims=True)
    hit1 = lane == jnp.min(jnp.where(pe == p1, lane, big), axis=-1, keepdims=True)
    pe2 = jnp.where(hit1, -1.0, pe)
    p2 = jnp.max(pe2, axis=-1, keepdims=True)
    hit2 = lane == jnp.min(jnp.where(pe2 == p2, lane, big), axis=-1, keepdims=True)
    denom = p1 + p2
    c_ref[...] = jnp.where(hit1, pg_top * p1 / denom, 0.0) + jnp.where(hit2, pg_top * p2 / denom, 0.0)


def _moe_kernel(h_ref, c_ref, wg_ref, wu_ref, wd_ref, x_ref, gf_ref, o_ref, acc_scr):
    e = pl.program_id(1)

    @pl.when(e == 0)
    def _():
        acc_scr[...] = jnp.zeros(acc_scr.shape, F32)

    h = h_ref[...]
    comb = c_ref[...]
    lane = lax.broadcasted_iota(jnp.int32, comb.shape, 1)
    c_e = jnp.sum(jnp.where(lane == e, comb, 0.0), axis=-1, keepdims=True)
    hid = jax.nn.silu(_dot(h, wg_ref[0])) * _dot(h, wu_ref[0])
    acc_scr[...] += _dot((hid * c_e).astype(BF16), wd_ref[0])

    @pl.when(e == pl.num_programs(1) - 1)
    def _():
        o_ref[...] = _rms(x_ref[...] + acc_scr[...], gf_ref[...])


def _moe(h, comb, wg, wu, wd, x, gf, tm):
    T = x.shape[0]
    return pl.pallas_call(
        _moe_kernel,
        grid=(T // tm, N_EXPERTS),
        in_specs=[
            pl.BlockSpec((tm, D_MODEL), lambda m, e: (m, 0)),
            pl.BlockSpec((tm, LANES), lambda m, e: (m, 0)),
            pl.BlockSpec((1, D_MODEL, EXPERT_FF), lambda m, e: (e, 0, 0)),
            pl.BlockSpec((1, D_MODEL, EXPERT_FF), lambda m, e: (e, 0, 0)),
            pl.BlockSpec((1, EXPERT_FF, D_MODEL), lambda m, e: (e, 0, 0)),
            pl.BlockSpec((tm, D_MODEL), lambda m, e: (m, 0)),
            pl.BlockSpec((1, D_MODEL), lambda m, e: (0, 0)),
        ],
        out_specs=pl.BlockSpec((tm, D_MODEL), lambda m, e: (m, 0)),
        out_shape=jax.ShapeDtypeStruct((T, D_MODEL), F32),
        scratch_shapes=[pltpu.VMEM((tm, D_MODEL), F32)],
        compiler_params=_cparams(("parallel", "arbitrary")),
        name="moe_experts",
    )(h, comb, wg, wu, wd, x, gf)


def _pack_w_in(w_in, rwkv_mu):
    nsa_cols = NSA_WIDTH + 6 * NSA_KV_WIDTH
    gl = w_in[:, nsa_cols:nsa_cols + 3 * NSA_HEADS]
    rw0 = nsa_cols + 3 * NSA_HEADS
    rkv = w_in[:, rw0:rw0 + 3 * RWKV_WIDTH + W_RANK + A_RANK]
    gd = w_in[:, rw0 + 3 * RWKV_WIDTH + W_RANK + A_RANK:]

    def pad(a, n):
        return jnp.pad(a, ((0, 0), (0, n - a.shape[1])))

    w = jnp.concatenate([w_in[:, :nsa_cols], rkv, pad(gd, 2 * LANES), pad(gl, LANES)], axis=1)
    mu = rwkv_mu.reshape(1, -1)
    mu_rkv = mu[:, :3 * RWKV_WIDTH + W_RANK + A_RANK]
    mu_gd = pad(mu[:, 3 * RWKV_WIDTH + W_RANK + A_RANK:], 2 * LANES)
    mu_t = jnp.concatenate([mu_rkv, mu_gd], axis=1).reshape(RW_TILES - 1, 1, LANES)
    return w.astype(BF16), mu_t


def kernel(x, mem, ln1_g, w_in, cmp_pos_k, cmp_w1_k, cmp_w2_k, cmp_pos_v, cmp_w1_v, cmp_w2_v, nsa_norm_g, rwkv_mu, rwkv_w0, rwkv_w_up, rwkv_a0, rwkv_a_up, rwkv_g_up, rwkv_k_k, rwkv_k_a, rwkv_r_k, rwkv_lnx_g, rwkv_lnx_b, w_out, ln_mem_g, ln2_g, wq_mem, wk_mem, wv_mem, wo_mem, ln3_g, router_group_w, router_group_b, router_expert_w, router_expert_b, moe_w_gate, moe_w_up, moe_w_down, lnf_g):
    B, T, _ = x.shape
    assert B == 1 and T % max(TM_PROJ, TK_SEL, TB_RWKV, TM_MOE) == 0
    assert w_in.shape[0] == 1
    row = lambda a: a.reshape(1, -1)
    xs = x[0]
    for l in range(w_in.shape[0]):
        wp, mu_t = _pack_w_in(w_in[l], rwkv_mu[l])
        zn, zr = _in_proj(xs, row(ln1_g[l]), wp, TM_PROJ)
        y_a = _nsa(zn, zr, cmp_pos_k[l], cmp_w1_k[l], cmp_w2_k[l], cmp_pos_v[l], cmp_w1_v[l], cmp_w2_v[l],
                   TQ_NSA, TK_SEL)
        y_b = _rwkv(zr, mu_t, rwkv_w0[l], rwkv_w_up[l], rwkv_a0[l], rwkv_a_up[l], rwkv_g_up[l], rwkv_k_k[l],
                    rwkv_k_a[l], rwkv_r_k[l], rwkv_lnx_g[l], rwkv_lnx_b[l], TB_RWKV)
        xs = _mix_out(y_a, y_b, row(nsa_norm_g[l]), w_out[l].astype(BF16), xs, TM_PROJ)
        m_tok = mem[0]
        k_mem = _norm_mm(m_tok, row(ln_mem_g[l]), wk_mem[l].astype(BF16), m_tok.shape[0], PROJ_COLS, "mem_k")
        v_mem = _norm_mm(m_tok, row(ln_mem_g[l]), wv_mem[l].astype(BF16), m_tok.shape[0], PROJ_COLS, "mem_v")
        w_r = jnp.pad(jnp.concatenate([router_expert_w[l], router_group_w[l]], axis=1),
                      ((0, 0), (0, LANES - N_EXPERTS - N_GROUPS)))
        b_r = jnp.pad(jnp.concatenate([router_expert_b[l], router_group_b[l]]), (0, LANES - N_EXPERTS - N_GROUPS))
        xs, h3, comb = _mem_router(xs, k_mem, v_mem, wq_mem[l].astype(BF16), wo_mem[l].astype(BF16),
                                   row(ln2_g[l]), row(ln3_g[l]), *_hi_lo(w_r), row(b_r), TM_PROJ)
        xs = _moe(h3, comb, moe_w_gate[l].astype(BF16), moe_w_up[l].astype(BF16), moe_w_down[l].astype(BF16),
                  xs, row(lnf_g), TM_MOE)
    return xs[None]
```

```python
import functools

import numpy as np
import jax
import jax.numpy as jnp
from jax import lax
from jax.experimental import pallas as pl
from jax.experimental.pallas import tpu as pltpu

F32 = jnp.float32
BF16 = jnp.bfloat16

LANES = 128
D_MODEL = 2048
EPS = 1e-6
NSA_HEAD_DIM = 128
NSA_HEADS = 8
NSA_KV_GROUPS = 2
NSA_HPG = NSA_HEADS // NSA_KV_GROUPS
NSA_WIDTH = NSA_HEADS * NSA_HEAD_DIM
NSA_KV_WIDTH = NSA_KV_GROUPS * NSA_HEAD_DIM
CMP_BLOCK = 32
CMP_STRIDE = 16
SLC_BLOCK = 64
SLC_TOPK = 16
WINDOW = 512
NEG = -1e30
RWKV_HEAD_DIM = 64
RWKV_HEADS = 16
RWKV_WIDTH = RWKV_HEADS * RWKV_HEAD_DIM
RWKV_PAIRS = RWKV_WIDTH // LANES
W_RANK = 64
A_RANK = 64
G_RANK = 160
LNX_EPS = 64e-5
RWKV_CHUNK = 64
MEM_HEADS = 4
MEM_HEAD_DIM = D_MODEL // MEM_HEADS
N_GROUPS = 4
EXPERTS_PER_GROUP = 8
N_EXPERTS = N_GROUPS * EXPERTS_PER_GROUP
EXPERT_FF = 256

NSA_TILES = 20
RW_TILES = 28
IN_TILE_BLOCK = 4

VMEM_LIMIT = 56 * 1024 * 1024

TM_PROJ = 256
TQ_NSA = 128
TK_SEL = 256
TB_RWKV = 256
TM_MOE = 1024
SUB_MOE = 128
GROUP_LANE = 64


def _cparams(sem):
    return pltpu.CompilerParams(dimension_semantics=sem, vmem_limit_bytes=VMEM_LIMIT)


def _dot(a, b):
    return jnp.dot(a, b, preferred_element_type=F32)


def _dot_nt(a, b):
    return lax.dot_general(a, b, (((1,), (1,)), ((), ())), preferred_element_type=F32)


def _dot_tn(a, b):
    return lax.dot_general(a, b, (((0,), (0,)), ((), ())), preferred_element_type=F32)


def _split2(x):
    hi = x.astype(BF16)
    lo = (x - hi.astype(F32)).astype(BF16)
    return hi, lo


def _split3(x):
    hi = x.astype(BF16)
    r1 = x - hi.astype(F32)
    mid = r1.astype(BF16)
    lo = (r1 - mid.astype(F32)).astype(BF16)
    return hi, mid, lo


def _dot_x2(x, w):
    hi, lo = _split2(x)
    return _dot(hi, w) + _dot(lo, w)


def _dot_x3(x, w):
    hi, mid, lo = _split3(x)
    return _dot(hi, w) + _dot(mid, w) + _dot(lo, w)


def _dot_xw3(x, w_hi, w_lo):
    hi, lo = _split2(x)
    return _dot(hi, w_hi) + _dot(lo, w_hi) + _dot(hi, w_lo)


def _rms(x, g):
    ms = jnp.mean(x * x, axis=-1, keepdims=True)
    return x * lax.rsqrt(ms + EPS) * g


def _resident(shape):
    return pl.BlockSpec(shape, lambda *_: (0,) * len(shape), pipeline_mode=pl.Buffered(1))


def _in_proj_kernel(x_ref, g_ref, w_ref, zn_ref, zr_ref, *, q_scale):
    h = _rms(x_ref[...], g_ref[...]).astype(BF16)
    nb = IN_TILE_BLOCK * LANES
    for n in range((NSA_TILES + RW_TILES) // IN_TILE_BLOCK):
        z = _dot(h, w_ref[:, n * nb:(n + 1) * nb])
        for t in range(IN_TILE_BLOCK):
            tile = n * IN_TILE_BLOCK + t
            zt = z[:, t * LANES:(t + 1) * LANES]
            if tile < NSA_HEADS:
                zn_ref[tile] = (zt * q_scale).astype(BF16)
            elif tile < NSA_TILES:
                zn_ref[tile] = zt.astype(BF16)
            else:
                zr_ref[tile - NSA_TILES] = zt


def _in_proj(x, g, w, tm):
    T = x.shape[0]
    kern = functools.partial(_in_proj_kernel, q_scale=NSA_HEAD_DIM ** -0.5)
    return pl.pallas_call(
        kern,
        grid=(T // tm,),
        in_specs=[
            pl.BlockSpec((tm, D_MODEL), lambda m: (m, 0)),
            _resident((1, D_MODEL)),
            _resident((D_MODEL, (NSA_TILES + RW_TILES) * LANES)),
        ],
        out_specs=[
            pl.BlockSpec((NSA_TILES, tm, LANES), lambda m: (0, m, 0)),
            pl.BlockSpec((RW_TILES, tm, LANES), lambda m: (0, m, 0)),
        ],
        out_shape=[
            jax.ShapeDtypeStruct((NSA_TILES, T, LANES), BF16),
            jax.ShapeDtypeStruct((RW_TILES, T, LANES), F32),
        ],
        compiler_params=_cparams(("parallel",)),
        name="in_proj",
    )(x, g, w)


def _compress_kernel(x_ref, pos_ref, w1_ref, w2_ref, o_ref):
    half = CMP_STRIDE * NSA_HEAD_DIM
    x = x_ref[0]
    w1 = w1_ref[0]
    h_first = _dot(x, w1[:half])
    h_second = _dot(x, w1[half:])
    n = x.shape[0]
    h_next = pltpu.roll(h_second, n - 1, axis=0)
    bias = _dot(pos_ref[0], w1)[0:1]
    hid = jax.nn.gelu(h_first + h_next + bias)
    o_ref[0] = _dot(hid.astype(BF16), w2_ref[0]).astype(o_ref.dtype)


def _compress(zn16, pos, w1, w2):
    n = zn16.shape[1]
    width = CMP_BLOCK * NSA_HEAD_DIM
    return pl.pallas_call(
        _compress_kernel,
        grid=(4,),
        in_specs=[
            pl.BlockSpec((1, n, CMP_STRIDE * NSA_HEAD_DIM), lambda s: (NSA_HEADS + s, 0, 0)),
            pl.BlockSpec((1, 8, width), lambda s: (s // 2, 0, 0)),
            pl.BlockSpec((1, width, NSA_HEAD_DIM), lambda s: (s // 2, 0, 0)),
            pl.BlockSpec((1, NSA_HEAD_DIM, NSA_HEAD_DIM), lambda s: (s // 2, 0, 0)),
        ],
        out_specs=pl.BlockSpec((1, n, NSA_HEAD_DIM), lambda s: (s, 0, 0)),
        out_shape=jax.ShapeDtypeStruct((4, n, NSA_HEAD_DIM), BF16),
        compiler_params=_cparams(("parallel",)),
        name="nsa_compress",
    )(zn16, pos, w1, w2)


def _cmp_select_kernel(q_ref, kc_ref, vc_ref, m_ref, oc_ref, sel_ref, *, tq, top_n):
    q0 = pl.program_id(1) * tq
    rows = NSA_HPG * tq
    ncp = kc_ref.shape[1]
    ns = m_ref.shape[0]
    q = q_ref[...].reshape(rows, NSA_HEAD_DIM)
    s = _dot_nt(q, kc_ref[0])
    qpos = q0 + (lax.broadcasted_iota(jnp.int32, (rows, 1), 0) & (tq - 1))
    cmp_end = lax.broadcasted_iota(jnp.int32, (1, ncp), 1) * CMP_STRIDE + (CMP_BLOCK - 1)
    mask = cmp_end <= qpos
    s = jnp.where(mask, s, NEG)
    mx = jnp.max(s, axis=-1, keepdims=True)
    e = jnp.where(mask, jnp.exp(s - mx), 0.0)
    p = e / jnp.maximum(jnp.sum(e, axis=-1, keepdims=True), 1e-30)
    o = _dot(p.astype(BF16), vc_ref[0])
    for h in range(NSA_HPG):
        oc_ref[:, h * NSA_HEAD_DIM:(h + 1) * NSA_HEAD_DIM] = o[h * tq:(h + 1) * tq]
    psum = p[0:tq]
    for h in range(1, NSA_HPG):
        psum = psum + p[h * tq:(h + 1) * tq]
    ps_hi, ps_lo = _split2(psum)
    p_sel = _dot_nt(m_ref[...], ps_hi) + _dot_nt(m_ref[...], ps_lo)
    blk = (q0 + lax.broadcasted_iota(jnp.int32, (1, tq), 1)) >> 6
    j = lax.broadcasted_iota(jnp.int32, (ns, tq), 0)
    future = j > blk
    forced = (j == 0) | (j == blk) | (j == blk - 1)
    score = jnp.where(future, -1.0, jnp.where(forced, 1e6, p_sel))
    sel = jnp.zeros((ns, tq), F32)
    for _ in range(top_n):
        best = jnp.max(score, axis=0, keepdims=True)
        idx = jnp.min(jnp.where(score == best, j, ns), axis=0, keepdims=True)
        hit = j == idx
        sel = jnp.where(hit & (best >= 0.0), 1.0, sel)
        score = jnp.where(hit, -2.0, score)
    sel_ref[0] = sel.astype(sel_ref.dtype)


def _cmp_select(zn, kv_cmp, cmp_to_sel, tq):
    T = zn.shape[1]
    ncp = kv_cmp.shape[1]
    ns = cmp_to_sel.shape[0]
    kern = functools.partial(_cmp_select_kernel, tq=tq, top_n=min(SLC_TOPK, ns))
    return pl.pallas_call(
        kern,
        grid=(NSA_KV_GROUPS, T // tq),
        in_specs=[
            pl.BlockSpec((NSA_HPG, tq, NSA_HEAD_DIM), lambda g, i: (g, i, 0)),
            pl.BlockSpec((1, ncp, NSA_HEAD_DIM), lambda g, i: (g, 0, 0)),
            pl.BlockSpec((1, ncp, NSA_HEAD_DIM), lambda g, i: (NSA_KV_GROUPS + g, 0, 0)),
            pl.BlockSpec((ns, ncp), lambda g, i: (0, 0)),
        ],
        out_specs=[
            pl.BlockSpec((tq, NSA_HPG * NSA_HEAD_DIM), lambda g, i: (i, g)),
            pl.BlockSpec((1, ns, tq), lambda g, i: (g, 0, i)),
        ],
        out_shape=[
            jax.ShapeDtypeStruct((T, NSA_WIDTH), F32),
            jax.ShapeDtypeStruct((NSA_KV_GROUPS, ns, T), BF16),
        ],
        compiler_params=_cparams(("parallel", "parallel")),
        name="nsa_cmp_select",
    )(zn, kv_cmp, kv_cmp, cmp_to_sel)


SEL_ONES_ROWS = 16


SEL_BUFFERS = 4


def _sel_attn_kernel(q_ref, k_ref, vt_ref, sel_ref, o_ref, bias_scr, qa_scr, acc_scr, m_scr, *bufs, tq, tk, wb):
    s_bufs, p_bufs = bufs[:SEL_BUFFERS], bufs[SEL_BUFFERS:]
    q0 = pl.program_id(1) * tq
    cols = NSA_HPG * tq
    tpw = wb * SLC_BLOCK // tk
    last_tile = k_ref.shape[1] // tk - 1
    ns = sel_ref.shape[1]
    before = lax.broadcasted_iota(jnp.int32, (ns, tq), 0) < q0 // SLC_BLOCK
    not_chosen = jnp.where(before & (sel_ref[0].astype(F32) > 0.5), 0.0, NEG)
    for h in range(NSA_HPG):
        bias_scr[:, h * tq:(h + 1) * tq] = not_chosen
        qa_scr[:NSA_HEAD_DIM, h * tq:(h + 1) * tq] = q_ref[h].astype(F32).T.astype(BF16)

    def set_window(w):
        rows = pl.ds(pl.multiple_of(w * wb, wb), wb)
        qa_scr[NSA_HEAD_DIM:, :] = bias_scr[rows, :].astype(BF16)

    def scores(j):
        j = jnp.minimum(j, last_tile)
        return _dot(k_ref[0, pl.ds(pl.multiple_of(j * tk, tk), tk), :], qa_scr[...])

    def pv(j, p_ref):
        return _dot(vt_ref[0, :, pl.ds(pl.multiple_of(j * tk, tk), tk)], p_ref[...])

    def softmax_tile(s_ref, p_ref):
        m_old = m_scr[...]
        m_new = jnp.maximum(m_old, jnp.max(s_ref[...], axis=0, keepdims=True))
        m_scr[...] = m_new
        p_ref[...] = jnp.exp((s_ref[...] - m_new).astype(BF16))
        return jnp.exp(m_old - m_new)

    own = pl.ds(pl.multiple_of(q0, tq), tq)
    s = _dot(k_ref[0, own, :NSA_HEAD_DIM], qa_scr[:NSA_HEAD_DIM, :])
    kpos = lax.broadcasted_iota(jnp.int32, (tq, 1), 0)
    qpos = lax.broadcasted_iota(jnp.int32, (1, cols), 1) & (tq - 1)
    s = jnp.where(kpos <= qpos, s, NEG)
    m_first = jnp.max(s, axis=0, keepdims=True)
    m_scr[...] = m_first
    acc_scr[...] = _dot(vt_ref[0, :, own], jnp.exp((s - m_first).astype(BF16)))

    n_tiles = (q0 + tk - 1) // tk

    nbuf = len(s_bufs)

    def window(w, carry):
        lo = w * tpw
        cnt = jnp.minimum(n_tiles - lo, tpw)
        set_window(w)
        s_bufs[0][...] = scores(lo)
        p_bufs[nbuf - 1][...] = jnp.zeros(p_bufs[nbuf - 1].shape, BF16)

        def group(t, alpha):
            first = lo + nbuf * t
            for i in range(nbuf):
                j = first + i
                s_bufs[(i + 1) % nbuf][...] = scores(j + 1)
                acc_scr[...] = alpha * acc_scr[...] + pv(jnp.maximum(j - 1, 0), p_bufs[(i - 1) % nbuf])
                alpha = softmax_tile(s_bufs[i], p_bufs[i])
            return alpha

        groups = (cnt + nbuf - 1) // nbuf
        alpha_last = lax.fori_loop(0, groups, group, jnp.ones((1, cols), F32))
        acc_scr[...] = alpha_last * acc_scr[...] + pv(lo + nbuf * groups - 1, p_bufs[nbuf - 1])
        return carry

    lax.fori_loop(0, (n_tiles + tpw - 1) // tpw, window, 0)
    acc = acc_scr[...]
    o_t = acc[:NSA_HEAD_DIM] / acc[NSA_HEAD_DIM:NSA_HEAD_DIM + 1]
    for h in range(NSA_HPG):
        o_ref[:, h * NSA_HEAD_DIM:(h + 1) * NSA_HEAD_DIM] = o_t[:, h * tq:(h + 1) * tq].T


def _sel_attn(zn, k_aug, vt_aug, sel, tq, tk):
    T = zn.shape[1]
    ns = sel.shape[1]
    wb = k_aug.shape[2] - NSA_HEAD_DIM
    assert tk % tq == 0 and (wb * SLC_BLOCK) % (SEL_BUFFERS * tk) == 0 and ns % wb == 0 and T % tk == 0
    kern = functools.partial(_sel_attn_kernel, tq=tq, tk=tk, wb=wb)
    cols = NSA_HPG * tq
    return pl.pallas_call(
        kern,
        grid=(NSA_KV_GROUPS, T // tq),
        in_specs=[
            pl.BlockSpec((NSA_HPG, tq, NSA_HEAD_DIM), lambda g, i: (g, i, 0)),
            pl.BlockSpec((1, T, NSA_HEAD_DIM + wb), lambda g, i: (g, 0, 0)),
            pl.BlockSpec((1, NSA_HEAD_DIM + SEL_ONES_ROWS, T), lambda g, i: (g, 0, 0)),
            pl.BlockSpec((1, ns, tq), lambda g, i: (g, 0, i)),
        ],
        out_specs=pl.BlockSpec((tq, NSA_HPG * NSA_HEAD_DIM), lambda g, i: (i, g)),
        out_shape=jax.ShapeDtypeStruct((T, NSA_WIDTH), F32),
        scratch_shapes=[
            pltpu.VMEM((ns, cols), F32),
            pltpu.VMEM((NSA_HEAD_DIM + wb, cols), BF16),
            pltpu.VMEM((NSA_HEAD_DIM + SEL_ONES_ROWS, cols), F32),
            pltpu.VMEM((1, cols), F32),
        ] + [pltpu.VMEM((tk, cols), F32)] * SEL_BUFFERS + [pltpu.VMEM((tk, cols), BF16)] * SEL_BUFFERS,
        compiler_params=_cparams(("parallel", "arbitrary")),
        name="nsa_sel_attn",
    )(zn, k_aug, vt_aug, sel)


def _win_kernel(*refs, tq, nwb):
    q_ref = refs[0]
    k_refs = refs[1:1 + nwb]
    v_refs = refs[1 + nwb:1 + 2 * nwb]
    gl_ref, oc_ref, os_ref, o_ref = refs[1 + 2 * nwb:]
    g = pl.program_id(0)
    i = pl.program_id(1)
    q0 = i * tq
    qpos = q0 + lax.broadcasted_iota(jnp.int32, (tq, 1), 0)
    kcat = jnp.concatenate([r[0] for r in k_refs], axis=0)
    vcat = jnp.concatenate([r[0] for r in v_refs], axis=0)
    kpos = q0 - (nwb - 1) * tq + lax.broadcasted_iota(jnp.int32, (1, nwb * tq), 1)
    diff = qpos - kpos
    ok = (diff >= 0) & (diff < WINDOW) & (kpos >= 0)
    gates = jax.nn.sigmoid(gl_ref[0])
    lane = lax.broadcasted_iota(jnp.int32, gates.shape, 1)
    for h in range(NSA_HPG):
        s = jnp.where(ok, _dot_nt(q_ref[h], kcat), NEG)
        mx = jnp.max(s, axis=-1, keepdims=True)
        e = jnp.where(ok, jnp.exp(s - mx), 0.0)
        p = e / jnp.maximum(jnp.sum(e, axis=-1, keepdims=True), 1e-30)
        o_w = _dot(p.astype(BF16), vcat)
        col = g * NSA_HPG + h

        def gate(branch):
            return jnp.sum(jnp.where(lane == branch * NSA_HEADS + col, gates, 0.0), axis=-1, keepdims=True)

        sl = slice(h * NSA_HEAD_DIM, (h + 1) * NSA_HEAD_DIM)
        o_ref[:, sl] = gate(0) * oc_ref[:, sl] + gate(1) * os_ref[:, sl] + gate(2) * o_w


def _win_combine(zn, zr, o_c, o_s, tq):
    T = zn.shape[1]
    nwb = WINDOW // tq + 1
    k_tile0 = NSA_HEADS + 4 * NSA_KV_GROUPS
    v_tile0 = k_tile0 + NSA_KV_GROUPS

    def kv_spec(tile0, d):
        return pl.BlockSpec((1, tq, NSA_HEAD_DIM),
                            lambda g, i: (tile0 + g, jnp.maximum(i - (nwb - 1) + d, 0), 0))

    wide = pl.BlockSpec((tq, NSA_HPG * NSA_HEAD_DIM), lambda g, i: (i, g))
    kern = functools.partial(_win_kernel, tq=tq, nwb=nwb)
    return pl.pallas_call(
        kern,
        grid=(NSA_KV_GROUPS, T // tq),
        in_specs=([pl.BlockSpec((NSA_HPG, tq, NSA_HEAD_DIM), lambda g, i: (g, i, 0))]
                  + [kv_spec(k_tile0, d) for d in range(nwb)]
                  + [kv_spec(v_tile0, d) for d in range(nwb)]
                  + [pl.BlockSpec((1, tq, LANES), lambda g, i: (RW_TILES - 1, i, 0)), wide, wide]),
        out_specs=wide,
        out_shape=jax.ShapeDtypeStruct((T, NSA_WIDTH), F32),
        compiler_params=_cparams(("parallel", "parallel")),
        name="nsa_window_combine",
    )(*([zn] * (1 + 2 * nwb)), zr, o_c, o_s)


def _cmp_to_sel_matrix(ncp, ns):
    cmp_start = np.arange(ncp)[:, None] * CMP_STRIDE
    sel_start = np.arange(ns)[None, :] * SLC_BLOCK
    overlap = np.minimum(cmp_start + CMP_BLOCK, sel_start + SLC_BLOCK) - np.maximum(cmp_start, sel_start)
    return jnp.asarray(np.clip(overlap, 0, None).astype(np.float32).T / CMP_BLOCK, dtype=BF16)


def _nsa(zn, zr, cmp_pos_k, cmp_w1_k, cmp_w2_k, cmp_pos_v, cmp_w1_v, cmp_w2_v, tq, tk, sel_window=LANES):
    T = zn.shape[1]
    zn16 = zn.reshape(NSA_TILES, T // CMP_STRIDE, CMP_STRIDE * NSA_HEAD_DIM)
    width = CMP_BLOCK * NSA_HEAD_DIM
    pos = jnp.stack([cmp_pos_k.reshape(1, width), cmp_pos_v.reshape(1, width)])
    pos = jnp.broadcast_to(pos, (2, 8, width)).astype(BF16)
    w1 = jnp.stack([cmp_w1_k, cmp_w1_v]).astype(BF16)
    w2 = jnp.stack([cmp_w2_k, cmp_w2_v]).astype(BF16)
    kv_cmp = _compress(zn16, pos, w1, w2)
    ns = T // SLC_BLOCK
    o_c, sel = _cmp_select(zn, kv_cmp, _cmp_to_sel_matrix(T // CMP_STRIDE, ns), tq)
    ks0 = NSA_HEADS + 2 * NSA_KV_GROUPS
    vs0 = ks0 + NSA_KV_GROUPS
    wb = min(sel_window, ns)
    blk_onehot = (jnp.arange(T)[:, None] // SLC_BLOCK % wb == jnp.arange(wb)[None, :]).astype(BF16)
    k_aug = jnp.concatenate([zn[ks0:vs0], jnp.broadcast_to(blk_onehot, (NSA_KV_GROUPS, T, wb))], axis=2)
    vt_aug = jnp.concatenate([jnp.swapaxes(zn[vs0:vs0 + NSA_KV_GROUPS], 1, 2),
                              jnp.ones((NSA_KV_GROUPS, SEL_ONES_ROWS, T), BF16)], axis=1)
    o_s = _sel_attn(zn, k_aug, vt_aug, sel, tq, tk)
    return _win_combine(zn, zr, o_c, o_s, tq)


def _softplus(y):
    return jnp.maximum(y, 0.0) + jnp.log(1.0 + jnp.exp(-jnp.abs(y)))


def _rwkv_kernel(z_ref, prev_ref, mu_ref, pv_ref, wup_ref, aup_ref, gup_ref, o_ref,
                 rt_s, at_s, kh_s, bh_s, kb_s, bb_s, v_s, gc_s, g_s, bonus_s, y_s, state_s, *, tb):
    C = RWKV_CHUNK
    step = pl.program_id(0)

    @pl.when(step == 0)
    def _():
        state_s[...] = jnp.zeros(state_s.shape, F32)

    has_prev = jnp.where(step > 0, 1.0, 0.0)
    row = lax.broadcasted_iota(jnp.int32, (tb, LANES), 0)

    def shifted_mix(t):
        z = z_ref[t]
        zp = jnp.where(row == 0, prev_ref[t, 7:8, :] * has_prev, pltpu.roll(z, 1, axis=0))
        return z + (zp - z) * mu_ref[t]

    ri = lax.broadcasted_iota(jnp.int32, (tb, tb), 0)
    ci = lax.broadcasted_iota(jnp.int32, (tb, tb), 1)
    tri = jnp.where(((ri >> 6) == (ci >> 6)) & (ci <= ri), 1.0, 0.0).astype(BF16)
    wide = 2 * LANES
    wr = lax.broadcasted_iota(jnp.int32, (wide, wide), 0)
    wc = lax.broadcasted_iota(jnp.int32, (wide, wide), 1)
    head_sum = jnp.where((wr >> 6) == (wc >> 6), 1.0, 0.0).astype(BF16)
    head_mean = jnp.where((wr >> 6) == (wc >> 6), 1.0 / RWKV_HEAD_DIM, 0.0).astype(BF16)

    wa = shifted_mix(3 * RWKV_PAIRS)
    dw = _dot(jnp.tanh(wa).astype(BF16), wup_ref[...])
    da = _dot(wa.astype(BF16), aup_ref[...])
    g0 = jax.nn.sigmoid(shifted_mix(3 * RWKV_PAIRS + 1)).astype(BF16)
    g1 = jax.nn.sigmoid(shifted_mix(3 * RWKV_PAIRS + 2)).astype(BF16)
    g_s[...] = _dot(g0, gup_ref[:LANES]) + _dot(g1, gup_ref[LANES:])

    def mixed(first_tile, q):
        return jnp.concatenate([shifted_mix(first_tile + 2 * q), shifted_mix(first_tile + 2 * q + 1)], axis=1)

    for q in range(RWKV_PAIRS // 2):
        sl = slice(q * wide, (q + 1) * wide)
        r = mixed(0, q)
        k = mixed(RWKV_PAIRS, q)
        v = mixed(2 * RWKV_PAIRS, q)
        w_log = -_softplus(-(pv_ref[0:1, sl] + dw[:, sl])) - 0.5
        lw = -jnp.exp(w_log)
        icl = jax.nn.sigmoid(pv_ref[1:2, sl] + da[:, sl])
        kk = k * pv_ref[2:3, sl]
        kk = kk * lax.rsqrt(jnp.maximum(_dot((kk * kk).astype(BF16), head_sum), 1e-24))
        k2 = k * (1.0 + (icl - 1.0) * pv_ref[3:4, sl])
        bonus_s[:, sl] = _dot((r * k2 * pv_ref[4:5, sl]).astype(BF16), head_sum) * v
        hi, mid, lo = _split3(lw)
        cum = _dot(tri, hi) + _dot(tri, mid) + _dot(tri, lo)
        b = kk * icl
        inv_decay = jnp.exp(-cum)
        to_end = []
        for c in range(tb // C):
            e_end = jnp.exp(cum[c * C + C - 1:c * C + C])
            gc_s[8 * c:8 * c + 1, sl] = e_end
            to_end.append(inv_decay[c * C:(c + 1) * C] * e_end)
        to_end = jnp.concatenate(to_end, axis=0)
        rt_s[:, sl] = (r * jnp.exp(cum)).astype(BF16)
        at_s[:, sl] = (-kk * jnp.exp(cum - lw)).astype(BF16)
        kh_s[:, sl] = (k2 * inv_decay).astype(BF16)
        bh_s[:, sl] = (b * inv_decay).astype(BF16)
        kb_s[:, sl] = (k2 * to_end).astype(BF16)
        bb_s[:, sl] = (b * to_end).astype(BF16)
        v_s[:, sl] = v.astype(BF16)

    hr = lax.broadcasted_iota(jnp.int32, (LANES, LANES), 0)
    hc = lax.broadcasted_iota(jnp.int32, (LANES, LANES), 1)
    same_head = (hr >> 6) == (hc >> 6)
    t_r = hr & (C - 1)
    t_c = hc & (C - 1)
    strict = same_head & (t_c < t_r)
    incl = same_head & (t_c <= t_r)
    eye = jnp.where(hr == hc, 1.0, 0.0)
    lane_head = lax.broadcasted_iota(jnp.int32, (C, LANES), 1) >> 6

    def stack(zc):
        return jnp.concatenate([jnp.where(lane_head == 0, zc, jnp.zeros_like(zc)),
                                jnp.where(lane_head == 1, zc, jnp.zeros_like(zc))], axis=0)

    def chunk(c, carry):
        t0 = pl.multiple_of(c * C, C)
        rows = pl.ds(t0, C)
        pairs = range(RWKV_PAIRS)
        sls = [slice(p * LANES, (p + 1) * LANES) for p in pairs]
        lhs = [jnp.concatenate([stack(at_s[rows, sl]), stack(rt_s[rows, sl])], axis=0) for sl in sls]
        rhs = [jnp.concatenate([stack(kh_s[rows, sl]), stack(bh_s[rows, sl])], axis=0) for sl in sls]
        aa = [_dot_nt(lhs[p], rhs[p]) for p in pairs]
        a_ak = [jnp.where(strict, aa[p][:2 * C, :2 * C], 0.0).astype(BF16) for p in pairs]
        n_pow = [jnp.where(strict, aa[p][:2 * C, 2 * C:], 0.0) for p in pairs]
        a_r = [jnp.concatenate([jnp.where(incl, aa[p][2 * C:, :2 * C], 0.0).astype(BF16),
                                jnp.where(incl, aa[p][2 * C:, 2 * C:], 0.0).astype(BF16)], axis=1) for p in pairs]
        t_inv = [eye + n_pow[p] for p in pairs]
        for _ in range(5):
            nb = [n_pow[p].astype(BF16) for p in pairs]
            n_pow = [_dot(nb[p], nb[p]) for p in pairs]
            t_inv = [t_inv[p] + _dot(t_inv[p].astype(BF16), n_pow[p].astype(BF16)) for p in pairs]
        state = [state_s[p] for p in pairs]
        xs = [_dot_nt(lhs[p], state[p].astype(BF16)) for p in pairs]
        v_c = [v_s[rows, sl] for sl in sls]
        v_st = [stack(v_c[p]) for p in pairs]
        av = [_dot(a_ak[p], v_st[p]) for p in pairs]
        sa = [_dot(t_inv[p].astype(BF16), (xs[p][:2 * C] + av[p]).astype(BF16)) for p in pairs]
        ys = [xs[p][2 * C:] + _dot(a_r[p], jnp.concatenate([v_st[p], sa[p].astype(BF16)], axis=0)) for p in pairs]
        upd = [_dot_tn(jnp.concatenate([v_c[p], (sa[p][:C] + sa[p][C:]).astype(BF16)], axis=0),
                       jnp.concatenate([kb_s[rows, sls[p]], bb_s[rows, sls[p]]], axis=0)) for p in pairs]
        for p in pairs:
            y_s[rows, sls[p]] = ys[p][:C] + ys[p][C:]
            state_s[p] = (state[p] * gc_s[pl.ds(pl.multiple_of(c * 8, 8), 1), sls[p]]
                          + jnp.where(same_head, upd[p], 0.0))
        return carry

    lax.fori_loop(0, tb // C, chunk, 0)

    for q in range(RWKV_PAIRS // 2):
        sl = slice(q * wide, (q + 1) * wide)
        y = y_s[:, sl]
        d = y - _dot(y.astype(BF16), head_mean)
        var = _dot((d * d).astype(BF16), head_mean)
        yn = d * lax.rsqrt(var + LNX_EPS) * pv_ref[5:6, sl] + pv_ref[6:7, sl]
        o_ref[:, sl] = (yn + bonus_s[:, sl]) * g_s[:, sl]


def _hi_lo(w):
    hi = w.astype(BF16)
    return hi, (w - hi.astype(F32)).astype(BF16)


def _rwkv(zr, mu_t, w0, w_up, a0, a_up, g_up, k_k, k_a, r_k, lnx_g, lnx_b, tb):
    T = zr.shape[1]
    n_in = RW_TILES - 1
    pv = jnp.stack([w0, a0, k_k, k_a, r_k.reshape(-1), lnx_g, lnx_b, jnp.zeros_like(w0)])
    wup = jnp.pad(w_up, ((0, LANES - W_RANK), (0, 0))).astype(BF16)
    aup = jnp.pad(a_up, ((W_RANK, LANES - W_RANK - A_RANK), (0, 0))).astype(BF16)
    gup = jnp.pad(g_up, ((0, 2 * LANES - G_RANK), (0, 0))).astype(BF16)
    full = lambda shape: pl.BlockSpec(shape, lambda s: (0,) * len(shape))
    kern = functools.partial(_rwkv_kernel, tb=tb)
    bf = lambda: pltpu.VMEM((tb, RWKV_WIDTH), BF16)
    ff = lambda: pltpu.VMEM((tb, RWKV_WIDTH), F32)
    return pl.pallas_call(
        kern,
        grid=(T // tb,),
        in_specs=[
            pl.BlockSpec((n_in, tb, LANES), lambda s: (0, s, 0)),
            pl.BlockSpec((n_in, 8, LANES), lambda s: (0, jnp.maximum(s * (tb // 8) - 1, 0), 0)),
            full((n_in, 1, LANES)),
            full((8, RWKV_WIDTH)),
            full((LANES, RWKV_WIDTH)), full((LANES, RWKV_WIDTH)), full((2 * LANES, RWKV_WIDTH)),
        ],
        out_specs=pl.BlockSpec((tb, RWKV_WIDTH), lambda s: (s, 0)),
        out_shape=jax.ShapeDtypeStruct((T, RWKV_WIDTH), F32),
        scratch_shapes=[bf(), bf(), bf(), bf(), bf(), bf(), bf(),
                        pltpu.VMEM((8 * (tb // RWKV_CHUNK), RWKV_WIDTH), F32), ff(), ff(), ff(),
                        pltpu.VMEM((RWKV_PAIRS, LANES, LANES), F32)],
        compiler_params=_cparams(("arbitrary",)),
        name="rwkv7",
    )(zr, zr, mu_t, pv, wup, aup, gup)


PROJ_COLS = 512


def _mix_out_kernel(ya_ref, yb_ref, g_ref, w_ref, x_ref, o_ref):
    h = jnp.concatenate([_rms(ya_ref[...], g_ref[...]).astype(BF16), yb_ref[...].astype(BF16)], axis=1)
    for n in range(D_MODEL // PROJ_COLS):
        sl = slice(n * PROJ_COLS, (n + 1) * PROJ_COLS)
        o_ref[:, sl] = x_ref[:, sl] + _dot(h, w_ref[:, sl])


def _mix_out(y_a, y_b, g, w, x, tm):
    T = x.shape[0]
    return pl.pallas_call(
        _mix_out_kernel,
        grid=(T // tm,),
        in_specs=[
            pl.BlockSpec((tm, NSA_WIDTH), lambda m: (m, 0)),
            pl.BlockSpec((tm, RWKV_WIDTH), lambda m: (m, 0)),
            _resident((1, NSA_WIDTH)),
            _resident((NSA_WIDTH + RWKV_WIDTH, D_MODEL)),
            pl.BlockSpec((tm, D_MODEL), lambda m: (m, 0)),
        ],
        out_specs=pl.BlockSpec((tm, D_MODEL), lambda m: (m, 0)),
        out_shape=jax.ShapeDtypeStruct((T, D_MODEL), F32),
        compiler_params=_cparams(("parallel",)),
        name="mix_out_proj",
    )(y_a, y_b, g, w, x)


def _norm_mm_kernel(x_ref, g_ref, w_ref, o_ref, h_scr):
    @pl.when(pl.program_id(1) == 0)
    def _():
        h_scr[...] = _rms(x_ref[...], g_ref[...]).astype(BF16)

    o_ref[...] = _dot(h_scr[...], w_ref[...]).astype(o_ref.dtype)


def _norm_mm(x, g, w, tm, tn, name):
    M, K = x.shape
    N = w.shape[1]
    return pl.pallas_call(
        _norm_mm_kernel,
        grid=(M // tm, N // tn),
        in_specs=[
            pl.BlockSpec((tm, K), lambda m, n: (m, 0)),
            pl.BlockSpec((1, K), lambda m, n: (0, 0)),
            pl.BlockSpec((K, tn), lambda m, n: (0, n)),
        ],
        out_specs=pl.BlockSpec((tm, tn), lambda m, n: (m, n)),
        out_shape=jax.ShapeDtypeStruct((M, N), BF16),
        scratch_shapes=[pltpu.VMEM((tm, K), BF16)],
        compiler_params=_cparams(("parallel", "arbitrary")),
        name=name,
    )(x, g, w)


def _mem_router_kernel(x_ref, km_ref, vm_ref, wq_ref, wo_ref, g2_ref, g3_ref, w_hi, w_lo, b_ref,
                       x2_ref, h_ref, c_ref, n_ref):
    scale = MEM_HEAD_DIM ** -0.5
    hq = _rms(x_ref[...], g2_ref[...]).astype(BF16)
    heads = []
    for hd in range(MEM_HEADS):
        sl = slice(hd * MEM_HEAD_DIM, (hd + 1) * MEM_HEAD_DIM)
        q = _dot(hq, wq_ref[:, sl]).astype(BF16)
        s = _dot_nt(q, km_ref[:, sl]) * scale
        e = jnp.exp(s - jnp.max(s, axis=-1, keepdims=True))
        p = e / jnp.sum(e, axis=-1, keepdims=True)
        heads.append(_dot(p.astype(BF16), vm_ref[:, sl]).astype(BF16))
    o = jnp.concatenate(heads, axis=1)
    for n in range(D_MODEL // PROJ_COLS):
        sl = slice(n * PROJ_COLS, (n + 1) * PROJ_COLS)
        x2_ref[:, sl] = x_ref[:, sl] + _dot(o, wo_ref[:, sl])
    _route(x2_ref[...], g3_ref, w_hi, w_lo, b_ref, h_ref, c_ref, n_ref)


def _mem_router(x, k_mem, v_mem, wq, wo, g2, g3, w_hi, w_lo, b, tm):
    T = x.shape[0]
    M = k_mem.shape[0]
    row_block = lambda width: pl.BlockSpec((tm, width), lambda m: (m, 0))
    return pl.pallas_call(
        _mem_router_kernel,
        grid=(T // tm,),
        in_specs=[
            row_block(D_MODEL),
            _resident((M, D_MODEL)), _resident((M, D_MODEL)),
            _resident((D_MODEL, D_MODEL)), _resident((D_MODEL, D_MODEL)),
            _resident((1, D_MODEL)), _resident((1, D_MODEL)),
            _resident((D_MODEL, LANES)), _resident((D_MODEL, LANES)), _resident((1, LANES)),
        ],
        out_specs=[row_block(D_MODEL), row_block(D_MODEL), row_block(LANES),
                   pl.BlockSpec((1, 8, LANES), lambda m: (m, 0, 0))],
        out_shape=[
            jax.ShapeDtypeStruct((T, D_MODEL), F32),
            jax.ShapeDtypeStruct((T, D_MODEL), BF16),
            jax.ShapeDtypeStruct((T, LANES), F32),
            jax.ShapeDtypeStruct((T // tm, 8, LANES), F32),
        ],
        compiler_params=_cparams(("parallel",)),
        name="mem_attn_router",
    )(x, k_mem, v_mem, wq, wo, g2, g3, w_hi, w_lo, b)


def _route(x, g_ref, w_hi, w_lo, b_ref, h_ref, c_ref, n_ref):
    h = _rms(x, g_ref[...])
    h_ref[...] = h.astype(BF16)
    logits = _dot_xw3(h, w_hi[...], w_lo[...]) + b_ref[...]
    lane = lax.broadcasted_iota(jnp.int32, logits.shape, 1)
    big = jnp.int32(LANES)
    is_grp = (lane >= N_EXPERTS) & (lane < N_EXPERTS + N_GROUPS)
    lg = jnp.where(is_grp, logits, NEG)
    eg = jnp.where(is_grp, jnp.exp(lg - jnp.max(lg, axis=-1, keepdims=True)), 0.0)
    pg = eg / jnp.sum(eg, axis=-1, keepdims=True)
    pg_top = jnp.max(pg, axis=-1, keepdims=True)
    g_idx = jnp.min(jnp.where(is_grp & (pg == pg_top), lane - N_EXPERTS, big), axis=-1, keepdims=True)
    in_grp = (lane < N_EXPERTS) & ((lane >> 3) == g_idx)
    le = jnp.where(in_grp, logits, NEG)
    ee = jnp.where(in_grp, jnp.exp(le - jnp.max(le, axis=-1, keepdims=True)), 0.0)
    pe = jnp.where(in_grp, ee / jnp.sum(ee, axis=-1, keepdims=True), -1.0)
    p1 = jnp.max(pe, axis=-1, keepdims=True)
    hit1 = lane == jnp.min(jnp.where(pe == p1, lane, big), axis=-1, keepdims=True)
    pe2 = jnp.where(hit1, -1.0, pe)
    p2 = jnp.max(pe2, axis=-1, keepdims=True)
    hit2 = lane == jnp.min(jnp.where(pe2 == p2, lane, big), axis=-1, keepdims=True)
    denom = p1 + p2
    comb = jnp.where(hit1, pg_top * p1 / denom, 0.0) + jnp.where(hit2, pg_top * p2 / denom, 0.0)
    c_ref[...] = jnp.where(lane == GROUP_LANE, g_idx.astype(F32), comb)
    in_group = jnp.where(lane == g_idx, 1.0, 0.0)
    n_ref[0] = jnp.broadcast_to(jnp.sum(in_group, axis=0, keepdims=True), n_ref.shape[1:])


def _moe_kernel(meta_ref, h_ref, c_ref, wg_ref, wu_ref, wd_ref, y_ref, hs, cs, ys, slot_scr, perm, *, tm, sub):
    m = pl.program_id(0)
    e = pl.program_id(1)
    nslot = hs.shape[0]

    @pl.when(e == 0)
    def _():
        c = c_ref[...]
        lane = lax.broadcasted_iota(jnp.int32, c.shape, 1)
        one_hot = jnp.where(lane.astype(F32) == c[:, GROUP_LANE:GROUP_LANE + 1], 1.0, 0.0)
        earlier = jnp.where(lax.broadcasted_iota(jnp.int32, (tm, tm), 1)
                            < lax.broadcasted_iota(jnp.int32, (tm, tm), 0), 1.0, 0.0).astype(BF16)
        rank = _dot(earlier, one_hot.astype(BF16))
        start = jnp.zeros((1, LANES), F32)
        for g in range(N_GROUPS):
            start = jnp.where(lane[0:1] == g, (meta_ref[m * 2 * N_GROUPS + g] * sub).astype(F32), start)
        slot = jnp.sum(one_hot * (rank + start), axis=-1, keepdims=True)
        slot_scr[...] = jnp.broadcast_to(slot, slot_scr.shape)
        slot_row = slot_scr[...].T[0:1, :]
        perm[...] = jnp.where(lax.broadcasted_iota(jnp.int32, (nslot, tm), 0).astype(F32) == slot_row,
                              1.0, 0.0).astype(BF16)
        hs[...] = _dot(perm[...], h_ref[...]).astype(BF16)
        w_hi, w_lo = _split2(jnp.where(lane < N_EXPERTS, c, 0.0))
        cs[...] = _dot(perm[...], w_hi) + _dot(perm[...], w_lo)
        ys[...] = jnp.zeros(ys.shape, F32)

    grp = e // EXPERTS_PER_GROUP
    first = meta_ref[m * 2 * N_GROUPS + grp]
    count = meta_ref[m * 2 * N_GROUPS + N_GROUPS + grp]
    def ffn(start_block, n_blocks):
        n = n_blocks * sub
        rows = pl.ds(pl.multiple_of(start_block * sub, sub), n)
        x = hs[rows, :]
        hid = jax.nn.silu(_dot(x, wg_ref[0])) * _dot(x, wu_ref[0])
        lane_s = lax.broadcasted_iota(jnp.int32, (n, LANES), 1)
        c_e = jnp.sum(jnp.where(lane_s == e, cs[rows, :], 0.0), axis=-1, keepdims=True)
        ys[rows, :] += _dot((hid * c_e).astype(BF16), wd_ref[0])

    def pair(s, carry):
        ffn(first + 2 * s, 2)
        return carry

    lax.fori_loop(0, count // 2, pair, 0)

    @pl.when(count % 2 == 1)
    def _():
        ffn(first + count - 1, 1)

    @pl.when(e == pl.num_programs(1) - 1)
    def _():
        back = jnp.where(lax.broadcasted_iota(jnp.int32, (tm, nslot), 1).astype(F32) == slot_scr[:, 0:1],
                         1.0, 0.0).astype(BF16)
        for n in range(D_MODEL // PROJ_COLS):
            sl = slice(n * PROJ_COLS, (n + 1) * PROJ_COLS)
            y_ref[:, sl] = _dot(back, ys[:, sl].astype(BF16)).astype(y_ref.dtype)


def _moe(h, comb, counts, wg, wu, wd, tm, sub):
    T = h.shape[0]
    nt = T // tm
    nslot = tm + N_GROUPS * sub
    cnt = counts[:, 0, :N_GROUPS].astype(jnp.int32).reshape(nt, -1, N_GROUPS).sum(axis=1)
    nblk = (cnt + sub - 1) // sub
    first = jnp.cumsum(nblk, axis=1) - nblk
    meta = jnp.concatenate([first, nblk], axis=1).reshape(-1)
    once = lambda shape: pl.BlockSpec(shape, lambda m, e, meta: (m, 0), pipeline_mode=pl.Buffered(1))
    kern = functools.partial(_moe_kernel, tm=tm, sub=sub)
    return pl.pallas_call(
        kern,
        grid_spec=pltpu.PrefetchScalarGridSpec(
            num_scalar_prefetch=1,
            grid=(nt, N_EXPERTS),
            in_specs=[
                once((tm, D_MODEL)),
                once((tm, LANES)),
                pl.BlockSpec((1, D_MODEL, EXPERT_FF), lambda m, e, meta: (e, 0, 0)),
                pl.BlockSpec((1, D_MODEL, EXPERT_FF), lambda m, e, meta: (e, 0, 0)),
                pl.BlockSpec((1, EXPERT_FF, D_MODEL), lambda m, e, meta: (e, 0, 0)),
            ],
            out_specs=pl.BlockSpec((tm, D_MODEL), lambda m, e, meta: (m, 0)),
            scratch_shapes=[
                pltpu.VMEM((nslot, D_MODEL), BF16),
                pltpu.VMEM((nslot, LANES), F32),
                pltpu.VMEM((nslot, D_MODEL), F32),
                pltpu.VMEM((tm, LANES), F32),
                pltpu.VMEM((nslot, tm), BF16),
            ],
        ),
        out_shape=jax.ShapeDtypeStruct((T, D_MODEL), BF16),
        compiler_params=_cparams(("parallel", "arbitrary")),
        name="moe_experts",
    )(meta, h, comb, wg, wu, wd)


def _final_norm_kernel(x_ref, y_ref, g_ref, o_ref):
    o_ref[...] = _rms(x_ref[...] + y_ref[...].astype(F32), g_ref[...])


def _final_norm(x, y, g, tm):
    T = x.shape[0]
    blk = pl.BlockSpec((tm, D_MODEL), lambda m: (m, 0))
    return pl.pallas_call(
        _final_norm_kernel,
        grid=(T // tm,),
        in_specs=[blk, blk, _resident((1, D_MODEL))],
        out_specs=blk,
        out_shape=jax.ShapeDtypeStruct((T, D_MODEL), F32),
        compiler_params=_cparams(("parallel",)),
        name="moe_residual_norm",
    )(x, y, g)


def _pack_w_in(w_in, rwkv_mu):
    nsa_cols = NSA_WIDTH + 6 * NSA_KV_WIDTH
    gl = w_in[:, nsa_cols:nsa_cols + 3 * NSA_HEADS]
    rw0 = nsa_cols + 3 * NSA_HEADS
    rkv = w_in[:, rw0:rw0 + 3 * RWKV_WIDTH + W_RANK + A_RANK]
    gd = w_in[:, rw0 + 3 * RWKV_WIDTH + W_RANK + A_RANK:]

    def pad(a, n):
        return jnp.pad(a, ((0, 0), (0, n - a.shape[1])))

    w = jnp.concatenate([w_in[:, :nsa_cols], rkv, pad(gd, 2 * LANES), pad(gl, LANES)], axis=1)
    mu = rwkv_mu.reshape(1, -1)
    mu_rkv = mu[:, :3 * RWKV_WIDTH + W_RANK + A_RANK]
    mu_gd = pad(mu[:, 3 * RWKV_WIDTH + W_RANK + A_RANK:], 2 * LANES)
    mu_t = jnp.concatenate([mu_rkv, mu_gd], axis=1).reshape(RW_TILES - 1, 1, LANES)
    return w.astype(BF16), mu_t


def kernel(x, mem, ln1_g, w_in, cmp_pos_k, cmp_w1_k, cmp_w2_k, cmp_pos_v, cmp_w1_v, cmp_w2_v, nsa_norm_g, rwkv_mu, rwkv_w0, rwkv_w_up, rwkv_a0, rwkv_a_up, rwkv_g_up, rwkv_k_k, rwkv_k_a, rwkv_r_k, rwkv_lnx_g, rwkv_lnx_b, w_out, ln_mem_g, ln2_g, wq_mem, wk_mem, wv_mem, wo_mem, ln3_g, router_group_w, router_group_b, router_expert_w, router_expert_b, moe_w_gate, moe_w_up, moe_w_down, lnf_g):
    B, T, _ = x.shape
    assert B == 1 and T % max(TM_PROJ, TK_SEL, TB_RWKV, TM_MOE) == 0
    assert w_in.shape[0] == 1
    row = lambda a: a.reshape(1, -1)
    xs = x[0]
    for l in range(w_in.shape[0]):
        wp, mu_t = _pack_w_in(w_in[l], rwkv_mu[l])
        zn, zr = _in_proj(xs, row(ln1_g[l]), wp, TM_PROJ)
        y_a = _nsa(zn, zr, cmp_pos_k[l], cmp_w1_k[l], cmp_w2_k[l], cmp_pos_v[l], cmp_w1_v[l], cmp_w2_v[l],
                   TQ_NSA, TK_SEL)
        y_b = _rwkv(zr, mu_t, rwkv_w0[l], rwkv_w_up[l], rwkv_a0[l], rwkv_a_up[l], rwkv_g_up[l], rwkv_k_k[l],
                    rwkv_k_a[l], rwkv_r_k[l], rwkv_lnx_g[l], rwkv_lnx_b[l], TB_RWKV)
        xs = _mix_out(y_a, y_b, row(nsa_norm_g[l]), w_out[l].astype(BF16), xs, TM_PROJ)
        m_tok = mem[0]
        k_mem = _norm_mm(m_tok, row(ln_mem_g[l]), wk_mem[l].astype(BF16), m_tok.shape[0], PROJ_COLS, "mem_k")
        v_mem = _norm_mm(m_tok, row(ln_mem_g[l]), wv_mem[l].astype(BF16), m_tok.shape[0], PROJ_COLS, "mem_v")
        w_r = jnp.pad(jnp.concatenate([router_expert_w[l], router_group_w[l]], axis=1),
                      ((0, 0), (0, LANES - N_EXPERTS - N_GROUPS)))
        b_r = jnp.pad(jnp.concatenate([router_expert_b[l], router_group_b[l]]), (0, LANES - N_EXPERTS - N_GROUPS))
        xs, h3, comb, counts = _mem_router(xs, k_mem, v_mem, wq_mem[l].astype(BF16), wo_mem[l].astype(BF16),
                                   row(ln2_g[l]), row(ln3_g[l]), *_hi_lo(w_r), row(b_r), TM_PROJ)
        y_moe = _moe(h3, comb, counts, moe_w_gate[l].astype(BF16), moe_w_up[l].astype(BF16),
                     moe_w_down[l].astype(BF16), TM_MOE, SUB_MOE)
        xs = _final_norm(xs, y_moe, row(lnf_g), TM_PROJ)
    return xs[None]
```

```python
import functools

import numpy as np
import jax
import jax.numpy as jnp
from jax import lax
from jax.experimental import pallas as pl
from jax.experimental.pallas import tpu as pltpu

F32 = jnp.float32
BF16 = jnp.bfloat16

LANES = 128
D_MODEL = 2048
EPS = 1e-6
NSA_HEAD_DIM = 128
NSA_HEADS = 8
NSA_KV_GROUPS = 2
NSA_HPG = NSA_HEADS // NSA_KV_GROUPS
NSA_WIDTH = NSA_HEADS * NSA_HEAD_DIM
NSA_KV_WIDTH = NSA_KV_GROUPS * NSA_HEAD_DIM
CMP_BLOCK = 32
CMP_STRIDE = 16
SLC_BLOCK = 64
SLC_TOPK = 16
WINDOW = 512
NEG = -1e30
RWKV_HEAD_DIM = 64
RWKV_HEADS = 16
RWKV_WIDTH = RWKV_HEADS * RWKV_HEAD_DIM
RWKV_PAIRS = RWKV_WIDTH // LANES
W_RANK = 64
A_RANK = 64
G_RANK = 160
LNX_EPS = 64e-5
RWKV_CHUNK = 64
MEM_HEADS = 4
MEM_HEAD_DIM = D_MODEL // MEM_HEADS
N_GROUPS = 4
EXPERTS_PER_GROUP = 8
N_EXPERTS = N_GROUPS * EXPERTS_PER_GROUP
EXPERT_FF = 256

NSA_TILES = 20
RW_TILES = 28
IN_TILE_BLOCK = 4

VMEM_LIMIT = 56 * 1024 * 1024

TM_PROJ = 256
TQ_NSA = 128
TK_SEL = 512
TB_RWKV = 256
TM_MOE = 1024
SUB_MOE = 128
GROUP_LANE = 64


def _cparams(sem):
    return pltpu.CompilerParams(dimension_semantics=sem, vmem_limit_bytes=VMEM_LIMIT)


def _dot(a, b):
    return jnp.dot(a, b, preferred_element_type=F32)


def _dot_nt(a, b):
    return lax.dot_general(a, b, (((1,), (1,)), ((), ())), preferred_element_type=F32)


def _dot_tn(a, b):
    return lax.dot_general(a, b, (((0,), (0,)), ((), ())), preferred_element_type=F32)


def _split2(x):
    hi = x.astype(BF16)
    lo = (x - hi.astype(F32)).astype(BF16)
    return hi, lo


def _split3(x):
    hi = x.astype(BF16)
    r1 = x - hi.astype(F32)
    mid = r1.astype(BF16)
    lo = (r1 - mid.astype(F32)).astype(BF16)
    return hi, mid, lo


def _dot_x2(x, w):
    hi, lo = _split2(x)
    return _dot(hi, w) + _dot(lo, w)


def _dot_x3(x, w):
    hi, mid, lo = _split3(x)
    return _dot(hi, w) + _dot(mid, w) + _dot(lo, w)


def _dot_xw3(x, w_hi, w_lo):
    hi, lo = _split2(x)
    return _dot(hi, w_hi) + _dot(lo, w_hi) + _dot(hi, w_lo)


def _rms(x, g):
    ms = jnp.mean(x * x, axis=-1, keepdims=True)
    return x * lax.rsqrt(ms + EPS) * g


def _resident(shape):
    return pl.BlockSpec(shape, lambda *_: (0,) * len(shape), pipeline_mode=pl.Buffered(1))


CMP_TILES = 2 * NSA_KV_GROUPS


def _in_proj_kernel(x_ref, g_ref, w_ref, zn_ref, zr_ref, zc_ref, *, q_scale):
    h = _rms(x_ref[...], g_ref[...]).astype(BF16)
    nb = IN_TILE_BLOCK * LANES
    for n in range((NSA_TILES + RW_TILES) // IN_TILE_BLOCK):
        z = _dot(h, w_ref[:, n * nb:(n + 1) * nb])
        for t in range(IN_TILE_BLOCK):
            tile = n * IN_TILE_BLOCK + t
            zt = z[:, t * LANES:(t + 1) * LANES]
            if tile < NSA_HEADS:
                zn_ref[tile] = (zt * q_scale).astype(BF16)
            elif tile < NSA_TILES:
                zn_ref[tile] = zt.astype(BF16)
                if tile < NSA_HEADS + CMP_TILES:
                    zc_ref[tile - NSA_HEADS] = zt
            else:
                zr_ref[tile - NSA_TILES] = zt


def _in_proj(x, g, w, tm):
    T = x.shape[0]
    kern = functools.partial(_in_proj_kernel, q_scale=NSA_HEAD_DIM ** -0.5)
    return pl.pallas_call(
        kern,
        grid=(T // tm,),
        in_specs=[
            pl.BlockSpec((tm, D_MODEL), lambda m: (m, 0)),
            _resident((1, D_MODEL)),
            _resident((D_MODEL, (NSA_TILES + RW_TILES) * LANES)),
        ],
        out_specs=[
            pl.BlockSpec((NSA_TILES, tm, LANES), lambda m: (0, m, 0)),
            pl.BlockSpec((RW_TILES, tm, LANES), lambda m: (0, m, 0)),
            pl.BlockSpec((CMP_TILES, tm, LANES), lambda m: (0, m, 0)),
        ],
        out_shape=[
            jax.ShapeDtypeStruct((NSA_TILES, T, LANES), BF16),
            jax.ShapeDtypeStruct((RW_TILES, T, LANES), F32),
            jax.ShapeDtypeStruct((CMP_TILES, T, LANES), F32),
        ],
        compiler_params=_cparams(("parallel",)),
        name="in_proj",
    )(x, g, w)


def _compress_kernel(x_ref, pos_ref, w1_ref, w2_ref, o_ref):
    n = x_ref.shape[1] // CMP_STRIDE
    w1 = w1_ref[0]
    h_first = jnp.zeros((n, NSA_HEAD_DIM), F32)
    h_second = jnp.zeros((n, NSA_HEAD_DIM), F32)
    for t in range(CMP_STRIDE):
        x_t = x_ref[0, pl.ds(t, n, stride=CMP_STRIDE), :].astype(BF16)
        h_first += _dot(x_t, w1[t * NSA_HEAD_DIM:(t + 1) * NSA_HEAD_DIM])
        h_second += _dot(x_t, w1[(CMP_STRIDE + t) * NSA_HEAD_DIM:(CMP_STRIDE + t + 1) * NSA_HEAD_DIM])
    h_next = pltpu.roll(h_second, n - 1, axis=0)
    bias = _dot(pos_ref[0], w1)[0:1]
    hid = jax.nn.gelu(h_first + h_next + bias)
    o_ref[0] = _dot(hid.astype(BF16), w2_ref[0]).astype(o_ref.dtype)


def _compress(zc, pos, w1, w2):
    T = zc.shape[1]
    n = T // CMP_STRIDE
    width = CMP_BLOCK * NSA_HEAD_DIM
    return pl.pallas_call(
        _compress_kernel,
        grid=(4,),
        in_specs=[
            pl.BlockSpec((1, T, NSA_HEAD_DIM), lambda s: (s, 0, 0)),
            pl.BlockSpec((1, 8, width), lambda s: (s // 2, 0, 0)),
            pl.BlockSpec((1, width, NSA_HEAD_DIM), lambda s: (s // 2, 0, 0)),
            pl.BlockSpec((1, NSA_HEAD_DIM, NSA_HEAD_DIM), lambda s: (s // 2, 0, 0)),
        ],
        out_specs=pl.BlockSpec((1, n, NSA_HEAD_DIM), lambda s: (s, 0, 0)),
        out_shape=jax.ShapeDtypeStruct((4, n, NSA_HEAD_DIM), BF16),
        compiler_params=_cparams(("parallel",)),
        name="nsa_compress",
    )(zc, pos, w1, w2)


def _cmp_win_kernel(q_ref, kc_ref, lhs_ref, kw_ref, vwt_ref, oc_ref, ow_ref, sel_ref, s_scr, acc_scr,
                    *, tq, top_n, ch):
    q0 = pl.program_id(1) * tq
    cols = NSA_HPG * tq
    ncp = kc_ref.shape[1]
    ns = lhs_ref.shape[1] - NSA_HEAD_DIM - SEL_ONES_ROWS
    q_t = jnp.concatenate([q_ref[h].astype(F32).T.astype(BF16) for h in range(NSA_HPG)], axis=1)
    qpos = q0 + (lax.broadcasted_iota(jnp.int32, (1, cols), 1) & (tq - 1))

    def store_heads(o_ref, o_t):
        for h in range(NSA_HPG):
            o_ref[:, h * NSA_HEAD_DIM:(h + 1) * NSA_HEAD_DIM] = o_t[:, h * tq:(h + 1) * tq].T

    span = WINDOW + tq
    start = pl.multiple_of(jnp.maximum(q0 - WINDOW, 0), tq)
    kpos = start + lax.broadcasted_iota(jnp.int32, (span, 1), 0)
    in_window = (qpos - kpos).astype(jnp.uint32) < jnp.uint32(WINDOW)
    s_w = jnp.where(in_window, _dot(kw_ref[0, pl.ds(start, span), :], q_t), NEG)
    e_w = jnp.exp(s_w - jnp.max(s_w, axis=0, keepdims=True))
    p_w = e_w / jnp.sum(e_w, axis=0, keepdims=True)
    store_heads(ow_ref, _dot(vwt_ref[0, :, pl.ds(start, span)], p_w.astype(BF16)))

    visible = (q0 + tq - 1 - (CMP_BLOCK - 1)) // CMP_STRIDE + 1
    clean = jnp.maximum((q0 - (CMP_BLOCK - 1)) // CMP_STRIDE + 1, 0) // ch
    n_chunks = (visible + ch - 1) // ch

    def chunk_rows(c):
        return pl.ds(pl.multiple_of(c * ch, ch), ch)

    def score_chunk(c, m, masked):
        s = _dot(kc_ref[0, chunk_rows(c), :], q_t)
        if masked:
            cmp_end = (c * ch + lax.broadcasted_iota(jnp.int32, (ch, 1), 0)) * CMP_STRIDE + (CMP_BLOCK - 1)
            s = jnp.where(cmp_end <= qpos, s, NEG)
        s_scr[chunk_rows(c), :] = s
        return jnp.maximum(m, jnp.max(s, axis=0, keepdims=True))

    m = jnp.full((1, cols), NEG, F32)
    m = lax.fori_loop(0, clean, lambda c, m: score_chunk(c, m, False), m)
    m = lax.fori_loop(clean, n_chunks, lambda c, m: score_chunk(c, m, True), m)
    acc_scr[...] = jnp.zeros(acc_scr.shape, F32)

    def weigh_chunk(c, carry):
        e = jnp.exp(s_scr[chunk_rows(c), :] - m).astype(BF16)
        acc_scr[...] += _dot(lhs_ref[0, :, chunk_rows(c)], e)
        return carry

    lax.fori_loop(0, n_chunks, weigh_chunk, 0)
    acc = acc_scr[...]
    sees_any = jnp.where(qpos >= CMP_BLOCK - 1, 1.0, 0.0)
    r_inv = sees_any / jnp.maximum(acc[NSA_HEAD_DIM + ns:NSA_HEAD_DIM + ns + 1], 1e-30)
    store_heads(oc_ref, acc[:NSA_HEAD_DIM] * r_inv)
    weighted = acc[NSA_HEAD_DIM:NSA_HEAD_DIM + ns] * r_inv
    p_sel = weighted[:, 0:tq]
    for h in range(1, NSA_HPG):
        p_sel = p_sel + weighted[:, h * tq:(h + 1) * tq]
    blk = (q0 + lax.broadcasted_iota(jnp.int32, (1, tq), 1)) >> 6
    j = lax.broadcasted_iota(jnp.int32, (ns, tq), 0)
    future = j > blk
    forced = (j == 0) | (j == blk) | (j == blk - 1)
    score = jnp.where(future, -1.0, jnp.where(forced, 1e6, p_sel))
    sel = jnp.zeros((ns, tq), F32)
    for _ in range(top_n):
        best = jnp.max(score, axis=0, keepdims=True)
        idx = jnp.min(jnp.where(score == best, j, ns), axis=0, keepdims=True)
        hit = j == idx
        sel = jnp.where(hit, jnp.where(best >= 0.0, 1.0, 0.0), sel)
        score = jnp.where(hit, -2.0, score)
    sel_ref[0] = sel.astype(sel_ref.dtype)


CMP_CHUNK = 256


def _cmp_win(zn, kv_cmp, cmp_to_sel, vw_t, tq):
    T = zn.shape[1]
    ncp = kv_cmp.shape[1]
    ns = cmp_to_sel.shape[0]
    ch = min(CMP_CHUNK, ncp)
    assert ncp % ch == 0
    kw_tile0 = NSA_HEADS + 4 * NSA_KV_GROUPS
    lhs = jnp.concatenate([jnp.swapaxes(kv_cmp[NSA_KV_GROUPS:], 1, 2),
                           jnp.broadcast_to(cmp_to_sel, (NSA_KV_GROUPS, ns, ncp)),
                           jnp.ones((NSA_KV_GROUPS, SEL_ONES_ROWS, ncp), BF16)], axis=1)
    rows = lhs.shape[1]
    cols = NSA_HPG * tq
    kern = functools.partial(_cmp_win_kernel, tq=tq, top_n=min(SLC_TOPK, ns), ch=ch)
    wide = pl.BlockSpec((tq, NSA_HPG * NSA_HEAD_DIM), lambda g, i: (i, g))
    return pl.pallas_call(
        kern,
        grid=(NSA_KV_GROUPS, T // tq),
        in_specs=[
            pl.BlockSpec((NSA_HPG, tq, NSA_HEAD_DIM), lambda g, i: (g, i, 0)),
            pl.BlockSpec((1, ncp, NSA_HEAD_DIM), lambda g, i: (g, 0, 0)),
            pl.BlockSpec((1, rows, ncp), lambda g, i: (g, 0, 0)),
            pl.BlockSpec((1, T, NSA_HEAD_DIM), lambda g, i: (kw_tile0 + g, 0, 0)),
            pl.BlockSpec((1, NSA_HEAD_DIM, T), lambda g, i: (g, 0, 0)),
        ],
        out_specs=[wide, wide, pl.BlockSpec((1, ns, tq), lambda g, i: (g, 0, i))],
        out_shape=[
            jax.ShapeDtypeStruct((T, NSA_WIDTH), F32),
            jax.ShapeDtypeStruct((T, NSA_WIDTH), F32),
            jax.ShapeDtypeStruct((NSA_KV_GROUPS, ns, T), BF16),
        ],
        scratch_shapes=[pltpu.VMEM((ncp, cols), F32), pltpu.VMEM((rows, cols), F32)],
        compiler_params=_cparams(("parallel", "arbitrary")),
        name="nsa_cmp_window",
    )(zn, kv_cmp, lhs, zn, vw_t)


SEL_ONES_ROWS = 16


SEL_BUFFERS = 2


def _sel_attn_kernel(q_ref, k_ref, vt_ref, sel_ref, gl_ref, oc_ref, ow_ref, o_ref,
                     bias_scr, qa_scr, acc_scr, m_scr, *bufs, tq, tk, wb):
    s_bufs, p_bufs = bufs[:SEL_BUFFERS], bufs[SEL_BUFFERS:]
    q0 = pl.program_id(1) * tq
    cols = NSA_HPG * tq
    tpw = wb * SLC_BLOCK // tk
    last_tile = k_ref.shape[1] // tk - 1
    ns = sel_ref.shape[1]
    before = lax.broadcasted_iota(jnp.int32, (ns, tq), 0) < q0 // SLC_BLOCK
    not_chosen = jnp.where(before & (sel_ref[0].astype(F32) > 0.5), 0.0, NEG)
    for h in range(NSA_HPG):
        bias_scr[:, h * tq:(h + 1) * tq] = not_chosen
        qa_scr[:NSA_HEAD_DIM, h * tq:(h + 1) * tq] = q_ref[h].astype(F32).T.astype(BF16)

    def set_window(w):
        rows = pl.ds(pl.multiple_of(w * wb, wb), wb)
        qa_scr[NSA_HEAD_DIM:, :] = bias_scr[rows, :].astype(BF16)

    def scores(j):
        j = jnp.minimum(j, last_tile)
        return _dot(k_ref[0, pl.ds(pl.multiple_of(j * tk, tk), tk), :], qa_scr[...])

    def pv(j, p_ref):
        return _dot(vt_ref[0, :, pl.ds(pl.multiple_of(j * tk, tk), tk)], p_ref[...])

    def softmax_tile(s_ref, p_ref):
        m_old = m_scr[...]
        m_new = jnp.maximum(m_old, jnp.max(s_ref[...], axis=0, keepdims=True))
        m_scr[...] = m_new
        p_ref[...] = jnp.exp((s_ref[...] - m_new).astype(BF16))
        return jnp.exp(m_old - m_new)

    own = pl.ds(pl.multiple_of(q0, tq), tq)
    s = _dot(k_ref[0, own, :NSA_HEAD_DIM], qa_scr[:NSA_HEAD_DIM, :])
    kpos = lax.broadcasted_iota(jnp.int32, (tq, 1), 0)
    qpos = lax.broadcasted_iota(jnp.int32, (1, cols), 1) & (tq - 1)
    s = jnp.where(kpos <= qpos, s, NEG)
    m_first = jnp.max(s, axis=0, keepdims=True)
    m_scr[...] = m_first
    acc_scr[...] = _dot(vt_ref[0, :, own], jnp.exp((s - m_first).astype(BF16)))

    n_tiles = (q0 + tk - 1) // tk

    nbuf = len(s_bufs)

    def window(w, carry):
        lo = w * tpw
        cnt = jnp.minimum(n_tiles - lo, tpw)
        set_window(w)
        s_bufs[0][...] = scores(lo)
        p_bufs[nbuf - 1][...] = jnp.zeros(p_bufs[nbuf - 1].shape, BF16)

        def group(t, alpha):
            first = lo + nbuf * t
            for i in range(nbuf):
                j = first + i
                s_bufs[(i + 1) % nbuf][...] = scores(j + 1)
                acc_scr[...] = alpha * acc_scr[...] + pv(jnp.maximum(j - 1, 0), p_bufs[(i - 1) % nbuf])
                alpha = softmax_tile(s_bufs[i], p_bufs[i])
            return alpha

        groups = (cnt + nbuf - 1) // nbuf
        alpha_last = lax.fori_loop(0, groups, group, jnp.ones((1, cols), F32))
        acc_scr[...] = alpha_last * acc_scr[...] + pv(lo + nbuf * groups - 1, p_bufs[nbuf - 1])
        return carry

    lax.fori_loop(0, (n_tiles + tpw - 1) // tpw, window, 0)
    acc = acc_scr[...]
    o_t = acc[:NSA_HEAD_DIM] / acc[NSA_HEAD_DIM:NSA_HEAD_DIM + 1]
    gates = jax.nn.sigmoid(gl_ref[0])
    lane = lax.broadcasted_iota(jnp.int32, gates.shape, 1)
    for h in range(NSA_HPG):
        col = pl.program_id(0) * NSA_HPG + h

        def gate(branch):
            return jnp.sum(jnp.where(lane == branch * NSA_HEADS + col, gates, 0.0), axis=-1, keepdims=True)

        sl = slice(h * NSA_HEAD_DIM, (h + 1) * NSA_HEAD_DIM)
        o_ref[:, sl] = gate(0) * oc_ref[:, sl] + gate(1) * o_t[:, h * tq:(h + 1) * tq].T + gate(2) * ow_ref[:, sl]


def _sel_attn(zn, zr, k_aug, vt_aug, sel, o_c, o_w, tq, tk):
    T = zn.shape[1]
    ns = sel.shape[1]
    wb = k_aug.shape[2] - NSA_HEAD_DIM
    assert tk % tq == 0 and (wb * SLC_BLOCK) % (SEL_BUFFERS * tk) == 0 and ns % wb == 0 and T % tk == 0
    kern = functools.partial(_sel_attn_kernel, tq=tq, tk=tk, wb=wb)
    cols = NSA_HPG * tq
    wide = pl.BlockSpec((tq, NSA_HPG * NSA_HEAD_DIM), lambda g, i: (i, g))
    return pl.pallas_call(
        kern,
        grid=(NSA_KV_GROUPS, T // tq),
        in_specs=[
            pl.BlockSpec((NSA_HPG, tq, NSA_HEAD_DIM), lambda g, i: (g, i, 0)),
            pl.BlockSpec((1, T, NSA_HEAD_DIM + wb), lambda g, i: (g, 0, 0)),
            pl.BlockSpec((1, NSA_HEAD_DIM + SEL_ONES_ROWS, T), lambda g, i: (g, 0, 0)),
            pl.BlockSpec((1, ns, tq), lambda g, i: (g, 0, i)),
            pl.BlockSpec((1, tq, LANES), lambda g, i: (RW_TILES - 1, i, 0)),
            wide, wide,
        ],
        out_specs=wide,
        out_shape=jax.ShapeDtypeStruct((T, NSA_WIDTH), F32),
        scratch_shapes=[
            pltpu.VMEM((ns, cols), F32),
            pltpu.VMEM((NSA_HEAD_DIM + wb, cols), BF16),
            pltpu.VMEM((NSA_HEAD_DIM + SEL_ONES_ROWS, cols), F32),
            pltpu.VMEM((1, cols), F32),
        ] + [pltpu.VMEM((tk, cols), F32)] * SEL_BUFFERS + [pltpu.VMEM((tk, cols), BF16)] * SEL_BUFFERS,
        compiler_params=_cparams(("parallel", "arbitrary")),
        name="nsa_sel_attn",
    )(zn, k_aug, vt_aug, sel, zr, o_c, o_w)


def _cmp_to_sel_matrix(ncp, ns):
    cmp_start = np.arange(ncp)[:, None] * CMP_STRIDE
    sel_start = np.arange(ns)[None, :] * SLC_BLOCK
    overlap = np.minimum(cmp_start + CMP_BLOCK, sel_start + SLC_BLOCK) - np.maximum(cmp_start, sel_start)
    return jnp.asarray(np.clip(overlap, 0, None).astype(np.float32).T / CMP_BLOCK, dtype=BF16)


def _nsa(zn, zr, zc, cmp_pos_k, cmp_w1_k, cmp_w2_k, cmp_pos_v, cmp_w1_v, cmp_w2_v, tq, tk, sel_window=LANES):
    T = zn.shape[1]
    width = CMP_BLOCK * NSA_HEAD_DIM
    pos = jnp.stack([cmp_pos_k.reshape(1, width), cmp_pos_v.reshape(1, width)])
    pos = jnp.broadcast_to(pos, (2, 8, width)).astype(BF16)
    w1 = jnp.stack([cmp_w1_k, cmp_w1_v]).astype(BF16)
    w2 = jnp.stack([cmp_w2_k, cmp_w2_v]).astype(BF16)
    kv_cmp = _compress(zc, pos, w1, w2)
    ns = T // SLC_BLOCK
    ks0 = NSA_HEADS + 2 * NSA_KV_GROUPS
    vs0 = ks0 + NSA_KV_GROUPS
    vw0 = vs0 + 2 * NSA_KV_GROUPS
    vw_t = jnp.swapaxes(zn[vw0:vw0 + NSA_KV_GROUPS], 1, 2)
    o_c, o_w, sel = _cmp_win(zn, kv_cmp, _cmp_to_sel_matrix(T // CMP_STRIDE, ns), vw_t, tq)
    wb = min(sel_window, ns)
    blk_onehot = (jnp.arange(T)[:, None] // SLC_BLOCK % wb == jnp.arange(wb)[None, :]).astype(BF16)
    k_aug = jnp.concatenate([zn[ks0:vs0], jnp.broadcast_to(blk_onehot, (NSA_KV_GROUPS, T, wb))], axis=2)
    vt_aug = jnp.concatenate([jnp.swapaxes(zn[vs0:vs0 + NSA_KV_GROUPS], 1, 2),
                              jnp.ones((NSA_KV_GROUPS, SEL_ONES_ROWS, T), BF16)], axis=1)
    return _sel_attn(zn, zr, k_aug, vt_aug, sel, o_c, o_w, tq, tk)


def _softplus(y):
    return jnp.maximum(y, 0.0) + jnp.log(1.0 + jnp.exp(-jnp.abs(y)))


def _rwkv_kernel(z_ref, prev_ref, mu_ref, pv_ref, wup_ref, aup_ref, gup_ref, o_ref,
                 rt_s, at_s, kh_s, bh_s, kb_s, bb_s, v_s, gc_s, g_s, bonus_s, y_s, state_s, *, tb):
    C = RWKV_CHUNK
    step = pl.program_id(0)

    @pl.when(step == 0)
    def _():
        state_s[...] = jnp.zeros(state_s.shape, F32)

    has_prev = jnp.where(step > 0, 1.0, 0.0)
    row = lax.broadcasted_iota(jnp.int32, (tb, LANES), 0)

    def shifted_mix(t):
        z = z_ref[t]
        zp = jnp.where(row == 0, prev_ref[t, 7:8, :] * has_prev, pltpu.roll(z, 1, axis=0))
        return z + (zp - z) * mu_ref[t]

    ri = lax.broadcasted_iota(jnp.int32, (tb, tb), 0)
    ci = lax.broadcasted_iota(jnp.int32, (tb, tb), 1)
    tri = jnp.where(((ri >> 6) == (ci >> 6)) & (ci <= ri), 1.0, 0.0).astype(BF16)
    wide = 2 * LANES
    wr = lax.broadcasted_iota(jnp.int32, (wide, wide), 0)
    wc = lax.broadcasted_iota(jnp.int32, (wide, wide), 1)
    head_sum = jnp.where((wr >> 6) == (wc >> 6), 1.0, 0.0).astype(BF16)
    head_mean = jnp.where((wr >> 6) == (wc >> 6), 1.0 / RWKV_HEAD_DIM, 0.0).astype(BF16)

    wa = shifted_mix(3 * RWKV_PAIRS)
    dw = _dot(jnp.tanh(wa).astype(BF16), wup_ref[...])
    da = _dot(wa.astype(BF16), aup_ref[...])
    g0 = jax.nn.sigmoid(shifted_mix(3 * RWKV_PAIRS + 1)).astype(BF16)
    g1 = jax.nn.sigmoid(shifted_mix(3 * RWKV_PAIRS + 2)).astype(BF16)
    g_s[...] = _dot(g0, gup_ref[:LANES]) + _dot(g1, gup_ref[LANES:])

    def mixed(first_tile, q):
        return jnp.concatenate([shifted_mix(first_tile + 2 * q), shifted_mix(first_tile + 2 * q + 1)], axis=1)

    for q in range(RWKV_PAIRS // 2):
        sl = slice(q * wide, (q + 1) * wide)
        r = mixed(0, q)
        k = mixed(RWKV_PAIRS, q)
        v = mixed(2 * RWKV_PAIRS, q)
        w_log = -_softplus(-(pv_ref[0:1, sl] + dw[:, sl])) - 0.5
        lw = -jnp.exp(w_log)
        icl = jax.nn.sigmoid(pv_ref[1:2, sl] + da[:, sl])
        kk = k * pv_ref[2:3, sl]
        kk = kk * lax.rsqrt(jnp.maximum(_dot((kk * kk).astype(BF16), head_sum), 1e-24))
        k2 = k * (1.0 + (icl - 1.0) * pv_ref[3:4, sl])
        bonus_s[:, sl] = _dot((r * k2 * pv_ref[4:5, sl]).astype(BF16), head_sum) * v
        hi, mid, lo = _split3(lw)
        cum = _dot(tri, hi) + _dot(tri, mid) + _dot(tri, lo)
        b = kk * icl
        inv_decay = jnp.exp(-cum)
        to_end = []
        for c in range(tb // C):
            e_end = jnp.exp(cum[c * C + C - 1:c * C + C])
            gc_s[8 * c:8 * c + 1, sl] = e_end
            to_end.append(inv_decay[c * C:(c + 1) * C] * e_end)
        to_end = jnp.concatenate(to_end, axis=0)
        rt_s[:, sl] = (r * jnp.exp(cum)).astype(BF16)
        at_s[:, sl] = (-kk * jnp.exp(cum - lw)).astype(BF16)
        kh_s[:, sl] = (k2 * inv_decay).astype(BF16)
        bh_s[:, sl] = (b * inv_decay).astype(BF16)
        kb_s[:, sl] = (k2 * to_end).astype(BF16)
        bb_s[:, sl] = (b * to_end).astype(BF16)
        v_s[:, sl] = v.astype(BF16)

    hr = lax.broadcasted_iota(jnp.int32, (LANES, LANES), 0)
    hc = lax.broadcasted_iota(jnp.int32, (LANES, LANES), 1)
    same_head = (hr >> 6) == (hc >> 6)
    t_r = hr & (C - 1)
    t_c = hc & (C - 1)
    strict = same_head & (t_c < t_r)
    incl = same_head & (t_c <= t_r)
    eye = jnp.where(hr == hc, 1.0, 0.0)
    lane_head = lax.broadcasted_iota(jnp.int32, (C, LANES), 1) >> 6

    def stack(zc):
        return jnp.concatenate([jnp.where(lane_head == 0, zc, jnp.zeros_like(zc)),
                                jnp.where(lane_head == 1, zc, jnp.zeros_like(zc))], axis=0)

    def chunk(c, carry):
        t0 = pl.multiple_of(c * C, C)
        rows = pl.ds(t0, C)
        pairs = range(RWKV_PAIRS)
        sls = [slice(p * LANES, (p + 1) * LANES) for p in pairs]
        lhs = [jnp.concatenate([stack(at_s[rows, sl]), stack(rt_s[rows, sl])], axis=0) for sl in sls]
        rhs = [jnp.concatenate([stack(kh_s[rows, sl]), stack(bh_s[rows, sl])], axis=0) for sl in sls]
        aa = [_dot_nt(lhs[p], rhs[p]) for p in pairs]
        a_ak = [jnp.where(strict, aa[p][:2 * C, :2 * C], 0.0).astype(BF16) for p in pairs]
        n_pow = [jnp.where(strict, aa[p][:2 * C, 2 * C:], 0.0) for p in pairs]
        a_r = [jnp.concatenate([jnp.where(incl, aa[p][2 * C:, :2 * C], 0.0).astype(BF16),
                                jnp.where(incl, aa[p][2 * C:, 2 * C:], 0.0).astype(BF16)], axis=1) for p in pairs]
        t_inv = [eye + n_pow[p] for p in pairs]
        for _ in range(5):
            nb = [n_pow[p].astype(BF16) for p in pairs]
            n_pow = [_dot(nb[p], nb[p]) for p in pairs]
            t_inv = [t_inv[p] + _dot(t_inv[p].astype(BF16), n_pow[p].astype(BF16)) for p in pairs]
        state = [state_s[p] for p in pairs]
        xs = [_dot_nt(lhs[p], state[p].astype(BF16)) for p in pairs]
        v_c = [v_s[rows, sl] for sl in sls]
        v_st = [stack(v_c[p]) for p in pairs]
        av = [_dot(a_ak[p], v_st[p]) for p in pairs]
        sa = [_dot(t_inv[p].astype(BF16), (xs[p][:2 * C] + av[p]).astype(BF16)) for p in pairs]
        ys = [xs[p][2 * C:] + _dot(a_r[p], jnp.concatenate([v_st[p], sa[p].astype(BF16)], axis=0)) for p in pairs]
        upd = [_dot_tn(jnp.concatenate([v_c[p], (sa[p][:C] + sa[p][C:]).astype(BF16)], axis=0),
                       jnp.concatenate([kb_s[rows, sls[p]], bb_s[rows, sls[p]]], axis=0)) for p in pairs]
        for p in pairs:
            y_s[rows, sls[p]] = ys[p][:C] + ys[p][C:]
            state_s[p] = (state[p] * gc_s[pl.ds(pl.multiple_of(c * 8, 8), 1), sls[p]]
                          + jnp.where(same_head, upd[p], 0.0))
        return carry

    lax.fori_loop(0, tb // C, chunk, 0)

    for q in range(RWKV_PAIRS // 2):
        sl = slice(q * wide, (q + 1) * wide)
        y = y_s[:, sl]
        d = y - _dot(y.astype(BF16), head_mean)
        var = _dot((d * d).astype(BF16), head_mean)
        yn = d * lax.rsqrt(var + LNX_EPS) * pv_ref[5:6, sl] + pv_ref[6:7, sl]
        o_ref[:, sl] = (yn + bonus_s[:, sl]) * g_s[:, sl]


def _hi_lo(w):
    hi = w.astype(BF16)
    return hi, (w - hi.astype(F32)).astype(BF16)


def _rwkv(zr, mu_t, w0, w_up, a0, a_up, g_up, k_k, k_a, r_k, lnx_g, lnx_b, tb):
    T = zr.shape[1]
    n_in = RW_TILES - 1
    pv = jnp.stack([w0, a0, k_k, k_a, r_k.reshape(-1), lnx_g, lnx_b, jnp.zeros_like(w0)])
    wup = jnp.pad(w_up, ((0, LANES - W_RANK), (0, 0))).astype(BF16)
    aup = jnp.pad(a_up, ((W_RANK, LANES - W_RANK - A_RANK), (0, 0))).astype(BF16)
    gup = jnp.pad(g_up, ((0, 2 * LANES - G_RANK), (0, 0))).astype(BF16)
    full = lambda shape: pl.BlockSpec(shape, lambda s: (0,) * len(shape))
    kern = functools.partial(_rwkv_kernel, tb=tb)
    bf = lambda: pltpu.VMEM((tb, RWKV_WIDTH), BF16)
    ff = lambda: pltpu.VMEM((tb, RWKV_WIDTH), F32)
    return pl.pallas_call(
        kern,
        grid=(T // tb,),
        in_specs=[
            pl.BlockSpec((n_in, tb, LANES), lambda s: (0, s, 0)),
            pl.BlockSpec((n_in, 8, LANES), lambda s: (0, jnp.maximum(s * (tb // 8) - 1, 0), 0)),
            full((n_in, 1, LANES)),
            full((8, RWKV_WIDTH)),
            full((LANES, RWKV_WIDTH)), full((LANES, RWKV_WIDTH)), full((2 * LANES, RWKV_WIDTH)),
        ],
        out_specs=pl.BlockSpec((tb, RWKV_WIDTH), lambda s: (s, 0)),
        out_shape=jax.ShapeDtypeStruct((T, RWKV_WIDTH), F32),
        scratch_shapes=[bf(), bf(), bf(), bf(), bf(), bf(), bf(),
                        pltpu.VMEM((8 * (tb // RWKV_CHUNK), RWKV_WIDTH), F32), ff(), ff(), ff(),
                        pltpu.VMEM((RWKV_PAIRS, LANES, LANES), F32)],
        compiler_params=_cparams(("arbitrary",)),
        name="rwkv7",
    )(zr, zr, mu_t, pv, wup, aup, gup)


PROJ_COLS = 512


def _mix_out_kernel(ya_ref, yb_ref, g_ref, w_ref, x_ref, o_ref):
    h = jnp.concatenate([_rms(ya_ref[...], g_ref[...]).astype(BF16), yb_ref[...].astype(BF16)], axis=1)
    for n in range(D_MODEL // PROJ_COLS):
        sl = slice(n * PROJ_COLS, (n + 1) * PROJ_COLS)
        o_ref[:, sl] = x_ref[:, sl] + _dot(h, w_ref[:, sl])


def _mix_out(y_a, y_b, g, w, x, tm):
    T = x.shape[0]
    return pl.pallas_call(
        _mix_out_kernel,
        grid=(T // tm,),
        in_specs=[
            pl.BlockSpec((tm, NSA_WIDTH), lambda m: (m, 0)),
            pl.BlockSpec((tm, RWKV_WIDTH), lambda m: (m, 0)),
            _resident((1, NSA_WIDTH)),
            _resident((NSA_WIDTH + RWKV_WIDTH, D_MODEL)),
            pl.BlockSpec((tm, D_MODEL), lambda m: (m, 0)),
        ],
        out_specs=pl.BlockSpec((tm, D_MODEL), lambda m: (m, 0)),
        out_shape=jax.ShapeDtypeStruct((T, D_MODEL), F32),
        compiler_params=_cparams(("parallel",)),
        name="mix_out_proj",
    )(y_a, y_b, g, w, x)


def _norm_mm_kernel(x_ref, g_ref, w_ref, o_ref, h_scr):
    @pl.when(pl.program_id(1) == 0)
    def _():
        h_scr[...] = _rms(x_ref[...], g_ref[...]).astype(BF16)

    o_ref[...] = _dot(h_scr[...], w_ref[...]).astype(o_ref.dtype)


def _norm_mm(x, g, w, tm, tn, name):
    M, K = x.shape
    N = w.shape[1]
    return pl.pallas_call(
        _norm_mm_kernel,
        grid=(M // tm, N // tn),
        in_specs=[
            pl.BlockSpec((tm, K), lambda m, n: (m, 0)),
            pl.BlockSpec((1, K), lambda m, n: (0, 0)),
            pl.BlockSpec((K, tn), lambda m, n: (0, n)),
        ],
        out_specs=pl.BlockSpec((tm, tn), lambda m, n: (m, n)),
        out_shape=jax.ShapeDtypeStruct((M, N), BF16),
        scratch_shapes=[pltpu.VMEM((tm, K), BF16)],
        compiler_params=_cparams(("parallel", "arbitrary")),
        name=name,
    )(x, g, w)


def _mem_router_kernel(x_ref, km_ref, vm_ref, wq_ref, wo_ref, g2_ref, g3_ref, w_hi, w_lo, b_ref,
                       x2_ref, h_ref, c_ref, n_ref):
    scale = MEM_HEAD_DIM ** -0.5
    hq = _rms(x_ref[...], g2_ref[...]).astype(BF16)
    heads = []
    for hd in range(MEM_HEADS):
        sl = slice(hd * MEM_HEAD_DIM, (hd + 1) * MEM_HEAD_DIM)
        q = _dot(hq, wq_ref[:, sl]).astype(BF16)
        s = _dot_nt(q, km_ref[:, sl]) * scale
        e = jnp.exp(s - jnp.max(s, axis=-1, keepdims=True))
        p = e / jnp.sum(e, axis=-1, keepdims=True)
        heads.append(_dot(p.astype(BF16), vm_ref[:, sl]).astype(BF16))
    o = jnp.concatenate(heads, axis=1)
    for n in range(D_MODEL // PROJ_COLS):
        sl = slice(n * PROJ_COLS, (n + 1) * PROJ_COLS)
        x2_ref[:, sl] = x_ref[:, sl] + _dot(o, wo_ref[:, sl])
    _route(x2_ref[...], g3_ref, w_hi, w_lo, b_ref, h_ref, c_ref, n_ref)


def _mem_router(x, k_mem, v_mem, wq, wo, g2, g3, w_hi, w_lo, b, tm):
    T = x.shape[0]
    M = k_mem.shape[0]
    row_block = lambda width: pl.BlockSpec((tm, width), lambda m: (m, 0))
    return pl.pallas_call(
        _mem_router_kernel,
        grid=(T // tm,),
        in_specs=[
            row_block(D_MODEL),
            _resident((M, D_MODEL)), _resident((M, D_MODEL)),
            _resident((D_MODEL, D_MODEL)), _resident((D_MODEL, D_MODEL)),
            _resident((1, D_MODEL)), _resident((1, D_MODEL)),
            _resident((D_MODEL, LANES)), _resident((D_MODEL, LANES)), _resident((1, LANES)),
        ],
        out_specs=[row_block(D_MODEL), row_block(D_MODEL), row_block(LANES),
                   pl.BlockSpec((1, 8, LANES), lambda m: (m, 0, 0))],
        out_shape=[
            jax.ShapeDtypeStruct((T, D_MODEL), F32),
            jax.ShapeDtypeStruct((T, D_MODEL), BF16),
            jax.ShapeDtypeStruct((T, LANES), F32),
            jax.ShapeDtypeStruct((T // tm, 8, LANES), F32),
        ],
        compiler_params=_cparams(("parallel",)),
        name="mem_attn_router",
    )(x, k_mem, v_mem, wq, wo, g2, g3, w_hi, w_lo, b)


def _route(x, g_ref, w_hi, w_lo, b_ref, h_ref, c_ref, n_ref):
    h = _rms(x, g_ref[...])
    h_ref[...] = h.astype(BF16)
    logits = _dot_xw3(h, w_hi[...], w_lo[...]) + b_ref[...]
    lane = lax.broadcasted_iota(jnp.int32, logits.shape, 1)
    big = jnp.int32(LANES)
    is_grp = (lane >= N_EXPERTS) & (lane < N_EXPERTS + N_GROUPS)
    lg = jnp.where(is_grp, logits, NEG)
    eg = jnp.where(is_grp, jnp.exp(lg - jnp.max(lg, axis=-1, keepdims=True)), 0.0)
    pg = eg / jnp.sum(eg, axis=-1, keepdims=True)
    pg_top = jnp.max(pg, axis=-1, keepdims=True)
    g_idx = jnp.min(jnp.where(is_grp & (pg == pg_top), lane - N_EXPERTS, big), axis=-1, keepdims=True)
    in_grp = (lane < N_EXPERTS) & ((lane >> 3) == g_idx)
    le = jnp.where(in_grp, logits, NEG)
    ee = jnp.where(in_grp, jnp.exp(le - jnp.max(le, axis=-1, keepdims=True)), 0.0)
    pe = jnp.where(in_grp, ee / jnp.sum(ee, axis=-1, keepdims=True), -1.0)
    p1 = jnp.max(pe, axis=-1, keepdims=True)
    hit1 = lane == jnp.min(jnp.where(pe == p1, lane, big), axis=-1, keepdims=True)
    pe2 = jnp.where(hit1, -1.0, pe)
    p2 = jnp.max(pe2, axis=-1, keepdims=True)
    hit2 = lane == jnp.min(jnp.where(pe2 == p2, lane, big), axis=-1, keepdims=True)
    denom = p1 + p2
    comb = jnp.where(hit1, pg_top * p1 / denom, 0.0) + jnp.where(hit2, pg_top * p2 / denom, 0.0)
    c_ref[...] = jnp.where(lane == GROUP_LANE, g_idx.astype(F32), comb)
    in_group = jnp.where(lane == g_idx, 1.0, 0.0)
    n_ref[0] = jnp.broadcast_to(jnp.sum(in_group, axis=0, keepdims=True), n_ref.shape[1:])


def _moe_kernel(meta_ref, h_ref, c_ref, wg_ref, wu_ref, wd_ref, y_ref, hs, cs, ys, slot_scr, perm, *, tm, sub):
    m = pl.program_id(0)
    e = pl.program_id(1)
    nslot = hs.shape[0]

    @pl.when(e == 0)
    def _():
        c = c_ref[...]
        lane = lax.broadcasted_iota(jnp.int32, c.shape, 1)
        one_hot = jnp.where(lane.astype(F32) == c[:, GROUP_LANE:GROUP_LANE + 1], 1.0, 0.0)
        earlier = jnp.where(lax.broadcasted_iota(jnp.int32, (tm, tm), 1)
                            < lax.broadcasted_iota(jnp.int32, (tm, tm), 0), 1.0, 0.0).astype(BF16)
        rank = _dot(earlier, one_hot.astype(BF16))
        start = jnp.zeros((1, LANES), F32)
        for g in range(N_GROUPS):
            start = jnp.where(lane[0:1] == g, (meta_ref[m * 2 * N_GROUPS + g] * sub).astype(F32), start)
        slot = jnp.sum(one_hot * (rank + start), axis=-1, keepdims=True)
        slot_scr[...] = jnp.broadcast_to(slot, slot_scr.shape)
        slot_row = slot_scr[...].T[0:1, :]
        perm[...] = jnp.where(lax.broadcasted_iota(jnp.int32, (nslot, tm), 0).astype(F32) == slot_row,
                              1.0, 0.0).astype(BF16)
        hs[...] = _dot(perm[...], h_ref[...]).astype(BF16)
        w_hi, w_lo = _split2(jnp.where(lane < N_EXPERTS, c, 0.0))
        cs[...] = _dot(perm[...], w_hi) + _dot(perm[...], w_lo)
        ys[...] = jnp.zeros(ys.shape, F32)

    grp = e // EXPERTS_PER_GROUP
    first = meta_ref[m * 2 * N_GROUPS + grp]
    count = meta_ref[m * 2 * N_GROUPS + N_GROUPS + grp]
    def ffn(start_block, n_blocks):
        n = n_blocks * sub
        rows = pl.ds(pl.multiple_of(start_block * sub, sub), n)
        x = hs[rows, :]
        hid = jax.nn.silu(_dot(x, wg_ref[0])) * _dot(x, wu_ref[0])
        lane_s = lax.broadcasted_iota(jnp.int32, (n, LANES), 1)
        c_e = jnp.sum(jnp.where(lane_s == e, cs[rows, :], 0.0), axis=-1, keepdims=True)
        ys[rows, :] += _dot((hid * c_e).astype(BF16), wd_ref[0])

    def pair(s, carry):
        ffn(first + 2 * s, 2)
        return carry

    lax.fori_loop(0, count // 2, pair, 0)

    @pl.when(count % 2 == 1)
    def _():
        ffn(first + count - 1, 1)

    @pl.when(e == pl.num_programs(1) - 1)
    def _():
        back = jnp.where(lax.broadcasted_iota(jnp.int32, (tm, nslot), 1).astype(F32) == slot_scr[:, 0:1],
                         1.0, 0.0).astype(BF16)
        for n in range(D_MODEL // PROJ_COLS):
            sl = slice(n * PROJ_COLS, (n + 1) * PROJ_COLS)
            y_ref[:, sl] = _dot(back, ys[:, sl].astype(BF16)).astype(y_ref.dtype)


def _moe(h, comb, counts, wg, wu, wd, tm, sub):
    T = h.shape[0]
    nt = T // tm
    nslot = tm + N_GROUPS * sub
    cnt = counts[:, 0, :N_GROUPS].astype(jnp.int32).reshape(nt, -1, N_GROUPS).sum(axis=1)
    nblk = (cnt + sub - 1) // sub
    first = jnp.cumsum(nblk, axis=1) - nblk
    meta = jnp.concatenate([first, nblk], axis=1).reshape(-1)
    once = lambda shape: pl.BlockSpec(shape, lambda m, e, meta: (m, 0), pipeline_mode=pl.Buffered(1))
    kern = functools.partial(_moe_kernel, tm=tm, sub=sub)
    return pl.pallas_call(
        kern,
        grid_spec=pltpu.PrefetchScalarGridSpec(
            num_scalar_prefetch=1,
            grid=(nt, N_EXPERTS),
            in_specs=[
                once((tm, D_MODEL)),
                once((tm, LANES)),
                pl.BlockSpec((1, D_MODEL, EXPERT_FF), lambda m, e, meta: (e, 0, 0)),
                pl.BlockSpec((1, D_MODEL, EXPERT_FF), lambda m, e, meta: (e, 0, 0)),
                pl.BlockSpec((1, EXPERT_FF, D_MODEL), lambda m, e, meta: (e, 0, 0)),
            ],
            out_specs=pl.BlockSpec((tm, D_MODEL), lambda m, e, meta: (m, 0)),
            scratch_shapes=[
                pltpu.VMEM((nslot, D_MODEL), BF16),
                pltpu.VMEM((nslot, LANES), F32),
                pltpu.VMEM((nslot, D_MODEL), F32),
                pltpu.VMEM((tm, LANES), F32),
                pltpu.VMEM((nslot, tm), BF16),
            ],
        ),
        out_shape=jax.ShapeDtypeStruct((T, D_MODEL), BF16),
        compiler_params=_cparams(("parallel", "arbitrary")),
        name="moe_experts",
    )(meta, h, comb, wg, wu, wd)


def _final_norm_kernel(x_ref, y_ref, g_ref, o_ref):
    o_ref[...] = _rms(x_ref[...] + y_ref[...].astype(F32), g_ref[...])


def _final_norm(x, y, g, tm):
    T = x.shape[0]
    blk = pl.BlockSpec((tm, D_MODEL), lambda m: (m, 0))
    return pl.pallas_call(
        _final_norm_kernel,
        grid=(T // tm,),
        in_specs=[blk, blk, _resident((1, D_MODEL))],
        out_specs=blk,
        out_shape=jax.ShapeDtypeStruct((T, D_MODEL), F32),
        compiler_params=_cparams(("parallel",)),
        name="moe_residual_norm",
    )(x, y, g)


def _pack_w_in(w_in, rwkv_mu):
    nsa_cols = NSA_WIDTH + 6 * NSA_KV_WIDTH
    gl = w_in[:, nsa_cols:nsa_cols + 3 * NSA_HEADS]
    rw0 = nsa_cols + 3 * NSA_HEADS
    rkv = w_in[:, rw0:rw0 + 3 * RWKV_WIDTH + W_RANK + A_RANK]
    gd = w_in[:, rw0 + 3 * RWKV_WIDTH + W_RANK + A_RANK:]

    def pad(a, n):
        return jnp.pad(a, ((0, 0), (0, n - a.shape[1])))

    w = jnp.concatenate([w_in[:, :nsa_cols], rkv, pad(gd, 2 * LANES), pad(gl, LANES)], axis=1)
    mu = rwkv_mu.reshape(1, -1)
    mu_rkv = mu[:, :3 * RWKV_WIDTH + W_RANK + A_RANK]
    mu_gd = pad(mu[:, 3 * RWKV_WIDTH + W_RANK + A_RANK:], 2 * LANES)
    mu_t = jnp.concatenate([mu_rkv, mu_gd], axis=1).reshape(RW_TILES - 1, 1, LANES)
    return w.astype(BF16), mu_t


def kernel(x, mem, ln1_g, w_in, cmp_pos_k, cmp_w1_k, cmp_w2_k, cmp_pos_v, cmp_w1_v, cmp_w2_v, nsa_norm_g, rwkv_mu, rwkv_w0, rwkv_w_up, rwkv_a0, rwkv_a_up, rwkv_g_up, rwkv_k_k, rwkv_k_a, rwkv_r_k, rwkv_lnx_g, rwkv_lnx_b, w_out, ln_mem_g, ln2_g, wq_mem, wk_mem, wv_mem, wo_mem, ln3_g, router_group_w, router_group_b, router_expert_w, router_expert_b, moe_w_gate, moe_w_up, moe_w_down, lnf_g):
    B, T, _ = x.shape
    assert B == 1 and T % max(TM_PROJ, TK_SEL, TB_RWKV, TM_MOE) == 0
    assert w_in.shape[0] == 1
    row = lambda a: a.reshape(1, -1)
    xs = x[0]
    for l in range(w_in.shape[0]):
        wp, mu_t = _pack_w_in(w_in[l], rwkv_mu[l])
        zn, zr, zc = _in_proj(xs, row(ln1_g[l]), wp, TM_PROJ)
        y_a = _nsa(zn, zr, zc, cmp_pos_k[l], cmp_w1_k[l], cmp_w2_k[l], cmp_pos_v[l], cmp_w1_v[l], cmp_w2_v[l],
                   TQ_NSA, TK_SEL)
        y_b = _rwkv(zr, mu_t, rwkv_w0[l], rwkv_w_up[l], rwkv_a0[l], rwkv_a_up[l], rwkv_g_up[l], rwkv_k_k[l],
                    rwkv_k_a[l], rwkv_r_k[l], rwkv_lnx_g[l], rwkv_lnx_b[l], TB_RWKV)
        xs = _mix_out(y_a, y_b, row(nsa_norm_g[l]), w_out[l].astype(BF16), xs, TM_PROJ)
        m_tok = mem[0]
        k_mem = _norm_mm(m_tok, row(ln_mem_g[l]), wk_mem[l].astype(BF16), m_tok.shape[0], PROJ_COLS, "mem_k")
        v_mem = _norm_mm(m_tok, row(ln_mem_g[l]), wv_mem[l].astype(BF16), m_tok.shape[0], PROJ_COLS, "mem_v")
        w_r = jnp.pad(jnp.concatenate([router_expert_w[l], router_group_w[l]], axis=1),
                      ((0, 0), (0, LANES - N_EXPERTS - N_GROUPS)))
        b_r = jnp.pad(jnp.concatenate([router_expert_b[l], router_group_b[l]]), (0, LANES - N_EXPERTS - N_GROUPS))
        xs, h3, comb, counts = _mem_router(xs, k_mem, v_mem, wq_mem[l].astype(BF16), wo_mem[l].astype(BF16),
                                   row(ln2_g[l]), row(ln3_g[l]), *_hi_lo(w_r), row(b_r), TM_PROJ)
        y_moe = _moe(h3, comb, counts, moe_w_gate[l].astype(BF16), moe_w_up[l].astype(BF16),
                     moe_w_down[l].astype(BF16), TM_MOE, SUB_MOE)
        xs = _final_norm(xs, y_moe, row(lnf_g), TM_PROJ)
    return xs[None]
```

```python
import functools

import numpy as np
import jax
import jax.numpy as jnp
from jax import lax
from jax.experimental import pallas as pl
from jax.experimental.pallas import tpu as pltpu

F32 = jnp.float32
BF16 = jnp.bfloat16

LANES = 128
D_MODEL = 2048
EPS = 1e-6
NSA_HEAD_DIM = 128
NSA_HEADS = 8
NSA_KV_GROUPS = 2
NSA_HPG = NSA_HEADS // NSA_KV_GROUPS
NSA_WIDTH = NSA_HEADS * NSA_HEAD_DIM
NSA_KV_WIDTH = NSA_KV_GROUPS * NSA_HEAD_DIM
CMP_BLOCK = 32
CMP_STRIDE = 16
SLC_BLOCK = 64
SLC_TOPK = 16
WINDOW = 512
NEG = -1e30
RWKV_HEAD_DIM = 64
RWKV_HEADS = 16
RWKV_WIDTH = RWKV_HEADS * RWKV_HEAD_DIM
RWKV_PAIRS = RWKV_WIDTH // LANES
W_RANK = 64
A_RANK = 64
G_RANK = 160
LNX_EPS = 64e-5
RWKV_CHUNK = 64
MEM_HEADS = 4
MEM_HEAD_DIM = D_MODEL // MEM_HEADS
N_GROUPS = 4
EXPERTS_PER_GROUP = 8
N_EXPERTS = N_GROUPS * EXPERTS_PER_GROUP
EXPERT_FF = 256

NSA_TILES = 20
RW_TILES = 28
IN_TILE_BLOCK = 4

VMEM_LIMIT = 56 * 1024 * 1024

TM_PROJ = 256
TM_MEM = 512
TQ_NSA = 128
TQ_SEL = 256
TK_SEL = 512
TB_RWKV = 256
TM_MOE = 1024
SUB_MOE = 128
GROUP_LANE = 64


def _cparams(sem):
    return pltpu.CompilerParams(dimension_semantics=sem, vmem_limit_bytes=VMEM_LIMIT)


def _dot(a, b):
    return jnp.dot(a, b, preferred_element_type=F32)


def _dot_nt(a, b):
    return lax.dot_general(a, b, (((1,), (1,)), ((), ())), preferred_element_type=F32)


def _dot_tn(a, b):
    return lax.dot_general(a, b, (((0,), (0,)), ((), ())), preferred_element_type=F32)


def _split2(x):
    hi = x.astype(BF16)
    lo = (x - hi.astype(F32)).astype(BF16)
    return hi, lo


def _split3(x):
    hi = x.astype(BF16)
    r1 = x - hi.astype(F32)
    mid = r1.astype(BF16)
    lo = (r1 - mid.astype(F32)).astype(BF16)
    return hi, mid, lo


def _dot_x2(x, w):
    hi, lo = _split2(x)
    return _dot(hi, w) + _dot(lo, w)


def _dot_x3(x, w):
    hi, mid, lo = _split3(x)
    return _dot(hi, w) + _dot(mid, w) + _dot(lo, w)


def _dot_xw3(x, w_hi, w_lo):
    hi, lo = _split2(x)
    return _dot(hi, w_hi) + _dot(lo, w_hi) + _dot(hi, w_lo)


def _rms(x, g):
    ms = jnp.mean(x * x, axis=-1, keepdims=True)
    return x * lax.rsqrt(ms + EPS) * g


def _resident(shape):
    return pl.BlockSpec(shape, lambda *_: (0,) * len(shape), pipeline_mode=pl.Buffered(1))


CMP_TILES = 2 * NSA_KV_GROUPS


def _in_proj_kernel(x_ref, g_ref, w_ref, zn_ref, zr_ref, zc_ref, *, q_scale):
    h = _rms(x_ref[...], g_ref[...]).astype(BF16)
    nb = IN_TILE_BLOCK * LANES
    for n in range((NSA_TILES + RW_TILES) // IN_TILE_BLOCK):
        z = _dot(h, w_ref[:, n * nb:(n + 1) * nb])
        for t in range(IN_TILE_BLOCK):
            tile = n * IN_TILE_BLOCK + t
            zt = z[:, t * LANES:(t + 1) * LANES]
            if tile < NSA_HEADS:
                zn_ref[tile] = (zt * q_scale).astype(BF16)
            elif tile < NSA_TILES:
                zn_ref[tile] = zt.astype(BF16)
                if tile < NSA_HEADS + CMP_TILES:
                    zc_ref[tile - NSA_HEADS] = zt
            else:
                zr_ref[tile - NSA_TILES] = zt


def _in_proj(x, g, w, tm):
    T = x.shape[0]
    kern = functools.partial(_in_proj_kernel, q_scale=NSA_HEAD_DIM ** -0.5)
    return pl.pallas_call(
        kern,
        grid=(T // tm,),
        in_specs=[
            pl.BlockSpec((tm, D_MODEL), lambda m: (m, 0)),
            _resident((1, D_MODEL)),
            _resident((D_MODEL, (NSA_TILES + RW_TILES) * LANES)),
        ],
        out_specs=[
            pl.BlockSpec((NSA_TILES, tm, LANES), lambda m: (0, m, 0)),
            pl.BlockSpec((RW_TILES, tm, LANES), lambda m: (0, m, 0)),
            pl.BlockSpec((CMP_TILES, tm, LANES), lambda m: (0, m, 0)),
        ],
        out_shape=[
            jax.ShapeDtypeStruct((NSA_TILES, T, LANES), BF16),
            jax.ShapeDtypeStruct((RW_TILES, T, LANES), F32),
            jax.ShapeDtypeStruct((CMP_TILES, T, LANES), F32),
        ],
        compiler_params=_cparams(("parallel",)),
        name="in_proj",
    )(x, g, w)


def _compress_kernel(x_ref, pos_ref, w1_ref, w2_ref, o_ref):
    n = x_ref.shape[1] // CMP_STRIDE
    w1 = w1_ref[0]
    h_first = jnp.zeros((n, NSA_HEAD_DIM), F32)
    h_second = jnp.zeros((n, NSA_HEAD_DIM), F32)
    for t in range(CMP_STRIDE):
        x_t = x_ref[0, pl.ds(t, n, stride=CMP_STRIDE), :].astype(BF16)
        h_first += _dot(x_t, w1[t * NSA_HEAD_DIM:(t + 1) * NSA_HEAD_DIM])
        h_second += _dot(x_t, w1[(CMP_STRIDE + t) * NSA_HEAD_DIM:(CMP_STRIDE + t + 1) * NSA_HEAD_DIM])
    h_next = pltpu.roll(h_second, n - 1, axis=0)
    bias = _dot(pos_ref[0], w1)[0:1]
    hid = jax.nn.gelu(h_first + h_next + bias)
    o_ref[0] = _dot(hid.astype(BF16), w2_ref[0]).astype(o_ref.dtype)


def _compress(zc, pos, w1, w2):
    T = zc.shape[1]
    n = T // CMP_STRIDE
    width = CMP_BLOCK * NSA_HEAD_DIM
    return pl.pallas_call(
        _compress_kernel,
        grid=(4,),
        in_specs=[
            pl.BlockSpec((1, T, NSA_HEAD_DIM), lambda s: (s, 0, 0)),
            pl.BlockSpec((1, 8, width), lambda s: (s // 2, 0, 0)),
            pl.BlockSpec((1, width, NSA_HEAD_DIM), lambda s: (s // 2, 0, 0)),
            pl.BlockSpec((1, NSA_HEAD_DIM, NSA_HEAD_DIM), lambda s: (s // 2, 0, 0)),
        ],
        out_specs=pl.BlockSpec((1, n, NSA_HEAD_DIM), lambda s: (s, 0, 0)),
        out_shape=jax.ShapeDtypeStruct((4, n, NSA_HEAD_DIM), BF16),
        compiler_params=_cparams(("parallel",)),
        name="nsa_compress",
    )(zc, pos, w1, w2)


def _cmp_win_kernel(q_ref, kc_ref, lhs_ref, kw_ref, vwt_ref, oc_ref, ow_ref, sel_ref, s_scr, acc_scr,
                    *, tq, top_n, ch):
    q0 = pl.program_id(1) * tq
    cols = NSA_HPG * tq
    ncp = kc_ref.shape[1]
    ns = lhs_ref.shape[1] - NSA_HEAD_DIM - SEL_ONES_ROWS
    q_t = jnp.concatenate([q_ref[h].astype(F32).T.astype(BF16) for h in range(NSA_HPG)], axis=1)
    qpos = q0 + (lax.broadcasted_iota(jnp.int32, (1, cols), 1) & (tq - 1))

    def store_heads(o_ref, o_t):
        for h in range(NSA_HPG):
            o_ref[:, h * NSA_HEAD_DIM:(h + 1) * NSA_HEAD_DIM] = o_t[:, h * tq:(h + 1) * tq].T

    span = WINDOW + tq
    start = pl.multiple_of(jnp.maximum(q0 - WINDOW, 0), tq)
    kpos = start + lax.broadcasted_iota(jnp.int32, (span, 1), 0)
    in_window = (qpos - kpos).astype(jnp.uint32) < jnp.uint32(WINDOW)
    s_w = jnp.where(in_window, _dot(kw_ref[0, pl.ds(start, span), :], q_t), NEG)
    e_w = jnp.exp(s_w - jnp.max(s_w, axis=0, keepdims=True))
    p_w = e_w / jnp.sum(e_w, axis=0, keepdims=True)
    store_heads(ow_ref, _dot(vwt_ref[0, :, pl.ds(start, span)], p_w.astype(BF16)))

    visible = (q0 + tq - 1 - (CMP_BLOCK - 1)) // CMP_STRIDE + 1
    clean = jnp.maximum((q0 - (CMP_BLOCK - 1)) // CMP_STRIDE + 1, 0) // ch
    n_chunks = (visible + ch - 1) // ch

    def chunk_rows(c):
        return pl.ds(pl.multiple_of(c * ch, ch), ch)

    def score_chunk(c, m, masked):
        s = _dot(kc_ref[0, chunk_rows(c), :], q_t)
        if masked:
            cmp_end = (c * ch + lax.broadcasted_iota(jnp.int32, (ch, 1), 0)) * CMP_STRIDE + (CMP_BLOCK - 1)
            s = jnp.where(cmp_end <= qpos, s, NEG)
        s_scr[chunk_rows(c), :] = s
        return jnp.maximum(m, jnp.max(s, axis=0, keepdims=True))

    m = jnp.full((1, cols), NEG, F32)
    m = lax.fori_loop(0, clean, lambda c, m: score_chunk(c, m, False), m)
    m = lax.fori_loop(clean, n_chunks, lambda c, m: score_chunk(c, m, True), m)
    acc_scr[...] = jnp.zeros(acc_scr.shape, F32)

    def weigh_chunk(c, carry):
        e = jnp.exp(s_scr[chunk_rows(c), :] - m).astype(BF16)
        acc_scr[...] += _dot(lhs_ref[0, :, chunk_rows(c)], e)
        return carry

    lax.fori_loop(0, n_chunks, weigh_chunk, 0)
    acc = acc_scr[...]
    sees_any = jnp.where(qpos >= CMP_BLOCK - 1, 1.0, 0.0)
    r_inv = sees_any / jnp.maximum(acc[NSA_HEAD_DIM + ns:NSA_HEAD_DIM + ns + 1], 1e-30)
    store_heads(oc_ref, acc[:NSA_HEAD_DIM] * r_inv)
    weighted = acc[NSA_HEAD_DIM:NSA_HEAD_DIM + ns] * r_inv
    p_sel = weighted[:, 0:tq]
    for h in range(1, NSA_HPG):
        p_sel = p_sel + weighted[:, h * tq:(h + 1) * tq]
    blk = (q0 + lax.broadcasted_iota(jnp.int32, (1, tq), 1)) >> 6
    j = lax.broadcasted_iota(jnp.int32, (ns, tq), 0)
    future = j > blk
    forced = (j == 0) | (j == blk) | (j == blk - 1)
    score = jnp.where(future, -1.0, jnp.where(forced, 1e6, p_sel))
    sel = jnp.zeros((ns, tq), F32)
    for _ in range(top_n):
        best = jnp.max(score, axis=0, keepdims=True)
        idx = jnp.min(jnp.where(score == best, j, ns), axis=0, keepdims=True)
        hit = j == idx
        sel = jnp.where(hit, jnp.where(best >= 0.0, 1.0, 0.0), sel)
        score = jnp.where(hit, -2.0, score)
    sel_ref[0] = sel.astype(sel_ref.dtype)


CMP_CHUNK = 256


def _cmp_win(zn, kv_cmp, cmp_to_sel, vw_t, tq):
    T = zn.shape[1]
    ncp = kv_cmp.shape[1]
    ns = cmp_to_sel.shape[0]
    ch = min(CMP_CHUNK, ncp)
    assert ncp % ch == 0
    kw_tile0 = NSA_HEADS + 4 * NSA_KV_GROUPS
    lhs = jnp.concatenate([jnp.swapaxes(kv_cmp[NSA_KV_GROUPS:], 1, 2),
                           jnp.broadcast_to(cmp_to_sel, (NSA_KV_GROUPS, ns, ncp)),
                           jnp.ones((NSA_KV_GROUPS, SEL_ONES_ROWS, ncp), BF16)], axis=1)
    rows = lhs.shape[1]
    cols = NSA_HPG * tq
    kern = functools.partial(_cmp_win_kernel, tq=tq, top_n=min(SLC_TOPK, ns), ch=ch)
    wide = pl.BlockSpec((tq, NSA_HPG * NSA_HEAD_DIM), lambda g, i: (i, g))
    return pl.pallas_call(
        kern,
        grid=(NSA_KV_GROUPS, T // tq),
        in_specs=[
            pl.BlockSpec((NSA_HPG, tq, NSA_HEAD_DIM), lambda g, i: (g, i, 0)),
            pl.BlockSpec((1, ncp, NSA_HEAD_DIM), lambda g, i: (g, 0, 0)),
            pl.BlockSpec((1, rows, ncp), lambda g, i: (g, 0, 0)),
            pl.BlockSpec((1, T, NSA_HEAD_DIM), lambda g, i: (kw_tile0 + g, 0, 0)),
            pl.BlockSpec((1, NSA_HEAD_DIM, T), lambda g, i: (g, 0, 0)),
        ],
        out_specs=[wide, wide, pl.BlockSpec((1, ns, tq), lambda g, i: (g, 0, i))],
        out_shape=[
            jax.ShapeDtypeStruct((T, NSA_WIDTH), F32),
            jax.ShapeDtypeStruct((T, NSA_WIDTH), F32),
            jax.ShapeDtypeStruct((NSA_KV_GROUPS, ns, T), BF16),
        ],
        scratch_shapes=[pltpu.VMEM((ncp, cols), F32), pltpu.VMEM((rows, cols), F32)],
        compiler_params=_cparams(("parallel", "arbitrary")),
        name="nsa_cmp_window",
    )(zn, kv_cmp, lhs, zn, vw_t)


SEL_ONES_ROWS = 16


SEL_BUFFERS = 2


def _sel_attn_kernel(q_ref, k_ref, vt_ref, sel_ref, gl_ref, oc_ref, ow_ref, o_ref,
                     bias_scr, qa_scr, acc_scr, m_scr, *bufs, tq, tk, wb):
    s_bufs, p_bufs = bufs[:SEL_BUFFERS], bufs[SEL_BUFFERS:]
    q0 = pl.program_id(1) * tq
    cols = NSA_HPG * tq
    tpw = wb * SLC_BLOCK // tk
    last_tile = k_ref.shape[1] // tk - 1
    ns = sel_ref.shape[1]
    first_own = q0 // SLC_BLOCK
    not_chosen = (sel_ref[0].astype(F32) - 1.0) * (-NEG)
    for h in range(NSA_HPG):
        bias_scr[:, h * tq:(h + 1) * tq] = not_chosen
        qa_scr[:NSA_HEAD_DIM, h * tq:(h + 1) * tq] = q_ref[h].astype(F32).T.astype(BF16)

    def set_window(w):
        rows = pl.ds(pl.multiple_of(w * wb, wb), wb)
        before = w * wb + lax.broadcasted_iota(jnp.int32, (wb, 1), 0) < first_own
        qa_scr[NSA_HEAD_DIM:, :] = jnp.where(before, bias_scr[rows, :], NEG).astype(BF16)

    def scores(j):
        j = jnp.minimum(j, last_tile)
        return _dot(k_ref[0, pl.ds(pl.multiple_of(j * tk, tk), tk), :], qa_scr[...])

    def pv(j, p_ref):
        return _dot(vt_ref[0, :, pl.ds(pl.multiple_of(j * tk, tk), tk)], p_ref[...])

    def score_into(i, j):
        s = scores(j)
        s_bufs[i][...] = s
        m_scr[8 * (i + 1):8 * (i + 1) + 1, :] = jnp.max(s, axis=0, keepdims=True)

    def softmax_tile(i):
        m_old = m_scr[0:1, :]
        m_new = jnp.maximum(m_old, m_scr[8 * (i + 1):8 * (i + 1) + 1, :])
        m_scr[0:1, :] = m_new
        p_bufs[i][...] = jnp.exp((s_bufs[i][...] - m_new).astype(BF16))
        return jnp.exp(m_old - m_new)

    own = pl.ds(pl.multiple_of(q0, tq), tq)
    s = _dot(k_ref[0, own, :NSA_HEAD_DIM], qa_scr[:NSA_HEAD_DIM, :])
    s = jnp.concatenate([s[b * SLC_BLOCK:(b + 1) * SLC_BLOCK] + bias_scr[pl.ds(first_own + b, 1), :]
                         for b in range(tq // SLC_BLOCK)], axis=0)
    kpos = lax.broadcasted_iota(jnp.int32, (tq, 1), 0)
    qpos = lax.broadcasted_iota(jnp.int32, (1, cols), 1) & (tq - 1)
    s = jnp.where(kpos <= qpos, s, NEG)
    m_first = jnp.max(s, axis=0, keepdims=True)
    m_scr[0:1, :] = m_first
    acc_scr[...] = _dot(vt_ref[0, :, own], jnp.exp((s - m_first).astype(BF16)))

    n_tiles = (q0 + tk - 1) // tk

    nbuf = len(s_bufs)

    def window(w, carry):
        lo = w * tpw
        cnt = jnp.minimum(n_tiles - lo, tpw)
        set_window(w)
        score_into(0, lo)
        p_bufs[nbuf - 1][...] = jnp.zeros(p_bufs[nbuf - 1].shape, BF16)

        def group(t, alpha):
            first = lo + nbuf * t
            for i in range(nbuf):
                j = first + i
                score_into((i + 1) % nbuf, j + 1)
                acc_scr[...] = alpha * acc_scr[...] + pv(jnp.maximum(j - 1, 0), p_bufs[(i - 1) % nbuf])
                alpha = softmax_tile(i)
            return alpha

        groups = (cnt + nbuf - 1) // nbuf
        alpha_last = lax.fori_loop(0, groups, group, jnp.ones((1, cols), F32))
        acc_scr[...] = alpha_last * acc_scr[...] + pv(lo + nbuf * groups - 1, p_bufs[nbuf - 1])
        return carry

    lax.fori_loop(0, (n_tiles + tpw - 1) // tpw, window, 0)
    acc = acc_scr[...]
    o_t = acc[:NSA_HEAD_DIM] / acc[NSA_HEAD_DIM:NSA_HEAD_DIM + 1]
    gates = jax.nn.sigmoid(gl_ref[0])
    lane = lax.broadcasted_iota(jnp.int32, gates.shape, 1)
    for h in range(NSA_HPG):
        col = pl.program_id(0) * NSA_HPG + h

        def gate(branch):
            return jnp.sum(jnp.where(lane == branch * NSA_HEADS + col, gates, 0.0), axis=-1, keepdims=True)

        sl = slice(h * NSA_HEAD_DIM, (h + 1) * NSA_HEAD_DIM)
        o_ref[:, sl] = gate(0) * oc_ref[:, sl] + gate(1) * o_t[:, h * tq:(h + 1) * tq].T + gate(2) * ow_ref[:, sl]


def _sel_attn(zn, zr, k_aug, vt_aug, sel, o_c, o_w, tq, tk):
    T = zn.shape[1]
    ns = sel.shape[1]
    wb = k_aug.shape[2] - NSA_HEAD_DIM
    assert tk % tq == 0 and (wb * SLC_BLOCK) % (SEL_BUFFERS * tk) == 0 and ns % wb == 0 and T % tk == 0
    kern = functools.partial(_sel_attn_kernel, tq=tq, tk=tk, wb=wb)
    cols = NSA_HPG * tq
    wide = pl.BlockSpec((tq, NSA_HPG * NSA_HEAD_DIM), lambda g, i: (i, g))
    return pl.pallas_call(
        kern,
        grid=(NSA_KV_GROUPS, T // tq),
        in_specs=[
            pl.BlockSpec((NSA_HPG, tq, NSA_HEAD_DIM), lambda g, i: (g, i, 0)),
            pl.BlockSpec((1, T, NSA_HEAD_DIM + wb), lambda g, i: (g, 0, 0)),
            pl.BlockSpec((1, NSA_HEAD_DIM + SEL_ONES_ROWS, T), lambda g, i: (g, 0, 0)),
            pl.BlockSpec((1, ns, tq), lambda g, i: (g, 0, i)),
            pl.BlockSpec((1, tq, LANES), lambda g, i: (RW_TILES - 1, i, 0)),
            wide, wide,
        ],
        out_specs=wide,
        out_shape=jax.ShapeDtypeStruct((T, NSA_WIDTH), F32),
        scratch_shapes=[
            pltpu.VMEM((ns, cols), F32),
            pltpu.VMEM((NSA_HEAD_DIM + wb, cols), BF16),
            pltpu.VMEM((NSA_HEAD_DIM + SEL_ONES_ROWS, cols), F32),
            pltpu.VMEM((8 * (1 + SEL_BUFFERS), cols), F32),
        ] + [pltpu.VMEM((tk, cols), F32)] * SEL_BUFFERS + [pltpu.VMEM((tk, cols), BF16)] * SEL_BUFFERS,
        compiler_params=_cparams(("parallel", "arbitrary")),
        name="nsa_sel_attn",
    )(zn, k_aug, vt_aug, sel, zr, o_c, o_w)


def _cmp_to_sel_matrix(ncp, ns):
    cmp_start = np.arange(ncp)[:, None] * CMP_STRIDE
    sel_start = np.arange(ns)[None, :] * SLC_BLOCK
    overlap = np.minimum(cmp_start + CMP_BLOCK, sel_start + SLC_BLOCK) - np.maximum(cmp_start, sel_start)
    return jnp.asarray(np.clip(overlap, 0, None).astype(np.float32).T / CMP_BLOCK, dtype=BF16)


def _nsa(zn, zr, zc, cmp_pos_k, cmp_w1_k, cmp_w2_k, cmp_pos_v, cmp_w1_v, cmp_w2_v, tq, tq_sel, tk,
         sel_window=LANES):
    T = zn.shape[1]
    width = CMP_BLOCK * NSA_HEAD_DIM
    pos = jnp.stack([cmp_pos_k.reshape(1, width), cmp_pos_v.reshape(1, width)])
    pos = jnp.broadcast_to(pos, (2, 8, width)).astype(BF16)
    w1 = jnp.stack([cmp_w1_k, cmp_w1_v]).astype(BF16)
    w2 = jnp.stack([cmp_w2_k, cmp_w2_v]).astype(BF16)
    kv_cmp = _compress(zc, pos, w1, w2)
    ns = T // SLC_BLOCK
    ks0 = NSA_HEADS + 2 * NSA_KV_GROUPS
    vs0 = ks0 + NSA_KV_GROUPS
    vw0 = vs0 + 2 * NSA_KV_GROUPS
    vw_t = jnp.swapaxes(zn[vw0:vw0 + NSA_KV_GROUPS], 1, 2)
    o_c, o_w, sel = _cmp_win(zn, kv_cmp, _cmp_to_sel_matrix(T // CMP_STRIDE, ns), vw_t, tq)
    wb = min(sel_window, ns)
    blk_onehot = (jnp.arange(T)[:, None] // SLC_BLOCK % wb == jnp.arange(wb)[None, :]).astype(BF16)
    k_aug = jnp.concatenate([zn[ks0:vs0], jnp.broadcast_to(blk_onehot, (NSA_KV_GROUPS, T, wb))], axis=2)
    vt_aug = jnp.concatenate([jnp.swapaxes(zn[vs0:vs0 + NSA_KV_GROUPS], 1, 2),
                              jnp.ones((NSA_KV_GROUPS, SEL_ONES_ROWS, T), BF16)], axis=1)
    return _sel_attn(zn, zr, k_aug, vt_aug, sel, o_c, o_w, tq_sel, tk)


def _softplus(y):
    return jnp.maximum(y, 0.0) + jnp.log(1.0 + jnp.exp(-jnp.abs(y)))


def _rwkv_kernel(z_ref, prev_ref, mu_ref, pv_ref, wup_ref, aup_ref, gup_ref, o_ref,
                 rt_s, at_s, kh_s, bh_s, kb_s, bb_s, v_s, gc_s, g_s, bonus_s, y_s, state_s, *, tb):
    C = RWKV_CHUNK
    step = pl.program_id(0)

    @pl.when(step == 0)
    def _():
        state_s[...] = jnp.zeros(state_s.shape, F32)

    has_prev = jnp.where(step > 0, 1.0, 0.0)
    row = lax.broadcasted_iota(jnp.int32, (tb, LANES), 0)

    def shifted_mix(t):
        z = z_ref[t]
        zp = jnp.where(row == 0, prev_ref[t, 7:8, :] * has_prev, pltpu.roll(z, 1, axis=0))
        return z + (zp - z) * mu_ref[t]

    ri = lax.broadcasted_iota(jnp.int32, (tb, tb), 0)
    ci = lax.broadcasted_iota(jnp.int32, (tb, tb), 1)
    tri = jnp.where(((ri >> 6) == (ci >> 6)) & (ci <= ri), 1.0, 0.0).astype(BF16)
    wide = 2 * LANES
    wr = lax.broadcasted_iota(jnp.int32, (wide, wide), 0)
    wc = lax.broadcasted_iota(jnp.int32, (wide, wide), 1)
    head_sum = jnp.where((wr >> 6) == (wc >> 6), 1.0, 0.0).astype(BF16)
    head_mean = jnp.where((wr >> 6) == (wc >> 6), 1.0 / RWKV_HEAD_DIM, 0.0).astype(BF16)

    wa = shifted_mix(3 * RWKV_PAIRS)
    dw = _dot(jnp.tanh(wa).astype(BF16), wup_ref[...])
    da = _dot(wa.astype(BF16), aup_ref[...])
    g0 = jax.nn.sigmoid(shifted_mix(3 * RWKV_PAIRS + 1)).astype(BF16)
    g1 = jax.nn.sigmoid(shifted_mix(3 * RWKV_PAIRS + 2)).astype(BF16)
    g_s[...] = _dot(g0, gup_ref[:LANES]) + _dot(g1, gup_ref[LANES:])

    def mixed(first_tile, q):
        return jnp.concatenate([shifted_mix(first_tile + 2 * q), shifted_mix(first_tile + 2 * q + 1)], axis=1)

    for q in range(RWKV_PAIRS // 2):
        sl = slice(q * wide, (q + 1) * wide)
        r = mixed(0, q)
        k = mixed(RWKV_PAIRS, q)
        v = mixed(2 * RWKV_PAIRS, q)
        w_log = -_softplus(-(pv_ref[0:1, sl] + dw[:, sl])) - 0.5
        lw = -jnp.exp(w_log)
        icl = jax.nn.sigmoid(pv_ref[1:2, sl] + da[:, sl])
        kk = k * pv_ref[2:3, sl]
        kk = kk * lax.rsqrt(jnp.maximum(_dot((kk * kk).astype(BF16), head_sum), 1e-24))
        k2 = k * (1.0 + (icl - 1.0) * pv_ref[3:4, sl])
        bonus_s[:, sl] = _dot((r * k2 * pv_ref[4:5, sl]).astype(BF16), head_sum) * v
        hi, mid, lo = _split3(lw)
        cum = _dot(tri, hi) + _dot(tri, mid) + _dot(tri, lo)
        b = kk * icl
        inv_decay = jnp.exp(-cum)
        to_end = []
        for c in range(tb // C):
            e_end = jnp.exp(cum[c * C + C - 1:c * C + C])
            gc_s[8 * c:8 * c + 1, sl] = e_end
            to_end.append(inv_decay[c * C:(c + 1) * C] * e_end)
        to_end = jnp.concatenate(to_end, axis=0)
        rt_s[:, sl] = (r * jnp.exp(cum)).astype(BF16)
        at_s[:, sl] = (-kk * jnp.exp(cum - lw)).astype(BF16)
        kh_s[:, sl] = (k2 * inv_decay).astype(BF16)
        bh_s[:, sl] = (b * inv_decay).astype(BF16)
        kb_s[:, sl] = (k2 * to_end).astype(BF16)
        bb_s[:, sl] = (b * to_end).astype(BF16)
        v_s[:, sl] = v.astype(BF16)

    hr = lax.broadcasted_iota(jnp.int32, (LANES, LANES), 0)
    hc = lax.broadcasted_iota(jnp.int32, (LANES, LANES), 1)
    same_head = (hr >> 6) == (hc >> 6)
    t_r = hr & (C - 1)
    t_c = hc & (C - 1)
    strict = same_head & (t_c < t_r)
    incl = same_head & (t_c <= t_r)
    eye = jnp.where(hr == hc, 1.0, 0.0)
    lane_head = lax.broadcasted_iota(jnp.int32, (C, LANES), 1) >> 6

    def stack(zc):
        return jnp.concatenate([jnp.where(lane_head == 0, zc, jnp.zeros_like(zc)),
                                jnp.where(lane_head == 1, zc, jnp.zeros_like(zc))], axis=0)

    def chunk(c, carry):
        t0 = pl.multiple_of(c * C, C)
        rows = pl.ds(t0, C)
        pairs = range(RWKV_PAIRS)
        sls = [slice(p * LANES, (p + 1) * LANES) for p in pairs]
        lhs = [jnp.concatenate([stack(at_s[rows, sl]), stack(rt_s[rows, sl])], axis=0) for sl in sls]
        rhs = [jnp.concatenate([stack(kh_s[rows, sl]), stack(bh_s[rows, sl])], axis=0) for sl in sls]
        aa = [_dot_nt(lhs[p], rhs[p]) for p in pairs]
        a_ak = [jnp.where(strict, aa[p][:2 * C, :2 * C], 0.0).astype(BF16) for p in pairs]
        n_pow = [jnp.where(strict, aa[p][:2 * C, 2 * C:], 0.0) for p in pairs]
        a_r = [jnp.concatenate([jnp.where(incl, aa[p][2 * C:, :2 * C], 0.0).astype(BF16),
                                jnp.where(incl, aa[p][2 * C:, 2 * C:], 0.0).astype(BF16)], axis=1) for p in pairs]
        t_inv = [eye + n_pow[p] for p in pairs]
        for _ in range(5):
            nb = [n_pow[p].astype(BF16) for p in pairs]
            n_pow = [_dot(nb[p], nb[p]) for p in pairs]
            t_inv = [t_inv[p] + _dot(t_inv[p].astype(BF16), n_pow[p].astype(BF16)) for p in pairs]
        state = [state_s[p] for p in pairs]
        xs = [_dot_nt(lhs[p], state[p].astype(BF16)) for p in pairs]
        v_c = [v_s[rows, sl] for sl in sls]
        v_st = [stack(v_c[p]) for p in pairs]
        av = [_dot(a_ak[p], v_st[p]) for p in pairs]
        sa = [_dot(t_inv[p].astype(BF16), (xs[p][:2 * C] + av[p]).astype(BF16)) for p in pairs]
        ys = [xs[p][2 * C:] + _dot(a_r[p], jnp.concatenate([v_st[p], sa[p].astype(BF16)], axis=0)) for p in pairs]
        upd = [_dot_tn(jnp.concatenate([v_c[p], (sa[p][:C] + sa[p][C:]).astype(BF16)], axis=0),
                       jnp.concatenate([kb_s[rows, sls[p]], bb_s[rows, sls[p]]], axis=0)) for p in pairs]
        for p in pairs:
            y_s[rows, sls[p]] = ys[p][:C] + ys[p][C:]
            state_s[p] = (state[p] * gc_s[pl.ds(pl.multiple_of(c * 8, 8), 1), sls[p]]
                          + jnp.where(same_head, upd[p], 0.0))
        return carry

    lax.fori_loop(0, tb // C, chunk, 0)

    for q in range(RWKV_PAIRS // 2):
        sl = slice(q * wide, (q + 1) * wide)
        y = y_s[:, sl]
        d = y - _dot(y.astype(BF16), head_mean)
        var = _dot((d * d).astype(BF16), head_mean)
        yn = d * lax.rsqrt(var + LNX_EPS) * pv_ref[5:6, sl] + pv_ref[6:7, sl]
        o_ref[:, sl] = (yn + bonus_s[:, sl]) * g_s[:, sl]


def _hi_lo(w):
    hi = w.astype(BF16)
    return hi, (w - hi.astype(F32)).astype(BF16)


def _rwkv(zr, mu_t, w0, w_up, a0, a_up, g_up, k_k, k_a, r_k, lnx_g, lnx_b, tb):
    T = zr.shape[1]
    n_in = RW_TILES - 1
    pv = jnp.stack([w0, a0, k_k, k_a, r_k.reshape(-1), lnx_g, lnx_b, jnp.zeros_like(w0)])
    wup = jnp.pad(w_up, ((0, LANES - W_RANK), (0, 0))).astype(BF16)
    aup = jnp.pad(a_up, ((W_RANK, LANES - W_RANK - A_RANK), (0, 0))).astype(BF16)
    gup = jnp.pad(g_up, ((0, 2 * LANES - G_RANK), (0, 0))).astype(BF16)
    full = lambda shape: pl.BlockSpec(shape, lambda s: (0,) * len(shape))
    kern = functools.partial(_rwkv_kernel, tb=tb)
    bf = lambda: pltpu.VMEM((tb, RWKV_WIDTH), BF16)
    ff = lambda: pltpu.VMEM((tb, RWKV_WIDTH), F32)
    return pl.pallas_call(
        kern,
        grid=(T // tb,),
        in_specs=[
            pl.BlockSpec((n_in, tb, LANES), lambda s: (0, s, 0)),
            pl.BlockSpec((n_in, 8, LANES), lambda s: (0, jnp.maximum(s * (tb // 8) - 1, 0), 0)),
            full((n_in, 1, LANES)),
            full((8, RWKV_WIDTH)),
            full((LANES, RWKV_WIDTH)), full((LANES, RWKV_WIDTH)), full((2 * LANES, RWKV_WIDTH)),
        ],
        out_specs=pl.BlockSpec((tb, RWKV_WIDTH), lambda s: (s, 0)),
        out_shape=jax.ShapeDtypeStruct((T, RWKV_WIDTH), F32),
        scratch_shapes=[bf(), bf(), bf(), bf(), bf(), bf(), bf(),
                        pltpu.VMEM((8 * (tb // RWKV_CHUNK), RWKV_WIDTH), F32), ff(), ff(), ff(),
                        pltpu.VMEM((RWKV_PAIRS, LANES, LANES), F32)],
        compiler_params=_cparams(("arbitrary",)),
        name="rwkv7",
    )(zr, zr, mu_t, pv, wup, aup, gup)


PROJ_COLS = 512


def _mix_out_kernel(ya_ref, yb_ref, g_ref, w_ref, x_ref, o_ref):
    h = jnp.concatenate([_rms(ya_ref[...], g_ref[...]).astype(BF16), yb_ref[...].astype(BF16)], axis=1)
    for n in range(D_MODEL // PROJ_COLS):
        sl = slice(n * PROJ_COLS, (n + 1) * PROJ_COLS)
        o_ref[:, sl] = x_ref[:, sl] + _dot(h, w_ref[:, sl])


def _mix_out(y_a, y_b, g, w, x, tm):
    T = x.shape[0]
    return pl.pallas_call(
        _mix_out_kernel,
        grid=(T // tm,),
        in_specs=[
            pl.BlockSpec((tm, NSA_WIDTH), lambda m: (m, 0)),
            pl.BlockSpec((tm, RWKV_WIDTH), lambda m: (m, 0)),
            _resident((1, NSA_WIDTH)),
            _resident((NSA_WIDTH + RWKV_WIDTH, D_MODEL)),
            pl.BlockSpec((tm, D_MODEL), lambda m: (m, 0)),
        ],
        out_specs=pl.BlockSpec((tm, D_MODEL), lambda m: (m, 0)),
        out_shape=jax.ShapeDtypeStruct((T, D_MODEL), F32),
        compiler_params=_cparams(("parallel",)),
        name="mix_out_proj",
    )(y_a, y_b, g, w, x)


def _norm_mm_kernel(x_ref, g_ref, w_ref, o_ref, h_scr):
    @pl.when(pl.program_id(1) == 0)
    def _():
        h_scr[...] = _rms(x_ref[...], g_ref[...]).astype(BF16)

    o_ref[...] = _dot(h_scr[...], w_ref[...]).astype(o_ref.dtype)


def _norm_mm(x, g, w, tm, tn, name):
    M, K = x.shape
    N = w.shape[1]
    return pl.pallas_call(
        _norm_mm_kernel,
        grid=(M // tm, N // tn),
        in_specs=[
            pl.BlockSpec((tm, K), lambda m, n: (m, 0)),
            pl.BlockSpec((1, K), lambda m, n: (0, 0)),
            pl.BlockSpec((K, tn), lambda m, n: (0, n)),
        ],
        out_specs=pl.BlockSpec((tm, tn), lambda m, n: (m, n)),
        out_shape=jax.ShapeDtypeStruct((M, N), BF16),
        scratch_shapes=[pltpu.VMEM((tm, K), BF16)],
        compiler_params=_cparams(("parallel", "arbitrary")),
        name=name,
    )(x, g, w)


def _mem_router_kernel(x_ref, km_ref, vm_ref, wq_ref, wo_ref, g2_ref, g3_ref, w_hi, w_lo, b_ref,
                       x2_ref, h_ref, c_ref, n_ref):
    scale = MEM_HEAD_DIM ** -0.5
    hq = _rms(x_ref[...], g2_ref[...]).astype(BF16)
    heads = []
    for hd in range(MEM_HEADS):
        sl = slice(hd * MEM_HEAD_DIM, (hd + 1) * MEM_HEAD_DIM)
        q = _dot(hq, wq_ref[:, sl]).astype(BF16)
        s = _dot_nt(q, km_ref[:, sl]) * scale
        e = jnp.exp(s - jnp.max(s, axis=-1, keepdims=True))
        p = e / jnp.sum(e, axis=-1, keepdims=True)
        heads.append(_dot(p.astype(BF16), vm_ref[:, sl]).astype(BF16))
    o = jnp.concatenate(heads, axis=1)
    for n in range(D_MODEL // PROJ_COLS):
        sl = slice(n * PROJ_COLS, (n + 1) * PROJ_COLS)
        x2_ref[:, sl] = x_ref[:, sl] + _dot(o, wo_ref[:, sl])
    _route(x2_ref[...], g3_ref, w_hi, w_lo, b_ref, h_ref, c_ref, n_ref)


def _mem_router(x, k_mem, v_mem, wq, wo, g2, g3, w_hi, w_lo, b, tm):
    T = x.shape[0]
    M = k_mem.shape[0]
    row_block = lambda width: pl.BlockSpec((tm, width), lambda m: (m, 0))
    return pl.pallas_call(
        _mem_router_kernel,
        grid=(T // tm,),
        in_specs=[
            row_block(D_MODEL),
            _resident((M, D_MODEL)), _resident((M, D_MODEL)),
            _resident((D_MODEL, D_MODEL)), _resident((D_MODEL, D_MODEL)),
            _resident((1, D_MODEL)), _resident((1, D_MODEL)),
            _resident((D_MODEL, LANES)), _resident((D_MODEL, LANES)), _resident((1, LANES)),
        ],
        out_specs=[row_block(D_MODEL), row_block(D_MODEL), row_block(LANES),
                   pl.BlockSpec((1, 8, LANES), lambda m: (m, 0, 0))],
        out_shape=[
            jax.ShapeDtypeStruct((T, D_MODEL), F32),
            jax.ShapeDtypeStruct((T, D_MODEL), BF16),
            jax.ShapeDtypeStruct((T, LANES), F32),
            jax.ShapeDtypeStruct((T // tm, 8, LANES), F32),
        ],
        compiler_params=_cparams(("parallel",)),
        name="mem_attn_router",
    )(x, k_mem, v_mem, wq, wo, g2, g3, w_hi, w_lo, b)


def _route(x, g_ref, w_hi, w_lo, b_ref, h_ref, c_ref, n_ref):
    h = _rms(x, g_ref[...])
    h_ref[...] = h.astype(BF16)
    logits = _dot_xw3(h, w_hi[...], w_lo[...]) + b_ref[...]
    lane = lax.broadcasted_iota(jnp.int32, logits.shape, 1)
    big = jnp.int32(LANES)
    is_grp = (lane >= N_EXPERTS) & (lane < N_EXPERTS + N_GROUPS)
    lg = jnp.where(is_grp, logits, NEG)
    eg = jnp.where(is_grp, jnp.exp(lg - jnp.max(lg, axis=-1, keepdims=True)), 0.0)
    pg = eg / jnp.sum(eg, axis=-1, keepdims=True)
    pg_top = jnp.max(pg, axis=-1, keepdims=True)
    g_idx = jnp.min(jnp.where(is_grp & (pg == pg_top), lane - N_EXPERTS, big), axis=-1, keepdims=True)
    in_grp = (lane < N_EXPERTS) & ((lane >> 3) == g_idx)
    le = jnp.where(in_grp, logits, NEG)
    ee = jnp.where(in_grp, jnp.exp(le - jnp.max(le, axis=-1, keepdims=True)), 0.0)
    pe = jnp.where(in_grp, ee / jnp.sum(ee, axis=-1, keepdims=True), -1.0)
    p1 = jnp.max(pe, axis=-1, keepdims=True)
    hit1 = lane == jnp.min(jnp.where(pe == p1, lane, big), axis=-1, keepdims=True)
    pe2 = jnp.where(hit1, -1.0, pe)
    p2 = jnp.max(pe2, axis=-1, keepdims=True)
    hit2 = lane == jnp.min(jnp.where(pe2 == p2, lane, big), axis=-1, keepdims=True)
    denom = p1 + p2
    comb = jnp.where(hit1, pg_top * p1 / denom, 0.0) + jnp.where(hit2, pg_top * p2 / denom, 0.0)
    c_ref[...] = jnp.where(lane == GROUP_LANE, g_idx.astype(F32), comb)
    in_group = jnp.where(lane == g_idx, 1.0, 0.0)
    n_ref[0] = jnp.broadcast_to(jnp.sum(in_group, axis=0, keepdims=True), n_ref.shape[1:])


def _moe_kernel(meta_ref, h_ref, c_ref, wg_ref, wu_ref, wd_ref, y_ref, hs, cs, ys, slot_scr, perm, *, tm, sub):
    m = pl.program_id(0)
    e = pl.program_id(1)
    nslot = hs.shape[0]

    @pl.when(e == 0)
    def _():
        c = c_ref[...]
        lane = lax.broadcasted_iota(jnp.int32, c.shape, 1)
        one_hot = jnp.where(lane.astype(F32) == c[:, GROUP_LANE:GROUP_LANE + 1], 1.0, 0.0)
        earlier = jnp.where(lax.broadcasted_iota(jnp.int32, (tm, tm), 1)
                            < lax.broadcasted_iota(jnp.int32, (tm, tm), 0), 1.0, 0.0).astype(BF16)
        rank = _dot(earlier, one_hot.astype(BF16))
        start = jnp.zeros((1, LANES), F32)
        for g in range(N_GROUPS):
            start = jnp.where(lane[0:1] == g, (meta_ref[m * 2 * N_GROUPS + g] * sub).astype(F32), start)
        slot = jnp.sum(one_hot * (rank + start), axis=-1, keepdims=True)
        slot_scr[...] = jnp.broadcast_to(slot, slot_scr.shape)
        slot_row = slot_scr[...].T[0:1, :]
        perm[...] = jnp.where(lax.broadcasted_iota(jnp.int32, (nslot, tm), 0).astype(F32) == slot_row,
                              1.0, 0.0).astype(BF16)
        hs[...] = _dot(perm[...], h_ref[...]).astype(BF16)
        w_hi, w_lo = _split2(jnp.where(lane < N_EXPERTS, c, 0.0))
        cs[...] = _dot(perm[...], w_hi) + _dot(perm[...], w_lo)
        ys[...] = jnp.zeros(ys.shape, F32)

    grp = e // EXPERTS_PER_GROUP
    first = meta_ref[m * 2 * N_GROUPS + grp]
    count = meta_ref[m * 2 * N_GROUPS + N_GROUPS + grp]
    def ffn(start_block, n_blocks):
        n = n_blocks * sub
        rows = pl.ds(pl.multiple_of(start_block * sub, sub), n)
        x = hs[rows, :]
        hid = jax.nn.silu(_dot(x, wg_ref[0])) * _dot(x, wu_ref[0])
        lane_s = lax.broadcasted_iota(jnp.int32, (n, LANES), 1)
        c_e = jnp.sum(jnp.where(lane_s == e, cs[rows, :], 0.0), axis=-1, keepdims=True)
        ys[rows, :] += _dot((hid * c_e).astype(BF16), wd_ref[0])

    def pair(s, carry):
        ffn(first + 2 * s, 2)
        return carry

    lax.fori_loop(0, count // 2, pair, 0)

    @pl.when(count % 2 == 1)
    def _():
        ffn(first + count - 1, 1)

    @pl.when(e == pl.num_programs(1) - 1)
    def _():
        back = jnp.where(lax.broadcasted_iota(jnp.int32, (tm, nslot), 1).astype(F32) == slot_scr[:, 0:1],
                         1.0, 0.0).astype(BF16)
        for n in range(D_MODEL // PROJ_COLS):
            sl = slice(n * PROJ_COLS, (n + 1) * PROJ_COLS)
            y_ref[:, sl] = _dot(back, ys[:, sl].astype(BF16)).astype(y_ref.dtype)


def _moe(h, comb, counts, wg, wu, wd, tm, sub):
    T = h.shape[0]
    nt = T // tm
    nslot = tm + N_GROUPS * sub
    cnt = counts[:, 0, :N_GROUPS].astype(jnp.int32).reshape(nt, -1, N_GROUPS).sum(axis=1)
    nblk = (cnt + sub - 1) // sub
    first = jnp.cumsum(nblk, axis=1) - nblk
    meta = jnp.concatenate([first, nblk], axis=1).reshape(-1)
    once = lambda shape: pl.BlockSpec(shape, lambda m, e, meta: (m, 0), pipeline_mode=pl.Buffered(1))
    kern = functools.partial(_moe_kernel, tm=tm, sub=sub)
    return pl.pallas_call(
        kern,
        grid_spec=pltpu.PrefetchScalarGridSpec(
            num_scalar_prefetch=1,
            grid=(nt, N_EXPERTS),
            in_specs=[
                once((tm, D_MODEL)),
                once((tm, LANES)),
                pl.BlockSpec((1, D_MODEL, EXPERT_FF), lambda m, e, meta: (e, 0, 0)),
                pl.BlockSpec((1, D_MODEL, EXPERT_FF), lambda m, e, meta: (e, 0, 0)),
                pl.BlockSpec((1, EXPERT_FF, D_MODEL), lambda m, e, meta: (e, 0, 0)),
            ],
            out_specs=pl.BlockSpec((tm, D_MODEL), lambda m, e, meta: (m, 0)),
            scratch_shapes=[
                pltpu.VMEM((nslot, D_MODEL), BF16),
                pltpu.VMEM((nslot, LANES), F32),
                pltpu.VMEM((nslot, D_MODEL), F32),
                pltpu.VMEM((tm, LANES), F32),
                pltpu.VMEM((nslot, tm), BF16),
            ],
        ),
        out_shape=jax.ShapeDtypeStruct((T, D_MODEL), BF16),
        compiler_params=_cparams(("parallel", "arbitrary")),
        name="moe_experts",
    )(meta, h, comb, wg, wu, wd)


def _final_norm_kernel(x_ref, y_ref, g_ref, o_ref):
    o_ref[...] = _rms(x_ref[...] + y_ref[...].astype(F32), g_ref[...])


def _final_norm(x, y, g, tm):
    T = x.shape[0]
    blk = pl.BlockSpec((tm, D_MODEL), lambda m: (m, 0))
    return pl.pallas_call(
        _final_norm_kernel,
        grid=(T // tm,),
        in_specs=[blk, blk, _resident((1, D_MODEL))],
        out_specs=blk,
        out_shape=jax.ShapeDtypeStruct((T, D_MODEL), F32),
        compiler_params=_cparams(("parallel",)),
        name="moe_residual_norm",
    )(x, y, g)


def _pack_w_in(w_in, rwkv_mu):
    nsa_cols = NSA_WIDTH + 6 * NSA_KV_WIDTH
    gl = w_in[:, nsa_cols:nsa_cols + 3 * NSA_HEADS]
    rw0 = nsa_cols + 3 * NSA_HEADS
    rkv = w_in[:, rw0:rw0 + 3 * RWKV_WIDTH + W_RANK + A_RANK]
    gd = w_in[:, rw0 + 3 * RWKV_WIDTH + W_RANK + A_RANK:]

    def pad(a, n):
        return jnp.pad(a, ((0, 0), (0, n - a.shape[1])))

    w = jnp.concatenate([w_in[:, :nsa_cols], rkv, pad(gd, 2 * LANES), pad(gl, LANES)], axis=1)
    mu = rwkv_mu.reshape(1, -1)
    mu_rkv = mu[:, :3 * RWKV_WIDTH + W_RANK + A_RANK]
    mu_gd = pad(mu[:, 3 * RWKV_WIDTH + W_RANK + A_RANK:], 2 * LANES)
    mu_t = jnp.concatenate([mu_rkv, mu_gd], axis=1).reshape(RW_TILES - 1, 1, LANES)
    return w.astype(BF16), mu_t


def kernel(x, mem, ln1_g, w_in, cmp_pos_k, cmp_w1_k, cmp_w2_k, cmp_pos_v, cmp_w1_v, cmp_w2_v, nsa_norm_g, rwkv_mu, rwkv_w0, rwkv_w_up, rwkv_a0, rwkv_a_up, rwkv_g_up, rwkv_k_k, rwkv_k_a, rwkv_r_k, rwkv_lnx_g, rwkv_lnx_b, w_out, ln_mem_g, ln2_g, wq_mem, wk_mem, wv_mem, wo_mem, ln3_g, router_group_w, router_group_b, router_expert_w, router_expert_b, moe_w_gate, moe_w_up, moe_w_down, lnf_g):
    B, T, _ = x.shape
    assert B == 1 and T % max(TM_PROJ, TM_MEM, TQ_SEL, TK_SEL, TB_RWKV, TM_MOE) == 0
    assert w_in.shape[0] == 1
    row = lambda a: a.reshape(1, -1)
    xs = x[0]
    for l in range(w_in.shape[0]):
        wp, mu_t = _pack_w_in(w_in[l], rwkv_mu[l])
        zn, zr, zc = _in_proj(xs, row(ln1_g[l]), wp, TM_PROJ)
        y_a = _nsa(zn, zr, zc, cmp_pos_k[l], cmp_w1_k[l], cmp_w2_k[l], cmp_pos_v[l], cmp_w1_v[l], cmp_w2_v[l],
                   TQ_NSA, TQ_SEL, TK_SEL)
        y_b = _rwkv(zr, mu_t, rwkv_w0[l], rwkv_w_up[l], rwkv_a0[l], rwkv_a_up[l], rwkv_g_up[l], rwkv_k_k[l],
                    rwkv_k_a[l], rwkv_r_k[l], rwkv_lnx_g[l], rwkv_lnx_b[l], TB_RWKV)
        xs = _mix_out(y_a, y_b, row(nsa_norm_g[l]), w_out[l].astype(BF16), xs, TM_MEM)
        m_tok = mem[0]
        k_mem = _norm_mm(m_tok, row(ln_mem_g[l]), wk_mem[l].astype(BF16), m_tok.shape[0], PROJ_COLS, "mem_k")
        v_mem = _norm_mm(m_tok, row(ln_mem_g[l]), wv_mem[l].astype(BF16), m_tok.shape[0], PROJ_COLS, "mem_v")
        w_r = jnp.pad(jnp.concatenate([router_expert_w[l], router_group_w[l]], axis=1),
                      ((0, 0), (0, LANES - N_EXPERTS - N_GROUPS)))
        b_r = jnp.pad(jnp.concatenate([router_expert_b[l], router_group_b[l]]), (0, LANES - N_EXPERTS - N_GROUPS))
        xs, h3, comb, counts = _mem_router(xs, k_mem, v_mem, wq_mem[l].astype(BF16), wo_mem[l].astype(BF16),
                                   row(ln2_g[l]), row(ln3_g[l]), *_hi_lo(w_r), row(b_r), TM_MEM)
        y_moe = _moe(h3, comb, counts, moe_w_gate[l].astype(BF16), moe_w_up[l].astype(BF16),
                     moe_w_down[l].astype(BF16), TM_MOE, SUB_MOE)
        xs = _final_norm(xs, y_moe, row(lnf_g), TM_PROJ)
    return xs[None]
```

```python
import functools

import numpy as np
import jax
import jax.numpy as jnp
from jax import lax
from jax.experimental import pallas as pl
from jax.experimental.pallas import tpu as pltpu

F32 = jnp.float32
BF16 = jnp.bfloat16

LANES = 128
D_MODEL = 2048
EPS = 1e-6
NSA_HEAD_DIM = 128
NSA_HEADS = 8
NSA_KV_GROUPS = 2
NSA_HPG = NSA_HEADS // NSA_KV_GROUPS
NSA_WIDTH = NSA_HEADS * NSA_HEAD_DIM
NSA_KV_WIDTH = NSA_KV_GROUPS * NSA_HEAD_DIM
CMP_BLOCK = 32
CMP_STRIDE = 16
SLC_BLOCK = 64
SLC_TOPK = 16
WINDOW = 512
NEG = -1e30
RWKV_HEAD_DIM = 64
RWKV_HEADS = 16
RWKV_WIDTH = RWKV_HEADS * RWKV_HEAD_DIM
RWKV_PAIRS = RWKV_WIDTH // LANES
W_RANK = 64
A_RANK = 64
G_RANK = 160
LNX_EPS = 64e-5
RWKV_CHUNK = 64
MEM_HEADS = 4
MEM_HEAD_DIM = D_MODEL // MEM_HEADS
N_GROUPS = 4
EXPERTS_PER_GROUP = 8
N_EXPERTS = N_GROUPS * EXPERTS_PER_GROUP
EXPERT_FF = 256

NSA_TILES = 20
RW_TILES = 28
IN_TILE_BLOCK = 4

VMEM_LIMIT = 56 * 1024 * 1024

TM_PROJ = 256
TM_MEM = 512
TQ_NSA = 256
TQ_SEL = 512
TK_SEL = 512
TB_RWKV = 256
TM_MOE = 1024
SUB_MOE = 64
GROUP_LANE = 64


def _cparams(sem):
    return pltpu.CompilerParams(dimension_semantics=sem, vmem_limit_bytes=VMEM_LIMIT)


def _dot(a, b):
    return jnp.dot(a, b, preferred_element_type=F32)


def _dot_nt(a, b):
    return lax.dot_general(a, b, (((1,), (1,)), ((), ())), preferred_element_type=F32)


def _dot_tn(a, b):
    return lax.dot_general(a, b, (((0,), (0,)), ((), ())), preferred_element_type=F32)


def _split2(x):
    hi = x.astype(BF16)
    lo = (x - hi.astype(F32)).astype(BF16)
    return hi, lo


def _split3(x):
    hi = x.astype(BF16)
    r1 = x - hi.astype(F32)
    mid = r1.astype(BF16)
    lo = (r1 - mid.astype(F32)).astype(BF16)
    return hi, mid, lo


def _dot_x2(x, w):
    hi, lo = _split2(x)
    return _dot(hi, w) + _dot(lo, w)


def _dot_x3(x, w):
    hi, mid, lo = _split3(x)
    return _dot(hi, w) + _dot(mid, w) + _dot(lo, w)


def _dot_xw3(x, w_hi, w_lo):
    hi, lo = _split2(x)
    return _dot(hi, w_hi) + _dot(lo, w_hi) + _dot(hi, w_lo)


def _rms(x, g):
    ms = jnp.mean(x * x, axis=-1, keepdims=True)
    return x * lax.rsqrt(ms + EPS) * g


def _resident(shape):
    return pl.BlockSpec(shape, lambda *_: (0,) * len(shape), pipeline_mode=pl.Buffered(1))


CMP_TILES = 2 * NSA_KV_GROUPS


def _in_proj_kernel(x_ref, g_ref, w_ref, zn_ref, zr_ref, zc_ref, *, q_scale):
    h = _rms(x_ref[...], g_ref[...]).astype(BF16)
    nb = IN_TILE_BLOCK * LANES
    for n in range((NSA_TILES + RW_TILES) // IN_TILE_BLOCK):
        z = _dot(h, w_ref[:, n * nb:(n + 1) * nb])
        for t in range(IN_TILE_BLOCK):
            tile = n * IN_TILE_BLOCK + t
            zt = z[:, t * LANES:(t + 1) * LANES]
            if tile < NSA_HEADS:
                zn_ref[tile] = (zt * q_scale).astype(BF16)
            elif tile < NSA_TILES:
                zn_ref[tile] = zt.astype(BF16)
                if tile < NSA_HEADS + CMP_TILES:
                    zc_ref[tile - NSA_HEADS] = zt
            else:
                zr_ref[tile - NSA_TILES] = zt


def _in_proj(x, g, w, tm):
    T = x.shape[0]
    kern = functools.partial(_in_proj_kernel, q_scale=NSA_HEAD_DIM ** -0.5)
    return pl.pallas_call(
        kern,
        grid=(T // tm,),
        in_specs=[
            pl.BlockSpec((tm, D_MODEL), lambda m: (m, 0)),
            _resident((1, D_MODEL)),
            _resident((D_MODEL, (NSA_TILES + RW_TILES) * LANES)),
        ],
        out_specs=[
            pl.BlockSpec((NSA_TILES, tm, LANES), lambda m: (0, m, 0)),
            pl.BlockSpec((RW_TILES, tm, LANES), lambda m: (0, m, 0)),
            pl.BlockSpec((CMP_TILES, tm, LANES), lambda m: (0, m, 0)),
        ],
        out_shape=[
            jax.ShapeDtypeStruct((NSA_TILES, T, LANES), BF16),
            jax.ShapeDtypeStruct((RW_TILES, T, LANES), F32),
            jax.ShapeDtypeStruct((CMP_TILES, T, LANES), F32),
        ],
        compiler_params=_cparams(("parallel",)),
        name="in_proj",
    )(x, g, w)


def _compress_kernel(x_ref, pos_ref, w1_ref, w2_ref, o_ref):
    n = x_ref.shape[1] // CMP_STRIDE
    w1 = w1_ref[0]
    h_first = jnp.zeros((n, NSA_HEAD_DIM), F32)
    h_second = jnp.zeros((n, NSA_HEAD_DIM), F32)
    for t in range(CMP_STRIDE):
        x_t = x_ref[0, pl.ds(t, n, stride=CMP_STRIDE), :].astype(BF16)
        h_first += _dot(x_t, w1[t * NSA_HEAD_DIM:(t + 1) * NSA_HEAD_DIM])
        h_second += _dot(x_t, w1[(CMP_STRIDE + t) * NSA_HEAD_DIM:(CMP_STRIDE + t + 1) * NSA_HEAD_DIM])
    h_next = pltpu.roll(h_second, n - 1, axis=0)
    bias = _dot(pos_ref[0], w1)[0:1]
    hid = jax.nn.gelu(h_first + h_next + bias)
    o_ref[0] = _dot(hid.astype(BF16), w2_ref[0]).astype(o_ref.dtype)


def _compress(zc, pos, w1, w2):
    T = zc.shape[1]
    n = T // CMP_STRIDE
    width = CMP_BLOCK * NSA_HEAD_DIM
    return pl.pallas_call(
        _compress_kernel,
        grid=(4,),
        in_specs=[
            pl.BlockSpec((1, T, NSA_HEAD_DIM), lambda s: (s, 0, 0)),
            pl.BlockSpec((1, 8, width), lambda s: (s // 2, 0, 0)),
            pl.BlockSpec((1, width, NSA_HEAD_DIM), lambda s: (s // 2, 0, 0)),
            pl.BlockSpec((1, NSA_HEAD_DIM, NSA_HEAD_DIM), lambda s: (s // 2, 0, 0)),
        ],
        out_specs=pl.BlockSpec((1, n, NSA_HEAD_DIM), lambda s: (s, 0, 0)),
        out_shape=jax.ShapeDtypeStruct((4, n, NSA_HEAD_DIM), BF16),
        compiler_params=_cparams(("parallel",)),
        name="nsa_compress",
    )(zc, pos, w1, w2)


def _cmp_win_kernel(q_ref, kc_ref, lhs_ref, kw_ref, vwt_ref, oc_ref, ow_ref, sel_ref, s_scr, acc_scr,
                    *, tq, top_n, ch):
    q0 = pl.program_id(1) * tq
    cols = NSA_HPG * tq
    ncp = kc_ref.shape[1]
    ns = lhs_ref.shape[1] - NSA_HEAD_DIM - SEL_ONES_ROWS
    q_t = jnp.concatenate([q_ref[h].astype(F32).T.astype(BF16) for h in range(NSA_HPG)], axis=1)
    qpos = q0 + (lax.broadcasted_iota(jnp.int32, (1, cols), 1) & (tq - 1))

    def store_heads(o_ref, o_t):
        for h in range(NSA_HPG):
            o_ref[:, h * NSA_HEAD_DIM:(h + 1) * NSA_HEAD_DIM] = o_t[:, h * tq:(h + 1) * tq].T

    span = WINDOW + tq
    start = pl.multiple_of(jnp.maximum(q0 - WINDOW, 0), tq)
    kpos = start + lax.broadcasted_iota(jnp.int32, (span, 1), 0)
    in_window = (qpos - kpos).astype(jnp.uint32) < jnp.uint32(WINDOW)
    s_w = jnp.where(in_window, _dot(kw_ref[0, pl.ds(start, span), :], q_t), NEG)
    e_w = jnp.exp(s_w - jnp.max(s_w, axis=0, keepdims=True))
    p_w = e_w / jnp.sum(e_w, axis=0, keepdims=True)
    store_heads(ow_ref, _dot(vwt_ref[0, :, pl.ds(start, span)], p_w.astype(BF16)))

    visible = (q0 + tq - 1 - (CMP_BLOCK - 1)) // CMP_STRIDE + 1
    clean = jnp.maximum((q0 - (CMP_BLOCK - 1)) // CMP_STRIDE + 1, 0) // ch
    n_chunks = (visible + ch - 1) // ch

    def chunk_rows(c):
        return pl.ds(pl.multiple_of(c * ch, ch), ch)

    def score_chunk(c, m, masked):
        s = _dot(kc_ref[0, chunk_rows(c), :], q_t)
        if masked:
            cmp_end = (c * ch + lax.broadcasted_iota(jnp.int32, (ch, 1), 0)) * CMP_STRIDE + (CMP_BLOCK - 1)
            s = jnp.where(cmp_end <= qpos, s, NEG)
        s_scr[chunk_rows(c), :] = s
        return jnp.maximum(m, jnp.max(s, axis=0, keepdims=True))

    m = jnp.full((1, cols), NEG, F32)
    m = lax.fori_loop(0, clean, lambda c, m: score_chunk(c, m, False), m)
    m = lax.fori_loop(clean, n_chunks, lambda c, m: score_chunk(c, m, True), m)
    acc_scr[...] = jnp.zeros(acc_scr.shape, F32)

    def weigh_chunk(c, carry):
        e = jnp.exp(s_scr[chunk_rows(c), :] - m).astype(BF16)
        acc_scr[...] += _dot(lhs_ref[0, :, chunk_rows(c)], e)
        return carry

    lax.fori_loop(0, n_chunks, weigh_chunk, 0)
    acc = acc_scr[...]
    sees_any = jnp.where(qpos >= CMP_BLOCK - 1, 1.0, 0.0)
    r_inv = sees_any / jnp.maximum(acc[NSA_HEAD_DIM + ns:NSA_HEAD_DIM + ns + 1], 1e-30)
    store_heads(oc_ref, acc[:NSA_HEAD_DIM] * r_inv)
    weighted = acc[NSA_HEAD_DIM:NSA_HEAD_DIM + ns] * r_inv
    p_sel = weighted[:, 0:tq]
    for h in range(1, NSA_HPG):
        p_sel = p_sel + weighted[:, h * tq:(h + 1) * tq]
    blk = (q0 + lax.broadcasted_iota(jnp.int32, (1, tq), 1)) >> 6
    j = lax.broadcasted_iota(jnp.int32, (ns, tq), 0)
    future = j > blk
    forced = (j == 0) | (j == blk) | (j == blk - 1)
    score = jnp.where(future, -1.0, jnp.where(forced, 1e6, p_sel))
    sel = jnp.zeros((ns, tq), F32)
    for _ in range(top_n):
        best = jnp.max(score, axis=0, keepdims=True)
        idx = jnp.min(jnp.where(score == best, j, ns), axis=0, keepdims=True)
        hit = j == idx
        sel = jnp.where(hit, jnp.where(best >= 0.0, 1.0, 0.0), sel)
        score = jnp.where(hit, -2.0, score)
    sel_ref[0] = sel.astype(sel_ref.dtype)


CMP_CHUNK = 256


def _cmp_win(zn, kv_cmp, cmp_to_sel, vw_t, tq):
    T = zn.shape[1]
    ncp = kv_cmp.shape[1]
    ns = cmp_to_sel.shape[0]
    ch = min(CMP_CHUNK, ncp)
    assert ncp % ch == 0
    kw_tile0 = NSA_HEADS + 4 * NSA_KV_GROUPS
    lhs = jnp.concatenate([jnp.swapaxes(kv_cmp[NSA_KV_GROUPS:], 1, 2),
                           jnp.broadcast_to(cmp_to_sel, (NSA_KV_GROUPS, ns, ncp)),
                           jnp.ones((NSA_KV_GROUPS, SEL_ONES_ROWS, ncp), BF16)], axis=1)
    rows = lhs.shape[1]
    cols = NSA_HPG * tq
    kern = functools.partial(_cmp_win_kernel, tq=tq, top_n=min(SLC_TOPK, ns), ch=ch)
    wide = pl.BlockSpec((tq, NSA_HPG * NSA_HEAD_DIM), lambda g, i: (i, g))
    return pl.pallas_call(
        kern,
        grid=(NSA_KV_GROUPS, T // tq),
        in_specs=[
            pl.BlockSpec((NSA_HPG, tq, NSA_HEAD_DIM), lambda g, i: (g, i, 0)),
            pl.BlockSpec((1, ncp, NSA_HEAD_DIM), lambda g, i: (g, 0, 0)),
            pl.BlockSpec((1, rows, ncp), lambda g, i: (g, 0, 0)),
            pl.BlockSpec((1, T, NSA_HEAD_DIM), lambda g, i: (kw_tile0 + g, 0, 0)),
            pl.BlockSpec((1, NSA_HEAD_DIM, T), lambda g, i: (g, 0, 0)),
        ],
        out_specs=[wide, wide, pl.BlockSpec((1, ns, tq), lambda g, i: (g, 0, i))],
        out_shape=[
            jax.ShapeDtypeStruct((T, NSA_WIDTH), F32),
            jax.ShapeDtypeStruct((T, NSA_WIDTH), F32),
            jax.ShapeDtypeStruct((NSA_KV_GROUPS, ns, T), BF16),
        ],
        scratch_shapes=[pltpu.VMEM((ncp, cols), F32), pltpu.VMEM((rows, cols), F32)],
        compiler_params=_cparams(("parallel", "arbitrary")),
        name="nsa_cmp_window",
    )(zn, kv_cmp, lhs, zn, vw_t)


SEL_ONES_ROWS = 16


SEL_BUFFERS = 2


def _sel_attn_kernel(q_ref, k_ref, vt_ref, sel_ref, gl_ref, oc_ref, ow_ref, o_ref,
                     bias_scr, qa_scr, acc_scr, m_scr, *bufs, tq, tk, wb):
    s_bufs, p_bufs = bufs[:SEL_BUFFERS], bufs[SEL_BUFFERS:]
    q0 = pl.program_id(1) * tq
    cols = NSA_HPG * tq
    tpw = wb * SLC_BLOCK // tk
    last_tile = k_ref.shape[1] // tk - 1
    ns = sel_ref.shape[1]
    first_own = q0 // SLC_BLOCK
    not_chosen = (sel_ref[0].astype(F32) - 1.0) * (-NEG)
    for h in range(NSA_HPG):
        bias_scr[:, h * tq:(h + 1) * tq] = not_chosen
        qa_scr[:NSA_HEAD_DIM, h * tq:(h + 1) * tq] = q_ref[h].astype(F32).T.astype(BF16)

    def set_window(w):
        rows = pl.ds(pl.multiple_of(w * wb, wb), wb)
        before = w * wb + lax.broadcasted_iota(jnp.int32, (wb, 1), 0) < first_own
        qa_scr[NSA_HEAD_DIM:, :] = jnp.where(before, bias_scr[rows, :], NEG).astype(BF16)

    def scores(j):
        j = jnp.minimum(j, last_tile)
        return _dot(k_ref[0, pl.ds(pl.multiple_of(j * tk, tk), tk), :], qa_scr[...])

    def pv(j, p_ref):
        return _dot(vt_ref[0, :, pl.ds(pl.multiple_of(j * tk, tk), tk)], p_ref[...])

    def score_into(i, j):
        s = scores(j)
        s_bufs[i][...] = s
        m_scr[8 * (i + 1):8 * (i + 1) + 1, :] = jnp.max(s, axis=0, keepdims=True)

    def softmax_tile(i):
        m_old = m_scr[0:1, :]
        m_new = jnp.maximum(m_old, m_scr[8 * (i + 1):8 * (i + 1) + 1, :])
        m_scr[0:1, :] = m_new
        p_bufs[i][...] = jnp.exp((s_bufs[i][...] - m_new).astype(BF16))
        return jnp.exp(m_old - m_new)

    own = pl.ds(pl.multiple_of(q0, tq), tq)
    s = _dot(k_ref[0, own, :NSA_HEAD_DIM], qa_scr[:NSA_HEAD_DIM, :])
    s = jnp.concatenate([s[b * SLC_BLOCK:(b + 1) * SLC_BLOCK] + bias_scr[pl.ds(first_own + b, 1), :]
                         for b in range(tq // SLC_BLOCK)], axis=0)
    kpos = lax.broadcasted_iota(jnp.int32, (tq, 1), 0)
    qpos = lax.broadcasted_iota(jnp.int32, (1, cols), 1) & (tq - 1)
    s = jnp.where(kpos <= qpos, s, NEG)
    m_first = jnp.max(s, axis=0, keepdims=True)
    m_scr[0:1, :] = m_first
    acc_scr[...] = _dot(vt_ref[0, :, own], jnp.exp((s - m_first).astype(BF16)))

    n_tiles = (q0 + tk - 1) // tk

    nbuf = len(s_bufs)

    def window(w, carry):
        lo = w * tpw
        cnt = jnp.minimum(n_tiles - lo, tpw)
        set_window(w)
        score_into(0, lo)
        p_bufs[nbuf - 1][...] = jnp.zeros(p_bufs[nbuf - 1].shape, BF16)

        def group(t, alpha):
            first = lo + nbuf * t
            for i in range(nbuf):
                j = first + i
                score_into((i + 1) % nbuf, j + 1)
                acc_scr[...] = alpha * acc_scr[...] + pv(jnp.maximum(j - 1, 0), p_bufs[(i - 1) % nbuf])
                alpha = softmax_tile(i)
            return alpha

        groups = (cnt + nbuf - 1) // nbuf
        alpha_last = lax.fori_loop(0, groups, group, jnp.ones((1, cols), F32))
        acc_scr[...] = alpha_last * acc_scr[...] + pv(lo + nbuf * groups - 1, p_bufs[nbuf - 1])
        return carry

    lax.fori_loop(0, (n_tiles + tpw - 1) // tpw, window, 0)
    acc = acc_scr[...]
    o_t = acc[:NSA_HEAD_DIM] / acc[NSA_HEAD_DIM:NSA_HEAD_DIM + 1]
    gates = jax.nn.sigmoid(gl_ref[0])
    lane = lax.broadcasted_iota(jnp.int32, gates.shape, 1)
    for h in range(NSA_HPG):
        col = pl.program_id(0) * NSA_HPG + h

        def gate(branch):
            return jnp.sum(jnp.where(lane == branch * NSA_HEADS + col, gates, 0.0), axis=-1, keepdims=True)

        sl = slice(h * NSA_HEAD_DIM, (h + 1) * NSA_HEAD_DIM)
        o_ref[:, sl] = gate(0) * oc_ref[:, sl] + gate(1) * o_t[:, h * tq:(h + 1) * tq].T + gate(2) * ow_ref[:, sl]


def _sel_attn(zn, zr, k_aug, vt_aug, sel, o_c, o_w, tq, tk):
    T = zn.shape[1]
    ns = sel.shape[1]
    wb = k_aug.shape[2] - NSA_HEAD_DIM
    assert tk % tq == 0 and (wb * SLC_BLOCK) % (SEL_BUFFERS * tk) == 0 and ns % wb == 0 and T % tk == 0
    kern = functools.partial(_sel_attn_kernel, tq=tq, tk=tk, wb=wb)
    cols = NSA_HPG * tq
    wide = pl.BlockSpec((tq, NSA_HPG * NSA_HEAD_DIM), lambda g, i: (i, g))
    return pl.pallas_call(
        kern,
        grid=(NSA_KV_GROUPS, T // tq),
        in_specs=[
            pl.BlockSpec((NSA_HPG, tq, NSA_HEAD_DIM), lambda g, i: (g, i, 0)),
            pl.BlockSpec((1, T, NSA_HEAD_DIM + wb), lambda g, i: (g, 0, 0)),
            pl.BlockSpec((1, NSA_HEAD_DIM + SEL_ONES_ROWS, T), lambda g, i: (g, 0, 0)),
            pl.BlockSpec((1, ns, tq), lambda g, i: (g, 0, i)),
            pl.BlockSpec((1, tq, LANES), lambda g, i: (RW_TILES - 1, i, 0)),
            wide, wide,
        ],
        out_specs=wide,
        out_shape=jax.ShapeDtypeStruct((T, NSA_WIDTH), F32),
        scratch_shapes=[
            pltpu.VMEM((ns, cols), F32),
            pltpu.VMEM((NSA_HEAD_DIM + wb, cols), BF16),
            pltpu.VMEM((NSA_HEAD_DIM + SEL_ONES_ROWS, cols), F32),
            pltpu.VMEM((8 * (1 + SEL_BUFFERS), cols), F32),
        ] + [pltpu.VMEM((tk, cols), F32)] * SEL_BUFFERS + [pltpu.VMEM((tk, cols), BF16)] * SEL_BUFFERS,
        compiler_params=_cparams(("parallel", "arbitrary")),
        name="nsa_sel_attn",
    )(zn, k_aug, vt_aug, sel, zr, o_c, o_w)


def _cmp_to_sel_matrix(ncp, ns):
    cmp_start = np.arange(ncp)[:, None] * CMP_STRIDE
    sel_start = np.arange(ns)[None, :] * SLC_BLOCK
    overlap = np.minimum(cmp_start + CMP_BLOCK, sel_start + SLC_BLOCK) - np.maximum(cmp_start, sel_start)
    return jnp.asarray(np.clip(overlap, 0, None).astype(np.float32).T / CMP_BLOCK, dtype=BF16)


def _nsa(zn, zr, zc, cmp_pos_k, cmp_w1_k, cmp_w2_k, cmp_pos_v, cmp_w1_v, cmp_w2_v, tq, tq_sel, tk,
         sel_window=LANES):
    T = zn.shape[1]
    width = CMP_BLOCK * NSA_HEAD_DIM
    pos = jnp.stack([cmp_pos_k.reshape(1, width), cmp_pos_v.reshape(1, width)])
    pos = jnp.broadcast_to(pos, (2, 8, width)).astype(BF16)
    w1 = jnp.stack([cmp_w1_k, cmp_w1_v]).astype(BF16)
    w2 = jnp.stack([cmp_w2_k, cmp_w2_v]).astype(BF16)
    kv_cmp = _compress(zc, pos, w1, w2)
    ns = T // SLC_BLOCK
    ks0 = NSA_HEADS + 2 * NSA_KV_GROUPS
    vs0 = ks0 + NSA_KV_GROUPS
    vw0 = vs0 + 2 * NSA_KV_GROUPS
    vw_t = jnp.swapaxes(zn[vw0:vw0 + NSA_KV_GROUPS], 1, 2)
    o_c, o_w, sel = _cmp_win(zn, kv_cmp, _cmp_to_sel_matrix(T // CMP_STRIDE, ns), vw_t, tq)
    wb = min(sel_window, ns)
    blk_onehot = (jnp.arange(T)[:, None] // SLC_BLOCK % wb == jnp.arange(wb)[None, :]).astype(BF16)
    k_aug = jnp.concatenate([zn[ks0:vs0], jnp.broadcast_to(blk_onehot, (NSA_KV_GROUPS, T, wb))], axis=2)
    vt_aug = jnp.concatenate([jnp.swapaxes(zn[vs0:vs0 + NSA_KV_GROUPS], 1, 2),
                              jnp.ones((NSA_KV_GROUPS, SEL_ONES_ROWS, T), BF16)], axis=1)
    return _sel_attn(zn, zr, k_aug, vt_aug, sel, o_c, o_w, tq_sel, tk)


def _softplus(y):
    return jnp.maximum(y, 0.0) + jnp.log(1.0 + jnp.exp(-jnp.abs(y)))


def _rwkv_kernel(z_ref, prev_ref, mu_ref, pv_ref, wup_ref, aup_ref, gup_ref, o_ref,
                 rt_s, at_s, kh_s, bh_s, kb_s, bb_s, v_s, gc_s, g_s, bonus_s, y_s, state_s, *, tb):
    C = RWKV_CHUNK
    step = pl.program_id(0)

    @pl.when(step == 0)
    def _():
        state_s[...] = jnp.zeros(state_s.shape, F32)

    has_prev = jnp.where(step > 0, 1.0, 0.0)
    row = lax.broadcasted_iota(jnp.int32, (tb, LANES), 0)

    def shifted_mix(t):
        z = z_ref[t]
        zp = jnp.where(row == 0, prev_ref[t, 7:8, :] * has_prev, pltpu.roll(z, 1, axis=0))
        return z + (zp - z) * mu_ref[t]

    ri = lax.broadcasted_iota(jnp.int32, (tb, tb), 0)
    ci = lax.broadcasted_iota(jnp.int32, (tb, tb), 1)
    tri = jnp.where(((ri >> 6) == (ci >> 6)) & (ci <= ri), 1.0, 0.0).astype(BF16)
    wide = 2 * LANES
    wr = lax.broadcasted_iota(jnp.int32, (wide, wide), 0)
    wc = lax.broadcasted_iota(jnp.int32, (wide, wide), 1)
    head_sum = jnp.where((wr >> 6) == (wc >> 6), 1.0, 0.0).astype(BF16)
    head_mean = jnp.where((wr >> 6) == (wc >> 6), 1.0 / RWKV_HEAD_DIM, 0.0).astype(BF16)

    wa = shifted_mix(3 * RWKV_PAIRS)
    dw = _dot(jnp.tanh(wa).astype(BF16), wup_ref[...])
    da = _dot(wa.astype(BF16), aup_ref[...])
    g0 = jax.nn.sigmoid(shifted_mix(3 * RWKV_PAIRS + 1)).astype(BF16)
    g1 = jax.nn.sigmoid(shifted_mix(3 * RWKV_PAIRS + 2)).astype(BF16)
    g_s[...] = _dot(g0, gup_ref[:LANES]) + _dot(g1, gup_ref[LANES:])

    def mixed(first_tile, q):
        return jnp.concatenate([shifted_mix(first_tile + 2 * q), shifted_mix(first_tile + 2 * q + 1)], axis=1)

    for q in range(RWKV_PAIRS // 2):
        sl = slice(q * wide, (q + 1) * wide)
        r = mixed(0, q)
        k = mixed(RWKV_PAIRS, q)
        v = mixed(2 * RWKV_PAIRS, q)
        w_log = -_softplus(-(pv_ref[0:1, sl] + dw[:, sl])) - 0.5
        lw = -jnp.exp(w_log)
        icl = jax.nn.sigmoid(pv_ref[1:2, sl] + da[:, sl])
        kk = k * pv_ref[2:3, sl]
        kk = kk * lax.rsqrt(jnp.maximum(_dot((kk * kk).astype(BF16), head_sum), 1e-24))
        k2 = k * (1.0 + (icl - 1.0) * pv_ref[3:4, sl])
        bonus_s[:, sl] = _dot((r * k2 * pv_ref[4:5, sl]).astype(BF16), head_sum) * v
        hi, mid, lo = _split3(lw)
        cum = _dot(tri, hi) + _dot(tri, mid) + _dot(tri, lo)
        b = kk * icl
        inv_decay = jnp.exp(-cum)
        to_end = []
        for c in range(tb // C):
            e_end = jnp.exp(cum[c * C + C - 1:c * C + C])
            gc_s[8 * c:8 * c + 1, sl] = e_end
            to_end.append(inv_decay[c * C:(c + 1) * C] * e_end)
        to_end = jnp.concatenate(to_end, axis=0)
        rt_s[:, sl] = (r * jnp.exp(cum)).astype(BF16)
        at_s[:, sl] = (-kk * jnp.exp(cum - lw)).astype(BF16)
        kh_s[:, sl] = (k2 * inv_decay).astype(BF16)
        bh_s[:, sl] = (b * inv_decay).astype(BF16)
        kb_s[:, sl] = (k2 * to_end).astype(BF16)
        bb_s[:, sl] = (b * to_end).astype(BF16)
        v_s[:, sl] = v.astype(BF16)

    hr = lax.broadcasted_iota(jnp.int32, (LANES, LANES), 0)
    hc = lax.broadcasted_iota(jnp.int32, (LANES, LANES), 1)
    same_head = (hr >> 6) == (hc >> 6)
    t_r = hr & (C - 1)
    t_c = hc & (C - 1)
    strict = same_head & (t_c < t_r)
    incl = same_head & (t_c <= t_r)
    eye = jnp.where(hr == hc, 1.0, 0.0)
    lane_head = lax.broadcasted_iota(jnp.int32, (C, LANES), 1) >> 6

    def stack(zc):
        return jnp.concatenate([jnp.where(lane_head == 0, zc, jnp.zeros_like(zc)),
                                jnp.where(lane_head == 1, zc, jnp.zeros_like(zc))], axis=0)

    def chunk(c, carry):
        t0 = pl.multiple_of(c * C, C)
        rows = pl.ds(t0, C)
        pairs = range(RWKV_PAIRS)
        sls = [slice(p * LANES, (p + 1) * LANES) for p in pairs]
        lhs = [jnp.concatenate([stack(at_s[rows, sl]), stack(rt_s[rows, sl])], axis=0) for sl in sls]
        rhs = [jnp.concatenate([stack(kh_s[rows, sl]), stack(bh_s[rows, sl])], axis=0) for sl in sls]
        aa = [_dot_nt(lhs[p], rhs[p]) for p in pairs]
        a_ak = [jnp.where(strict, aa[p][:2 * C, :2 * C], 0.0).astype(BF16) for p in pairs]
        n_pow = [jnp.where(strict, aa[p][:2 * C, 2 * C:], 0.0) for p in pairs]
        a_r = [jnp.concatenate([jnp.where(incl, aa[p][2 * C:, :2 * C], 0.0).astype(BF16),
                                jnp.where(incl, aa[p][2 * C:, 2 * C:], 0.0).astype(BF16)], axis=1) for p in pairs]
        t_inv = [eye + n_pow[p] for p in pairs]
        for _ in range(5):
            nb = [n_pow[p].astype(BF16) for p in pairs]
            n_pow = [_dot(nb[p], nb[p]) for p in pairs]
            t_inv = [t_inv[p] + _dot(t_inv[p].astype(BF16), n_pow[p].astype(BF16)) for p in pairs]
        state = [state_s[p] for p in pairs]
        xs = [_dot_nt(lhs[p], state[p].astype(BF16)) for p in pairs]
        v_c = [v_s[rows, sl] for sl in sls]
        v_st = [stack(v_c[p]) for p in pairs]
        av = [_dot(a_ak[p], v_st[p]) for p in pairs]
        sa = [_dot(t_inv[p].astype(BF16), (xs[p][:2 * C] + av[p]).astype(BF16)) for p in pairs]
        ys = [xs[p][2 * C:] + _dot(a_r[p], jnp.concatenate([v_st[p], sa[p].astype(BF16)], axis=0)) for p in pairs]
        upd = [_dot_tn(jnp.concatenate([v_c[p], (sa[p][:C] + sa[p][C:]).astype(BF16)], axis=0),
                       jnp.concatenate([kb_s[rows, sls[p]], bb_s[rows, sls[p]]], axis=0)) for p in pairs]
        for p in pairs:
            y_s[rows, sls[p]] = ys[p][:C] + ys[p][C:]
            state_s[p] = (state[p] * gc_s[pl.ds(pl.multiple_of(c * 8, 8), 1), sls[p]]
                          + jnp.where(same_head, upd[p], 0.0))
        return carry

    lax.fori_loop(0, tb // C, chunk, 0)

    for q in range(RWKV_PAIRS // 2):
        sl = slice(q * wide, (q + 1) * wide)
        y = y_s[:, sl]
        d = y - _dot(y.astype(BF16), head_mean)
        var = _dot((d * d).astype(BF16), head_mean)
        yn = d * lax.rsqrt(var + LNX_EPS) * pv_ref[5:6, sl] + pv_ref[6:7, sl]
        o_ref[:, sl] = (yn + bonus_s[:, sl]) * g_s[:, sl]


def _hi_lo(w):
    hi = w.astype(BF16)
    return hi, (w - hi.astype(F32)).astype(BF16)


def _rwkv(zr, mu_t, w0, w_up, a0, a_up, g_up, k_k, k_a, r_k, lnx_g, lnx_b, tb):
    T = zr.shape[1]
    n_in = RW_TILES - 1
    pv = jnp.stack([w0, a0, k_k, k_a, r_k.reshape(-1), lnx_g, lnx_b, jnp.zeros_like(w0)])
    wup = jnp.pad(w_up, ((0, LANES - W_RANK), (0, 0))).astype(BF16)
    aup = jnp.pad(a_up, ((W_RANK, LANES - W_RANK - A_RANK), (0, 0))).astype(BF16)
    gup = jnp.pad(g_up, ((0, 2 * LANES - G_RANK), (0, 0))).astype(BF16)
    full = lambda shape: pl.BlockSpec(shape, lambda s: (0,) * len(shape))
    kern = functools.partial(_rwkv_kernel, tb=tb)
    bf = lambda: pltpu.VMEM((tb, RWKV_WIDTH), BF16)
    ff = lambda: pltpu.VMEM((tb, RWKV_WIDTH), F32)
    return pl.pallas_call(
        kern,
        grid=(T // tb,),
        in_specs=[
            pl.BlockSpec((n_in, tb, LANES), lambda s: (0, s, 0)),
            pl.BlockSpec((n_in, 8, LANES), lambda s: (0, jnp.maximum(s * (tb // 8) - 1, 0), 0)),
            full((n_in, 1, LANES)),
            full((8, RWKV_WIDTH)),
            full((LANES, RWKV_WIDTH)), full((LANES, RWKV_WIDTH)), full((2 * LANES, RWKV_WIDTH)),
        ],
        out_specs=pl.BlockSpec((tb, RWKV_WIDTH), lambda s: (s, 0)),
        out_shape=jax.ShapeDtypeStruct((T, RWKV_WIDTH), F32),
        scratch_shapes=[bf(), bf(), bf(), bf(), bf(), bf(), bf(),
                        pltpu.VMEM((8 * (tb // RWKV_CHUNK), RWKV_WIDTH), F32), ff(), ff(), ff(),
                        pltpu.VMEM((RWKV_PAIRS, LANES, LANES), F32)],
        compiler_params=_cparams(("arbitrary",)),
        name="rwkv7",
    )(zr, zr, mu_t, pv, wup, aup, gup)


PROJ_COLS = 512


def _mix_out_kernel(ya_ref, yb_ref, g_ref, w_ref, x_ref, o_ref):
    h = jnp.concatenate([_rms(ya_ref[...], g_ref[...]).astype(BF16), yb_ref[...].astype(BF16)], axis=1)
    for n in range(D_MODEL // PROJ_COLS):
        sl = slice(n * PROJ_COLS, (n + 1) * PROJ_COLS)
        o_ref[:, sl] = x_ref[:, sl] + _dot(h, w_ref[:, sl])


def _mix_out(y_a, y_b, g, w, x, tm):
    T = x.shape[0]
    return pl.pallas_call(
        _mix_out_kernel,
        grid=(T // tm,),
        in_specs=[
            pl.BlockSpec((tm, NSA_WIDTH), lambda m: (m, 0)),
            pl.BlockSpec((tm, RWKV_WIDTH), lambda m: (m, 0)),
            _resident((1, NSA_WIDTH)),
            _resident((NSA_WIDTH + RWKV_WIDTH, D_MODEL)),
            pl.BlockSpec((tm, D_MODEL), lambda m: (m, 0)),
        ],
        out_specs=pl.BlockSpec((tm, D_MODEL), lambda m: (m, 0)),
        out_shape=jax.ShapeDtypeStruct((T, D_MODEL), F32),
        compiler_params=_cparams(("parallel",)),
        name="mix_out_proj",
    )(y_a, y_b, g, w, x)


def _norm_mm_kernel(x_ref, g_ref, w_ref, o_ref, h_scr):
    @pl.when(pl.program_id(1) == 0)
    def _():
        h_scr[...] = _rms(x_ref[...], g_ref[...]).astype(BF16)

    o_ref[...] = _dot(h_scr[...], w_ref[...]).astype(o_ref.dtype)


def _norm_mm(x, g, w, tm, tn, name):
    M, K = x.shape
    N = w.shape[1]
    return pl.pallas_call(
        _norm_mm_kernel,
        grid=(M // tm, N // tn),
        in_specs=[
            pl.BlockSpec((tm, K), lambda m, n: (m, 0)),
            pl.BlockSpec((1, K), lambda m, n: (0, 0)),
            pl.BlockSpec((K, tn), lambda m, n: (0, n)),
        ],
        out_specs=pl.BlockSpec((tm, tn), lambda m, n: (m, n)),
        out_shape=jax.ShapeDtypeStruct((M, N), BF16),
        scratch_shapes=[pltpu.VMEM((tm, K), BF16)],
        compiler_params=_cparams(("parallel", "arbitrary")),
        name=name,
    )(x, g, w)


def _mem_router_kernel(x_ref, km_ref, vm_ref, wq_ref, wo_ref, g2_ref, g3_ref, w_hi, w_lo, b_ref,
                       x2_ref, h_ref, c_ref, n_ref):
    scale = MEM_HEAD_DIM ** -0.5
    hq = _rms(x_ref[...], g2_ref[...]).astype(BF16)
    heads = []
    for hd in range(MEM_HEADS):
        sl = slice(hd * MEM_HEAD_DIM, (hd + 1) * MEM_HEAD_DIM)
        q = _dot(hq, wq_ref[:, sl]).astype(BF16)
        s = _dot_nt(q, km_ref[:, sl]) * scale
        e = jnp.exp(s - jnp.max(s, axis=-1, keepdims=True))
        p = e / jnp.sum(e, axis=-1, keepdims=True)
        heads.append(_dot(p.astype(BF16), vm_ref[:, sl]).astype(BF16))
    o = jnp.concatenate(heads, axis=1)
    for n in range(D_MODEL // PROJ_COLS):
        sl = slice(n * PROJ_COLS, (n + 1) * PROJ_COLS)
        x2_ref[:, sl] = x_ref[:, sl] + _dot(o, wo_ref[:, sl])
    _route(x2_ref[...], g3_ref, w_hi, w_lo, b_ref, h_ref, c_ref, n_ref)


def _mem_router(x, k_mem, v_mem, wq, wo, g2, g3, w_hi, w_lo, b, tm):
    T = x.shape[0]
    M = k_mem.shape[0]
    row_block = lambda width: pl.BlockSpec((tm, width), lambda m: (m, 0))
    return pl.pallas_call(
        _mem_router_kernel,
        grid=(T // tm,),
        in_specs=[
            row_block(D_MODEL),
            _resident((M, D_MODEL)), _resident((M, D_MODEL)),
            _resident((D_MODEL, D_MODEL)), _resident((D_MODEL, D_MODEL)),
            _resident((1, D_MODEL)), _resident((1, D_MODEL)),
            _resident((D_MODEL, LANES)), _resident((D_MODEL, LANES)), _resident((1, LANES)),
        ],
        out_specs=[row_block(D_MODEL), row_block(D_MODEL), row_block(LANES),
                   pl.BlockSpec((1, 8, LANES), lambda m: (m, 0, 0))],
        out_shape=[
            jax.ShapeDtypeStruct((T, D_MODEL), F32),
            jax.ShapeDtypeStruct((T, D_MODEL), BF16),
            jax.ShapeDtypeStruct((T, LANES), F32),
            jax.ShapeDtypeStruct((T // tm, 8, LANES), F32),
        ],
        compiler_params=_cparams(("parallel",)),
        name="mem_attn_router",
    )(x, k_mem, v_mem, wq, wo, g2, g3, w_hi, w_lo, b)


def _route(x, g_ref, w_hi, w_lo, b_ref, h_ref, c_ref, n_ref):
    h = _rms(x, g_ref[...])
    h_ref[...] = h.astype(BF16)
    logits = _dot_xw3(h, w_hi[...], w_lo[...]) + b_ref[...]
    lane = lax.broadcasted_iota(jnp.int32, logits.shape, 1)
    big = jnp.int32(LANES)
    is_grp = (lane >= N_EXPERTS) & (lane < N_EXPERTS + N_GROUPS)
    lg = jnp.where(is_grp, logits, NEG)
    eg = jnp.where(is_grp, jnp.exp(lg - jnp.max(lg, axis=-1, keepdims=True)), 0.0)
    pg = eg / jnp.sum(eg, axis=-1, keepdims=True)
    pg_top = jnp.max(pg, axis=-1, keepdims=True)
    g_idx = jnp.min(jnp.where(is_grp & (pg == pg_top), lane - N_EXPERTS, big), axis=-1, keepdims=True)
    in_grp = (lane < N_EXPERTS) & ((lane >> 3) == g_idx)
    le = jnp.where(in_grp, logits, NEG)
    ee = jnp.where(in_grp, jnp.exp(le - jnp.max(le, axis=-1, keepdims=True)), 0.0)
    pe = jnp.where(in_grp, ee / jnp.sum(ee, axis=-1, keepdims=True), -1.0)
    p1 = jnp.max(pe, axis=-1, keepdims=True)
    hit1 = lane == jnp.min(jnp.where(pe == p1, lane, big), axis=-1, keepdims=True)
    pe2 = jnp.where(hit1, -1.0, pe)
    p2 = jnp.max(pe2, axis=-1, keepdims=True)
    hit2 = lane == jnp.min(jnp.where(pe2 == p2, lane, big), axis=-1, keepdims=True)
    denom = p1 + p2
    comb = jnp.where(hit1, pg_top * p1 / denom, 0.0) + jnp.where(hit2, pg_top * p2 / denom, 0.0)
    c_ref[...] = jnp.where(lane == GROUP_LANE, g_idx.astype(F32), comb)
    in_group = jnp.where(lane == g_idx, 1.0, 0.0)
    n_ref[0] = jnp.broadcast_to(jnp.sum(in_group, axis=0, keepdims=True), n_ref.shape[1:])


def _moe_kernel(meta_ref, h_ref, c_ref, wg_ref, wu_ref, wd_ref, y_ref, hs, cs, ys, slot_scr, perm, *, tm, sub):
    m = pl.program_id(0)
    e = pl.program_id(1)
    nslot = hs.shape[0]

    @pl.when(e == 0)
    def _():
        c = c_ref[...]
        lane = lax.broadcasted_iota(jnp.int32, c.shape, 1)
        one_hot = jnp.where(lane.astype(F32) == c[:, GROUP_LANE:GROUP_LANE + 1], 1.0, 0.0)
        earlier = jnp.where(lax.broadcasted_iota(jnp.int32, (tm, tm), 1)
                            < lax.broadcasted_iota(jnp.int32, (tm, tm), 0), 1.0, 0.0).astype(BF16)
        rank = _dot(earlier, one_hot.astype(BF16))
        start = jnp.zeros((1, LANES), F32)
        for g in range(N_GROUPS):
            start = jnp.where(lane[0:1] == g, (meta_ref[m * 2 * N_GROUPS + g] * sub).astype(F32), start)
        slot = jnp.sum(one_hot * (rank + start), axis=-1, keepdims=True)
        slot_scr[...] = jnp.broadcast_to(slot, slot_scr.shape)
        slot_row = slot_scr[...].T[0:1, :]
        perm[...] = jnp.where(lax.broadcasted_iota(jnp.int32, (nslot, tm), 0).astype(F32) == slot_row,
                              1.0, 0.0).astype(BF16)
        hs[...] = _dot(perm[...], h_ref[...]).astype(BF16)
        w_hi, w_lo = _split2(jnp.where(lane < N_EXPERTS, c, 0.0))
        cs[...] = _dot(perm[...], w_hi) + _dot(perm[...], w_lo)
        ys[...] = jnp.zeros(ys.shape, F32)

    grp = e // EXPERTS_PER_GROUP
    first = meta_ref[m * 2 * N_GROUPS + grp]
    count = meta_ref[m * 2 * N_GROUPS + N_GROUPS + grp]
    def ffn(start_block, n_blocks):
        n = n_blocks * sub
        rows = pl.ds(pl.multiple_of(start_block * sub, sub), n)
        x = hs[rows, :]
        hid = jax.nn.silu(_dot(x, wg_ref[0])) * _dot(x, wu_ref[0])
        lane_s = lax.broadcasted_iota(jnp.int32, (n, LANES), 1)
        c_e = jnp.sum(jnp.where(lane_s == e, cs[rows, :], 0.0), axis=-1, keepdims=True)
        ys[rows, :] += _dot((hid * c_e).astype(BF16), wd_ref[0])

    def quad(s, carry):
        ffn(first + 4 * s, 4)
        return carry

    quads = count // 4
    lax.fori_loop(0, quads, quad, 0)
    rest = count - 4 * quads

    @pl.when(rest >= 2)
    def _():
        ffn(first + 4 * quads, 2)

    @pl.when(rest % 2 == 1)
    def _():
        ffn(first + count - 1, 1)

    @pl.when(e == pl.num_programs(1) - 1)
    def _():
        back = jnp.where(lax.broadcasted_iota(jnp.int32, (tm, nslot), 1).astype(F32) == slot_scr[:, 0:1],
                         1.0, 0.0).astype(BF16)
        for n in range(D_MODEL // PROJ_COLS):
            sl = slice(n * PROJ_COLS, (n + 1) * PROJ_COLS)
            y_ref[:, sl] = _dot(back, ys[:, sl].astype(BF16)).astype(y_ref.dtype)


def _moe(h, comb, counts, wg, wu, wd, tm, sub):
    T = h.shape[0]
    nt = T // tm
    nslot = tm + N_GROUPS * sub
    cnt = counts[:, 0, :N_GROUPS].astype(jnp.int32).reshape(nt, -1, N_GROUPS).sum(axis=1)
    nblk = (cnt + sub - 1) // sub
    first = jnp.cumsum(nblk, axis=1) - nblk
    meta = jnp.concatenate([first, nblk], axis=1).reshape(-1)
    once = lambda shape: pl.BlockSpec(shape, lambda m, e, meta: (m, 0), pipeline_mode=pl.Buffered(1))
    kern = functools.partial(_moe_kernel, tm=tm, sub=sub)
    return pl.pallas_call(
        kern,
        grid_spec=pltpu.PrefetchScalarGridSpec(
            num_scalar_prefetch=1,
            grid=(nt, N_EXPERTS),
            in_specs=[
                once((tm, D_MODEL)),
                once((tm, LANES)),
                pl.BlockSpec((1, D_MODEL, EXPERT_FF), lambda m, e, meta: (e, 0, 0)),
                pl.BlockSpec((1, D_MODEL, EXPERT_FF), lambda m, e, meta: (e, 0, 0)),
                pl.BlockSpec((1, EXPERT_FF, D_MODEL), lambda m, e, meta: (e, 0, 0)),
            ],
            out_specs=pl.BlockSpec((tm, D_MODEL), lambda m, e, meta: (m, 0)),
            scratch_shapes=[
                pltpu.VMEM((nslot, D_MODEL), BF16),
                pltpu.VMEM((nslot, LANES), F32),
                pltpu.VMEM((nslot, D_MODEL), F32),
                pltpu.VMEM((tm, LANES), F32),
                pltpu.VMEM((nslot, tm), BF16),
            ],
        ),
        out_shape=jax.ShapeDtypeStruct((T, D_MODEL), BF16),
        compiler_params=_cparams(("parallel", "arbitrary")),
        name="moe_experts",
    )(meta, h, comb, wg, wu, wd)


def _final_norm_kernel(x_ref, y_ref, g_ref, o_ref):
    o_ref[...] = _rms(x_ref[...] + y_ref[...].astype(F32), g_ref[...])


def _final_norm(x, y, g, tm):
    T = x.shape[0]
    blk = pl.BlockSpec((tm, D_MODEL), lambda m: (m, 0))
    return pl.pallas_call(
        _final_norm_kernel,
        grid=(T // tm,),
        in_specs=[blk, blk, _resident((1, D_MODEL))],
        out_specs=blk,
        out_shape=jax.ShapeDtypeStruct((T, D_MODEL), F32),
        compiler_params=_cparams(("parallel",)),
        name="moe_residual_norm",
    )(x, y, g)


def _pack_w_in(w_in, rwkv_mu):
    nsa_cols = NSA_WIDTH + 6 * NSA_KV_WIDTH
    gl = w_in[:, nsa_cols:nsa_cols + 3 * NSA_HEADS]
    rw0 = nsa_cols + 3 * NSA_HEADS
    rkv = w_in[:, rw0:rw0 + 3 * RWKV_WIDTH + W_RANK + A_RANK]
    gd = w_in[:, rw0 + 3 * RWKV_WIDTH + W_RANK + A_RANK:]

    def pad(a, n):
        return jnp.pad(a, ((0, 0), (0, n - a.shape[1])))

    w = jnp.concatenate([w_in[:, :nsa_cols], rkv, pad(gd, 2 * LANES), pad(gl, LANES)], axis=1)
    mu = rwkv_mu.reshape(1, -1)
    mu_rkv = mu[:, :3 * RWKV_WIDTH + W_RANK + A_RANK]
    mu_gd = pad(mu[:, 3 * RWKV_WIDTH + W_RANK + A_RANK:], 2 * LANES)
    mu_t = jnp.concatenate([mu_rkv, mu_gd], axis=1).reshape(RW_TILES - 1, 1, LANES)
    return w.astype(BF16), mu_t


def kernel(x, mem, ln1_g, w_in, cmp_pos_k, cmp_w1_k, cmp_w2_k, cmp_pos_v, cmp_w1_v, cmp_w2_v, nsa_norm_g, rwkv_mu, rwkv_w0, rwkv_w_up, rwkv_a0, rwkv_a_up, rwkv_g_up, rwkv_k_k, rwkv_k_a, rwkv_r_k, rwkv_lnx_g, rwkv_lnx_b, w_out, ln_mem_g, ln2_g, wq_mem, wk_mem, wv_mem, wo_mem, ln3_g, router_group_w, router_group_b, router_expert_w, router_expert_b, moe_w_gate, moe_w_up, moe_w_down, lnf_g):
    B, T, _ = x.shape
    assert B == 1 and T % max(TM_PROJ, TM_MEM, TQ_SEL, TK_SEL, TB_RWKV, TM_MOE) == 0
    assert w_in.shape[0] == 1
    row = lambda a: a.reshape(1, -1)
    xs = x[0]
    for l in range(w_in.shape[0]):
        wp, mu_t = _pack_w_in(w_in[l], rwkv_mu[l])
        zn, zr, zc = _in_proj(xs, row(ln1_g[l]), wp, TM_PROJ)
        y_a = _nsa(zn, zr, zc, cmp_pos_k[l], cmp_w1_k[l], cmp_w2_k[l], cmp_pos_v[l], cmp_w1_v[l], cmp_w2_v[l],
                   TQ_NSA, TQ_SEL, TK_SEL)
        y_b = _rwkv(zr, mu_t, rwkv_w0[l], rwkv_w_up[l], rwkv_a0[l], rwkv_a_up[l], rwkv_g_up[l], rwkv_k_k[l],
                    rwkv_k_a[l], rwkv_r_k[l], rwkv_lnx_g[l], rwkv_lnx_b[l], TB_RWKV)
        xs = _mix_out(y_a, y_b, row(nsa_norm_g[l]), w_out[l].astype(BF16), xs, TM_MEM)
        m_tok = mem[0]
        k_mem = _norm_mm(m_tok, row(ln_mem_g[l]), wk_mem[l].astype(BF16), m_tok.shape[0], PROJ_COLS, "mem_k")
        v_mem = _norm_mm(m_tok, row(ln_mem_g[l]), wv_mem[l].astype(BF16), m_tok.shape[0], PROJ_COLS, "mem_v")
        w_r = jnp.pad(jnp.concatenate([router_expert_w[l], router_group_w[l]], axis=1),
                      ((0, 0), (0, LANES - N_EXPERTS - N_GROUPS)))
        b_r = jnp.pad(jnp.concatenate([router_expert_b[l], router_group_b[l]]), (0, LANES - N_EXPERTS - N_GROUPS))
        xs, h3, comb, counts = _mem_router(xs, k_mem, v_mem, wq_mem[l].astype(BF16), wo_mem[l].astype(BF16),
                                   row(ln2_g[l]), row(ln3_g[l]), *_hi_lo(w_r), row(b_r), TM_MEM)
        y_moe = _moe(h3, comb, counts, moe_w_gate[l].astype(BF16), moe_w_up[l].astype(BF16),
                     moe_w_down[l].astype(BF16), TM_MOE, SUB_MOE)
        xs = _final_norm(xs, y_moe, row(lnf_g), TM_PROJ)
    return xs[None]
```

```python
import functools

import numpy as np
import jax
import jax.numpy as jnp
from jax import lax
from jax.experimental import pallas as pl
from jax.experimental.pallas import tpu as pltpu

F32 = jnp.float32
BF16 = jnp.bfloat16

LANES = 128
D_MODEL = 2048
EPS = 1e-6
NSA_HEAD_DIM = 128
NSA_HEADS = 8
NSA_KV_GROUPS = 2
NSA_HPG = NSA_HEADS // NSA_KV_GROUPS
NSA_WIDTH = NSA_HEADS * NSA_HEAD_DIM
NSA_KV_WIDTH = NSA_KV_GROUPS * NSA_HEAD_DIM
CMP_BLOCK = 32
CMP_STRIDE = 16
SLC_BLOCK = 64
SLC_TOPK = 16
WINDOW = 512
NEG = -1e30
RWKV_HEAD_DIM = 64
RWKV_HEADS = 16
RWKV_WIDTH = RWKV_HEADS * RWKV_HEAD_DIM
RWKV_PAIRS = RWKV_WIDTH // LANES
W_RANK = 64
A_RANK = 64
G_RANK = 160
LNX_EPS = 64e-5
RWKV_CHUNK = 64
RWKV_CHUNK_GROUP = 4
MEM_HEADS = 4
MEM_HEAD_DIM = D_MODEL // MEM_HEADS
N_GROUPS = 4
EXPERTS_PER_GROUP = 8
N_EXPERTS = N_GROUPS * EXPERTS_PER_GROUP
EXPERT_FF = 256

NSA_TILES = 20
RW_TILES = 28
IN_TILE_BLOCK = 4

VMEM_LIMIT = 56 * 1024 * 1024

TM_PROJ = 256
TM_MEM = 512
TQ_NSA = 256
TQ_SEL = 512
TK_SEL = 512
TB_RWKV = 256
TM_MOE = 1024
SUB_MOE = 64
GROUP_LANE = 64


def _cparams(sem):
    return pltpu.CompilerParams(dimension_semantics=sem, vmem_limit_bytes=VMEM_LIMIT)


def _dot(a, b):
    return jnp.dot(a, b, preferred_element_type=F32)


def _dot_nt(a, b):
    return lax.dot_general(a, b, (((1,), (1,)), ((), ())), preferred_element_type=F32)


def _dot_tn(a, b):
    return lax.dot_general(a, b, (((0,), (0,)), ((), ())), preferred_element_type=F32)


def _split2(x):
    hi = x.astype(BF16)
    lo = (x - hi.astype(F32)).astype(BF16)
    return hi, lo


def _split3(x):
    hi = x.astype(BF16)
    r1 = x - hi.astype(F32)
    mid = r1.astype(BF16)
    lo = (r1 - mid.astype(F32)).astype(BF16)
    return hi, mid, lo


def _dot_x2(x, w):
    hi, lo = _split2(x)
    return _dot(hi, w) + _dot(lo, w)


def _dot_x3(x, w):
    hi, mid, lo = _split3(x)
    return _dot(hi, w) + _dot(mid, w) + _dot(lo, w)


def _dot_xw3(x, w_hi, w_lo):
    hi, lo = _split2(x)
    return _dot(hi, w_hi) + _dot(lo, w_hi) + _dot(hi, w_lo)


def _rms(x, g):
    ms = jnp.mean(x * x, axis=-1, keepdims=True)
    return x * lax.rsqrt(ms + EPS) * g


def _resident(shape):
    return pl.BlockSpec(shape, lambda *_: (0,) * len(shape), pipeline_mode=pl.Buffered(1))


CMP_TILES = 2 * NSA_KV_GROUPS


def _in_proj_kernel(x_ref, g_ref, w_ref, zn_ref, zr_ref, zc_ref, *, q_scale):
    h = _rms(x_ref[...], g_ref[...]).astype(BF16)
    nb = IN_TILE_BLOCK * LANES
    for n in range((NSA_TILES + RW_TILES) // IN_TILE_BLOCK):
        z = _dot(h, w_ref[:, n * nb:(n + 1) * nb])
        for t in range(IN_TILE_BLOCK):
            tile = n * IN_TILE_BLOCK + t
            zt = z[:, t * LANES:(t + 1) * LANES]
            if tile < NSA_HEADS:
                zn_ref[tile] = (zt * q_scale).astype(BF16)
            elif tile < NSA_TILES:
                zn_ref[tile] = zt.astype(BF16)
                if tile < NSA_HEADS + CMP_TILES:
                    zc_ref[tile - NSA_HEADS] = zt
            else:
                zr_ref[tile - NSA_TILES] = zt


def _in_proj(x, g, w, tm):
    T = x.shape[0]
    kern = functools.partial(_in_proj_kernel, q_scale=NSA_HEAD_DIM ** -0.5)
    return pl.pallas_call(
        kern,
        grid=(T // tm,),
        in_specs=[
            pl.BlockSpec((tm, D_MODEL), lambda m: (m, 0)),
            _resident((1, D_MODEL)),
            _resident((D_MODEL, (NSA_TILES + RW_TILES) * LANES)),
        ],
        out_specs=[
            pl.BlockSpec((NSA_TILES, tm, LANES), lambda m: (0, m, 0)),
            pl.BlockSpec((RW_TILES, tm, LANES), lambda m: (0, m, 0)),
            pl.BlockSpec((CMP_TILES, tm, LANES), lambda m: (0, m, 0)),
        ],
        out_shape=[
            jax.ShapeDtypeStruct((NSA_TILES, T, LANES), BF16),
            jax.ShapeDtypeStruct((RW_TILES, T, LANES), F32),
            jax.ShapeDtypeStruct((CMP_TILES, T, LANES), F32),
        ],
        compiler_params=_cparams(("parallel",)),
        name="in_proj",
    )(x, g, w)


def _compress_kernel(x_ref, pos_ref, w1_ref, w2_ref, o_ref):
    n = x_ref.shape[1] // CMP_STRIDE
    w1 = w1_ref[0]
    h_first = jnp.zeros((n, NSA_HEAD_DIM), F32)
    h_second = jnp.zeros((n, NSA_HEAD_DIM), F32)
    for t in range(CMP_STRIDE):
        x_t = x_ref[0, pl.ds(t, n, stride=CMP_STRIDE), :].astype(BF16)
        h_first += _dot(x_t, w1[t * NSA_HEAD_DIM:(t + 1) * NSA_HEAD_DIM])
        h_second += _dot(x_t, w1[(CMP_STRIDE + t) * NSA_HEAD_DIM:(CMP_STRIDE + t + 1) * NSA_HEAD_DIM])
    h_next = pltpu.roll(h_second, n - 1, axis=0)
    bias = _dot(pos_ref[0], w1)[0:1]
    hid = jax.nn.gelu(h_first + h_next + bias)
    o_ref[0] = _dot(hid.astype(BF16), w2_ref[0]).astype(o_ref.dtype)


def _compress(zc, pos, w1, w2):
    T = zc.shape[1]
    n = T // CMP_STRIDE
    width = CMP_BLOCK * NSA_HEAD_DIM
    return pl.pallas_call(
        _compress_kernel,
        grid=(4,),
        in_specs=[
            pl.BlockSpec((1, T, NSA_HEAD_DIM), lambda s: (s, 0, 0)),
            pl.BlockSpec((1, 8, width), lambda s: (s // 2, 0, 0)),
            pl.BlockSpec((1, width, NSA_HEAD_DIM), lambda s: (s // 2, 0, 0)),
            pl.BlockSpec((1, NSA_HEAD_DIM, NSA_HEAD_DIM), lambda s: (s // 2, 0, 0)),
        ],
        out_specs=pl.BlockSpec((1, n, NSA_HEAD_DIM), lambda s: (s, 0, 0)),
        out_shape=jax.ShapeDtypeStruct((4, n, NSA_HEAD_DIM), BF16),
        compiler_params=_cparams(("parallel",)),
        name="nsa_compress",
    )(zc, pos, w1, w2)


def _cmp_win_kernel(q_ref, kc_ref, lhs_ref, kw_ref, vwt_ref, oc_ref, ow_ref, sel_ref, s_scr, acc_scr,
                    *, tq, top_n, ch):
    q0 = pl.program_id(1) * tq
    cols = NSA_HPG * tq
    ncp = kc_ref.shape[1]
    ns = lhs_ref.shape[1] - NSA_HEAD_DIM - SEL_ONES_ROWS
    q_t = jnp.concatenate([q_ref[h].astype(F32).T.astype(BF16) for h in range(NSA_HPG)], axis=1)
    qpos = q0 + (lax.broadcasted_iota(jnp.int32, (1, cols), 1) & (tq - 1))

    def store_heads(o_ref, o_t):
        for h in range(NSA_HPG):
            o_ref[:, h * NSA_HEAD_DIM:(h + 1) * NSA_HEAD_DIM] = o_t[:, h * tq:(h + 1) * tq].T

    span = WINDOW + tq
    start = pl.multiple_of(jnp.maximum(q0 - WINDOW, 0), tq)
    kpos = start + lax.broadcasted_iota(jnp.int32, (span, 1), 0)
    in_window = (qpos - kpos).astype(jnp.uint32) < jnp.uint32(WINDOW)
    s_w = jnp.where(in_window, _dot(kw_ref[0, pl.ds(start, span), :], q_t), NEG)
    e_w = jnp.exp(s_w - jnp.max(s_w, axis=0, keepdims=True))
    p_w = e_w / jnp.sum(e_w, axis=0, keepdims=True)
    store_heads(ow_ref, _dot(vwt_ref[0, :, pl.ds(start, span)], p_w.astype(BF16)))

    visible = (q0 + tq - 1 - (CMP_BLOCK - 1)) // CMP_STRIDE + 1
    clean = jnp.maximum((q0 - (CMP_BLOCK - 1)) // CMP_STRIDE + 1, 0) // ch
    n_chunks = (visible + ch - 1) // ch

    def chunk_rows(c):
        return pl.ds(pl.multiple_of(c * ch, ch), ch)

    def score_chunk(c, m, masked):
        s = _dot(kc_ref[0, chunk_rows(c), :], q_t)
        if masked:
            cmp_end = (c * ch + lax.broadcasted_iota(jnp.int32, (ch, 1), 0)) * CMP_STRIDE + (CMP_BLOCK - 1)
            s = jnp.where(cmp_end <= qpos, s, NEG)
        s_scr[chunk_rows(c), :] = s
        return jnp.maximum(m, jnp.max(s, axis=0, keepdims=True))

    m = jnp.full((1, cols), NEG, F32)
    m = lax.fori_loop(0, clean, lambda c, m: score_chunk(c, m, False), m)
    m = lax.fori_loop(clean, n_chunks, lambda c, m: score_chunk(c, m, True), m)
    acc_scr[...] = jnp.zeros(acc_scr.shape, F32)

    def weigh_chunk(c, carry):
        e = jnp.exp(s_scr[chunk_rows(c), :] - m).astype(BF16)
        acc_scr[...] += _dot(lhs_ref[0, :, chunk_rows(c)], e)
        return carry

    lax.fori_loop(0, n_chunks, weigh_chunk, 0)
    acc = acc_scr[...]
    sees_any = jnp.where(qpos >= CMP_BLOCK - 1, 1.0, 0.0)
    r_inv = sees_any / jnp.maximum(acc[NSA_HEAD_DIM + ns:NSA_HEAD_DIM + ns + 1], 1e-30)
    store_heads(oc_ref, acc[:NSA_HEAD_DIM] * r_inv)
    weighted = acc[NSA_HEAD_DIM:NSA_HEAD_DIM + ns] * r_inv
    p_sel = weighted[:, 0:tq]
    for h in range(1, NSA_HPG):
        p_sel = p_sel + weighted[:, h * tq:(h + 1) * tq]
    blk = (q0 + lax.broadcasted_iota(jnp.int32, (1, tq), 1)) >> 6
    j = lax.broadcasted_iota(jnp.int32, (ns, tq), 0)
    future = j > blk
    forced = (j == 0) | (j == blk) | (j == blk - 1)
    score = jnp.where(future, -1.0, jnp.where(forced, 1e6, p_sel))
    sel = jnp.zeros((ns, tq), F32)
    for _ in range(top_n):
        best = jnp.max(score, axis=0, keepdims=True)
        idx = jnp.min(jnp.where(score == best, j, ns), axis=0, keepdims=True)
        hit = j == idx
        sel = jnp.where(hit, jnp.where(best >= 0.0, 1.0, 0.0), sel)
        score = jnp.where(hit, -2.0, score)
    sel_ref[0] = sel.astype(sel_ref.dtype)


CMP_CHUNK = 256


def _cmp_win(zn, kv_cmp, cmp_to_sel, vw_t, tq):
    T = zn.shape[1]
    ncp = kv_cmp.shape[1]
    ns = cmp_to_sel.shape[0]
    ch = min(CMP_CHUNK, ncp)
    assert ncp % ch == 0
    kw_tile0 = NSA_HEADS + 4 * NSA_KV_GROUPS
    lhs = jnp.concatenate([jnp.swapaxes(kv_cmp[NSA_KV_GROUPS:], 1, 2),
                           jnp.broadcast_to(cmp_to_sel, (NSA_KV_GROUPS, ns, ncp)),
                           jnp.ones((NSA_KV_GROUPS, SEL_ONES_ROWS, ncp), BF16)], axis=1)
    rows = lhs.shape[1]
    cols = NSA_HPG * tq
    kern = functools.partial(_cmp_win_kernel, tq=tq, top_n=min(SLC_TOPK, ns), ch=ch)
    wide = pl.BlockSpec((tq, NSA_HPG * NSA_HEAD_DIM), lambda g, i: (i, g))
    return pl.pallas_call(
        kern,
        grid=(NSA_KV_GROUPS, T // tq),
        in_specs=[
            pl.BlockSpec((NSA_HPG, tq, NSA_HEAD_DIM), lambda g, i: (g, i, 0)),
            pl.BlockSpec((1, ncp, NSA_HEAD_DIM), lambda g, i: (g, 0, 0)),
            pl.BlockSpec((1, rows, ncp), lambda g, i: (g, 0, 0)),
            pl.BlockSpec((1, T, NSA_HEAD_DIM), lambda g, i: (kw_tile0 + g, 0, 0)),
            pl.BlockSpec((1, NSA_HEAD_DIM, T), lambda g, i: (g, 0, 0)),
        ],
        out_specs=[wide, wide, pl.BlockSpec((1, ns, tq), lambda g, i: (g, 0, i))],
        out_shape=[
            jax.ShapeDtypeStruct((T, NSA_WIDTH), F32),
            jax.ShapeDtypeStruct((T, NSA_WIDTH), F32),
            jax.ShapeDtypeStruct((NSA_KV_GROUPS, ns, T), BF16),
        ],
        scratch_shapes=[pltpu.VMEM((ncp, cols), F32), pltpu.VMEM((rows, cols), F32)],
        compiler_params=_cparams(("parallel", "arbitrary")),
        name="nsa_cmp_window",
    )(zn, kv_cmp, lhs, zn, vw_t)


SEL_ONES_ROWS = 16


SEL_BUFFERS = 2


def _sel_attn_kernel(q_ref, k_ref, vt_ref, sel_ref, gl_ref, oc_ref, ow_ref, o_ref,
                     bias_scr, qa_scr, acc_scr, m_scr, *bufs, tq, tk, wb):
    s_bufs, p_bufs = bufs[:SEL_BUFFERS], bufs[SEL_BUFFERS:]
    q0 = pl.program_id(1) * tq
    cols = NSA_HPG * tq
    tpw = wb * SLC_BLOCK // tk
    last_tile = k_ref.shape[1] // tk - 1
    ns = sel_ref.shape[1]
    first_own = q0 // SLC_BLOCK
    not_chosen = (sel_ref[0].astype(F32) - 1.0) * (-NEG)
    for h in range(NSA_HPG):
        bias_scr[:, h * tq:(h + 1) * tq] = not_chosen
        qa_scr[:NSA_HEAD_DIM, h * tq:(h + 1) * tq] = q_ref[h].astype(F32).T.astype(BF16)

    def set_window(w):
        rows = pl.ds(pl.multiple_of(w * wb, wb), wb)
        before = w * wb + lax.broadcasted_iota(jnp.int32, (wb, 1), 0) < first_own
        qa_scr[NSA_HEAD_DIM:, :] = jnp.where(before, bias_scr[rows, :], NEG).astype(BF16)

    def scores(j):
        j = jnp.minimum(j, last_tile)
        return _dot(k_ref[0, pl.ds(pl.multiple_of(j * tk, tk), tk), :], qa_scr[...])

    def pv(j, p_ref):
        return _dot(vt_ref[0, :, pl.ds(pl.multiple_of(j * tk, tk), tk)], p_ref[...])

    def score_into(i, j):
        s = scores(j)
        s_bufs[i][...] = s
        m_scr[8 * (i + 1):8 * (i + 1) + 1, :] = jnp.max(s, axis=0, keepdims=True)

    def softmax_tile(i):
        m_old = m_scr[0:1, :]
        m_new = jnp.maximum(m_old, m_scr[8 * (i + 1):8 * (i + 1) + 1, :])
        m_scr[0:1, :] = m_new
        p_bufs[i][...] = jnp.exp((s_bufs[i][...] - m_new).astype(BF16))
        return jnp.exp(m_old - m_new)

    own = pl.ds(pl.multiple_of(q0, tq), tq)
    s = _dot(k_ref[0, own, :NSA_HEAD_DIM], qa_scr[:NSA_HEAD_DIM, :])
    s = jnp.concatenate([s[b * SLC_BLOCK:(b + 1) * SLC_BLOCK] + bias_scr[pl.ds(first_own + b, 1), :]
                         for b in range(tq // SLC_BLOCK)], axis=0)
    kpos = lax.broadcasted_iota(jnp.int32, (tq, 1), 0)
    qpos = lax.broadcasted_iota(jnp.int32, (1, cols), 1) & (tq - 1)
    s = jnp.where(kpos <= qpos, s, NEG)
    m_first = jnp.max(s, axis=0, keepdims=True)
    m_scr[0:1, :] = m_first
    acc_scr[...] = _dot(vt_ref[0, :, own], jnp.exp((s - m_first).astype(BF16)))

    n_tiles = (q0 + tk - 1) // tk

    nbuf = len(s_bufs)

    def window(w, carry):
        lo = w * tpw
        cnt = jnp.minimum(n_tiles - lo, tpw)
        set_window(w)
        score_into(0, lo)
        p_bufs[nbuf - 1][...] = jnp.zeros(p_bufs[nbuf - 1].shape, BF16)

        def group(t, alpha):
            first = lo + nbuf * t
            for i in range(nbuf):
                j = first + i
                score_into((i + 1) % nbuf, j + 1)
                acc_scr[...] = alpha * acc_scr[...] + pv(jnp.maximum(j - 1, 0), p_bufs[(i - 1) % nbuf])
                alpha = softmax_tile(i)
            return alpha

        groups = (cnt + nbuf - 1) // nbuf
        alpha_last = lax.fori_loop(0, groups, group, jnp.ones((1, cols), F32))
        acc_scr[...] = alpha_last * acc_scr[...] + pv(lo + nbuf * groups - 1, p_bufs[nbuf - 1])
        return carry

    lax.fori_loop(0, (n_tiles + tpw - 1) // tpw, window, 0)
    acc = acc_scr[...]
    o_t = acc[:NSA_HEAD_DIM] / acc[NSA_HEAD_DIM:NSA_HEAD_DIM + 1]
    gates = jax.nn.sigmoid(gl_ref[0])
    lane = lax.broadcasted_iota(jnp.int32, gates.shape, 1)
    for h in range(NSA_HPG):
        col = pl.program_id(0) * NSA_HPG + h

        def gate(branch):
            return jnp.sum(jnp.where(lane == branch * NSA_HEADS + col, gates, 0.0), axis=-1, keepdims=True)

        sl = slice(h * NSA_HEAD_DIM, (h + 1) * NSA_HEAD_DIM)
        o_ref[:, sl] = gate(0) * oc_ref[:, sl] + gate(1) * o_t[:, h * tq:(h + 1) * tq].T + gate(2) * ow_ref[:, sl]


def _sel_attn(zn, zr, k_aug, vt_aug, sel, o_c, o_w, tq, tk):
    T = zn.shape[1]
    ns = sel.shape[1]
    wb = k_aug.shape[2] - NSA_HEAD_DIM
    assert tk % tq == 0 and (wb * SLC_BLOCK) % (SEL_BUFFERS * tk) == 0 and ns % wb == 0 and T % tk == 0
    kern = functools.partial(_sel_attn_kernel, tq=tq, tk=tk, wb=wb)
    cols = NSA_HPG * tq
    wide = pl.BlockSpec((tq, NSA_HPG * NSA_HEAD_DIM), lambda g, i: (i, g))
    return pl.pallas_call(
        kern,
        grid=(NSA_KV_GROUPS, T // tq),
        in_specs=[
            pl.BlockSpec((NSA_HPG, tq, NSA_HEAD_DIM), lambda g, i: (g, i, 0)),
            pl.BlockSpec((1, T, NSA_HEAD_DIM + wb), lambda g, i: (g, 0, 0)),
            pl.BlockSpec((1, NSA_HEAD_DIM + SEL_ONES_ROWS, T), lambda g, i: (g, 0, 0)),
            pl.BlockSpec((1, ns, tq), lambda g, i: (g, 0, i)),
            pl.BlockSpec((1, tq, LANES), lambda g, i: (RW_TILES - 1, i, 0)),
            wide, wide,
        ],
        out_specs=wide,
        out_shape=jax.ShapeDtypeStruct((T, NSA_WIDTH), F32),
        scratch_shapes=[
            pltpu.VMEM((ns, cols), F32),
            pltpu.VMEM((NSA_HEAD_DIM + wb, cols), BF16),
            pltpu.VMEM((NSA_HEAD_DIM + SEL_ONES_ROWS, cols), F32),
            pltpu.VMEM((8 * (1 + SEL_BUFFERS), cols), F32),
        ] + [pltpu.VMEM((tk, cols), F32)] * SEL_BUFFERS + [pltpu.VMEM((tk, cols), BF16)] * SEL_BUFFERS,
        compiler_params=_cparams(("parallel", "arbitrary")),
        name="nsa_sel_attn",
    )(zn, k_aug, vt_aug, sel, zr, o_c, o_w)


def _cmp_to_sel_matrix(ncp, ns):
    cmp_start = np.arange(ncp)[:, None] * CMP_STRIDE
    sel_start = np.arange(ns)[None, :] * SLC_BLOCK
    overlap = np.minimum(cmp_start + CMP_BLOCK, sel_start + SLC_BLOCK) - np.maximum(cmp_start, sel_start)
    return jnp.asarray(np.clip(overlap, 0, None).astype(np.float32).T / CMP_BLOCK, dtype=BF16)


def _nsa(zn, zr, zc, cmp_pos_k, cmp_w1_k, cmp_w2_k, cmp_pos_v, cmp_w1_v, cmp_w2_v, tq, tq_sel, tk,
         sel_window=LANES):
    T = zn.shape[1]
    width = CMP_BLOCK * NSA_HEAD_DIM
    pos = jnp.stack([cmp_pos_k.reshape(1, width), cmp_pos_v.reshape(1, width)])
    pos = jnp.broadcast_to(pos, (2, 8, width)).astype(BF16)
    w1 = jnp.stack([cmp_w1_k, cmp_w1_v]).astype(BF16)
    w2 = jnp.stack([cmp_w2_k, cmp_w2_v]).astype(BF16)
    kv_cmp = _compress(zc, pos, w1, w2)
    ns = T // SLC_BLOCK
    ks0 = NSA_HEADS + 2 * NSA_KV_GROUPS
    vs0 = ks0 + NSA_KV_GROUPS
    vw0 = vs0 + 2 * NSA_KV_GROUPS
    vw_t = jnp.swapaxes(zn[vw0:vw0 + NSA_KV_GROUPS], 1, 2)
    o_c, o_w, sel = _cmp_win(zn, kv_cmp, _cmp_to_sel_matrix(T // CMP_STRIDE, ns), vw_t, tq)
    wb = min(sel_window, ns)
    blk_onehot = (jnp.arange(T)[:, None] // SLC_BLOCK % wb == jnp.arange(wb)[None, :]).astype(BF16)
    k_aug = jnp.concatenate([zn[ks0:vs0], jnp.broadcast_to(blk_onehot, (NSA_KV_GROUPS, T, wb))], axis=2)
    vt_aug = jnp.concatenate([jnp.swapaxes(zn[vs0:vs0 + NSA_KV_GROUPS], 1, 2),
                              jnp.ones((NSA_KV_GROUPS, SEL_ONES_ROWS, T), BF16)], axis=1)
    return _sel_attn(zn, zr, k_aug, vt_aug, sel, o_c, o_w, tq_sel, tk)


def _softplus(y):
    return jnp.maximum(y, 0.0) + jnp.log(1.0 + jnp.exp(-jnp.abs(y)))


def _rwkv_kernel(z_ref, prev_ref, mu_ref, pv_ref, wup_ref, aup_ref, gup_ref, o_ref,
                 rt_s, at_s, kh_s, bh_s, kb_s, bb_s, v_s, gc_s, g_s, bonus_s, y_s, state_s, *, tb):
    C = RWKV_CHUNK
    step = pl.program_id(0)

    @pl.when(step == 0)
    def _():
        state_s[...] = jnp.zeros(state_s.shape, F32)

    has_prev = jnp.where(step > 0, 1.0, 0.0)
    row = lax.broadcasted_iota(jnp.int32, (tb, LANES), 0)

    def shifted_mix(t):
        z = z_ref[t]
        zp = jnp.where(row == 0, prev_ref[t, 7:8, :] * has_prev, pltpu.roll(z, 1, axis=0))
        return z + (zp - z) * mu_ref[t]

    ri = lax.broadcasted_iota(jnp.int32, (tb, tb), 0)
    ci = lax.broadcasted_iota(jnp.int32, (tb, tb), 1)
    tri = jnp.where(((ri >> 6) == (ci >> 6)) & (ci <= ri), 1.0, 0.0).astype(BF16)
    wide = 2 * LANES
    wr = lax.broadcasted_iota(jnp.int32, (wide, wide), 0)
    wc = lax.broadcasted_iota(jnp.int32, (wide, wide), 1)
    head_sum = jnp.where((wr >> 6) == (wc >> 6), 1.0, 0.0).astype(BF16)
    head_mean = jnp.where((wr >> 6) == (wc >> 6), 1.0 / RWKV_HEAD_DIM, 0.0).astype(BF16)

    wa = shifted_mix(3 * RWKV_PAIRS)
    dw = _dot(jnp.tanh(wa).astype(BF16), wup_ref[...])
    da = _dot(wa.astype(BF16), aup_ref[...])
    g0 = jax.nn.sigmoid(shifted_mix(3 * RWKV_PAIRS + 1)).astype(BF16)
    g1 = jax.nn.sigmoid(shifted_mix(3 * RWKV_PAIRS + 2)).astype(BF16)
    g_s[...] = _dot(g0, gup_ref[:LANES]) + _dot(g1, gup_ref[LANES:])

    def mixed(first_tile, q):
        return jnp.concatenate([shifted_mix(first_tile + 2 * q), shifted_mix(first_tile + 2 * q + 1)], axis=1)

    for q in range(RWKV_PAIRS // 2):
        sl = slice(q * wide, (q + 1) * wide)
        r = mixed(0, q)
        k = mixed(RWKV_PAIRS, q)
        v = mixed(2 * RWKV_PAIRS, q)
        w_log = -_softplus(-(pv_ref[0:1, sl] + dw[:, sl])) - 0.5
        lw = -jnp.exp(w_log)
        icl = jax.nn.sigmoid(pv_ref[1:2, sl] + da[:, sl])
        kk = k * pv_ref[2:3, sl]
        kk = kk * lax.rsqrt(jnp.maximum(_dot((kk * kk).astype(BF16), head_sum), 1e-24))
        k2 = k * (1.0 + (icl - 1.0) * pv_ref[3:4, sl])
        bonus_s[:, sl] = _dot((r * k2 * pv_ref[4:5, sl]).astype(BF16), head_sum) * v
        hi, mid, lo = _split3(lw)
        cum = _dot(tri, hi) + _dot(tri, mid) + _dot(tri, lo)
        b = kk * icl
        inv_decay = jnp.exp(-cum)
        to_end = []
        for c in range(tb // C):
            e_end = jnp.exp(cum[c * C + C - 1:c * C + C])
            gc_s[8 * c:8 * c + 1, sl] = e_end
            to_end.append(inv_decay[c * C:(c + 1) * C] * e_end)
        to_end = jnp.concatenate(to_end, axis=0)
        rt_s[:, sl] = (r * jnp.exp(cum)).astype(BF16)
        at_s[:, sl] = (-kk * jnp.exp(cum - lw)).astype(BF16)
        kh_s[:, sl] = (k2 * inv_decay).astype(BF16)
        bh_s[:, sl] = (b * inv_decay).astype(BF16)
        kb_s[:, sl] = (k2 * to_end).astype(BF16)
        bb_s[:, sl] = (b * to_end).astype(BF16)
        v_s[:, sl] = v.astype(BF16)

    hr = lax.broadcasted_iota(jnp.int32, (LANES, LANES), 0)
    hc = lax.broadcasted_iota(jnp.int32, (LANES, LANES), 1)
    same_head = (hr >> 6) == (hc >> 6)
    t_r = hr & (C - 1)
    t_c = hc & (C - 1)
    strict = same_head & (t_c < t_r)
    incl = same_head & (t_c <= t_r)
    eye = jnp.where(hr == hc, 1.0, 0.0)
    lane_head = lax.broadcasted_iota(jnp.int32, (C, LANES), 1) >> 6

    def stack(zc):
        return jnp.concatenate([jnp.where(lane_head == 0, zc, jnp.zeros_like(zc)),
                                jnp.where(lane_head == 1, zc, jnp.zeros_like(zc))], axis=0)

    def chunks(cc, carry):
        group = range(RWKV_CHUNK_GROUP)
        pairs = range(RWKV_PAIRS)
        units = [(d, p) for d in group for p in pairs]
        rows = [pl.ds(pl.multiple_of((cc * RWKV_CHUNK_GROUP + d) * C, C), C) for d in group]
        sls = [slice(p * LANES, (p + 1) * LANES) for p in pairs]
        lhs = {u: jnp.concatenate([stack(at_s[rows[u[0]], sls[u[1]]]), stack(rt_s[rows[u[0]], sls[u[1]]])], axis=0)
               for u in units}
        rhs = {u: jnp.concatenate([stack(kh_s[rows[u[0]], sls[u[1]]]), stack(bh_s[rows[u[0]], sls[u[1]]])], axis=0)
               for u in units}
        aa = {u: _dot_nt(lhs[u], rhs[u]) for u in units}
        a_ak = {u: jnp.where(strict, aa[u][:2 * C, :2 * C], 0.0).astype(BF16) for u in units}
        n_pow = {u: jnp.where(strict, aa[u][:2 * C, 2 * C:], 0.0) for u in units}
        a_r = {u: jnp.concatenate([jnp.where(incl, aa[u][2 * C:, :2 * C], 0.0).astype(BF16),
                                   jnp.where(incl, aa[u][2 * C:, 2 * C:], 0.0).astype(BF16)], axis=1) for u in units}
        t_inv = {u: eye + n_pow[u] for u in units}
        for _ in range(5):
            nb = {u: n_pow[u].astype(BF16) for u in units}
            n_pow = {u: _dot(nb[u], nb[u]) for u in units}
            t_inv = {u: t_inv[u] + _dot(t_inv[u].astype(BF16), n_pow[u].astype(BF16)) for u in units}
        t_inv = {u: t_inv[u].astype(BF16) for u in units}
        state = [state_s[p] for p in pairs]
        for d in group:
            xs = [_dot_nt(lhs[d, p], state[p].astype(BF16)) for p in pairs]
            v_c = [v_s[rows[d], sl] for sl in sls]
            v_st = [stack(v_c[p]) for p in pairs]
            av = [_dot(a_ak[d, p], v_st[p]) for p in pairs]
            sa = [_dot(t_inv[d, p], (xs[p][:2 * C] + av[p]).astype(BF16)) for p in pairs]
            ys = [xs[p][2 * C:] + _dot(a_r[d, p], jnp.concatenate([v_st[p], sa[p].astype(BF16)], axis=0))
                  for p in pairs]
            upd = [_dot_tn(jnp.concatenate([v_c[p], (sa[p][:C] + sa[p][C:]).astype(BF16)], axis=0),
                           jnp.concatenate([kb_s[rows[d], sls[p]], bb_s[rows[d], sls[p]]], axis=0)) for p in pairs]
            decay_row = pl.ds(pl.multiple_of((cc * RWKV_CHUNK_GROUP + d) * 8, 8), 1)
            for p in pairs:
                y_s[rows[d], sls[p]] = ys[p][:C] + ys[p][C:]
            state = [state[p] * gc_s[decay_row, sls[p]] + jnp.where(same_head, upd[p], 0.0) for p in pairs]
        for p in pairs:
            state_s[p] = state[p]
        return carry

    lax.fori_loop(0, tb // (C * RWKV_CHUNK_GROUP), chunks, 0)

    for q in range(RWKV_PAIRS // 2):
        sl = slice(q * wide, (q + 1) * wide)
        y = y_s[:, sl]
        d = y - _dot(y.astype(BF16), head_mean)
        var = _dot((d * d).astype(BF16), head_mean)
        yn = d * lax.rsqrt(var + LNX_EPS) * pv_ref[5:6, sl] + pv_ref[6:7, sl]
        o_ref[:, sl] = (yn + bonus_s[:, sl]) * g_s[:, sl]


def _hi_lo(w):
    hi = w.astype(BF16)
    return hi, (w - hi.astype(F32)).astype(BF16)


def _rwkv(zr, mu_t, w0, w_up, a0, a_up, g_up, k_k, k_a, r_k, lnx_g, lnx_b, tb):
    T = zr.shape[1]
    n_in = RW_TILES - 1
    pv = jnp.stack([w0, a0, k_k, k_a, r_k.reshape(-1), lnx_g, lnx_b, jnp.zeros_like(w0)])
    wup = jnp.pad(w_up, ((0, LANES - W_RANK), (0, 0))).astype(BF16)
    aup = jnp.pad(a_up, ((W_RANK, LANES - W_RANK - A_RANK), (0, 0))).astype(BF16)
    gup = jnp.pad(g_up, ((0, 2 * LANES - G_RANK), (0, 0))).astype(BF16)
    full = lambda shape: pl.BlockSpec(shape, lambda s: (0,) * len(shape))
    kern = functools.partial(_rwkv_kernel, tb=tb)
    bf = lambda: pltpu.VMEM((tb, RWKV_WIDTH), BF16)
    ff = lambda: pltpu.VMEM((tb, RWKV_WIDTH), F32)
    return pl.pallas_call(
        kern,
        grid=(T // tb,),
        in_specs=[
            pl.BlockSpec((n_in, tb, LANES), lambda s: (0, s, 0)),
            pl.BlockSpec((n_in, 8, LANES), lambda s: (0, jnp.maximum(s * (tb // 8) - 1, 0), 0)),
            full((n_in, 1, LANES)),
            full((8, RWKV_WIDTH)),
            full((LANES, RWKV_WIDTH)), full((LANES, RWKV_WIDTH)), full((2 * LANES, RWKV_WIDTH)),
        ],
        out_specs=pl.BlockSpec((tb, RWKV_WIDTH), lambda s: (s, 0)),
        out_shape=jax.ShapeDtypeStruct((T, RWKV_WIDTH), F32),
        scratch_shapes=[bf(), bf(), bf(), bf(), bf(), bf(), bf(),
                        pltpu.VMEM((8 * (tb // RWKV_CHUNK), RWKV_WIDTH), F32), ff(), ff(), ff(),
                        pltpu.VMEM((RWKV_PAIRS, LANES, LANES), F32)],
        compiler_params=_cparams(("arbitrary",)),
        name="rwkv7",
    )(zr, zr, mu_t, pv, wup, aup, gup)


PROJ_COLS = 512


def _mix_out_kernel(ya_ref, yb_ref, g_ref, w_ref, x_ref, o_ref):
    h = jnp.concatenate([_rms(ya_ref[...], g_ref[...]).astype(BF16), yb_ref[...].astype(BF16)], axis=1)
    for n in range(D_MODEL // PROJ_COLS):
        sl = slice(n * PROJ_COLS, (n + 1) * PROJ_COLS)
        o_ref[:, sl] = x_ref[:, sl] + _dot(h, w_ref[:, sl])


def _mix_out(y_a, y_b, g, w, x, tm):
    T = x.shape[0]
    return pl.pallas_call(
        _mix_out_kernel,
        grid=(T // tm,),
        in_specs=[
            pl.BlockSpec((tm, NSA_WIDTH), lambda m: (m, 0)),
            pl.BlockSpec((tm, RWKV_WIDTH), lambda m: (m, 0)),
            _resident((1, NSA_WIDTH)),
            _resident((NSA_WIDTH + RWKV_WIDTH, D_MODEL)),
            pl.BlockSpec((tm, D_MODEL), lambda m: (m, 0)),
        ],
        out_specs=pl.BlockSpec((tm, D_MODEL), lambda m: (m, 0)),
        out_shape=jax.ShapeDtypeStruct((T, D_MODEL), F32),
        compiler_params=_cparams(("parallel",)),
        name="mix_out_proj",
    )(y_a, y_b, g, w, x)


def _norm_mm_kernel(x_ref, g_ref, w_ref, o_ref, h_scr):
    @pl.when(pl.program_id(1) == 0)
    def _():
        h_scr[...] = _rms(x_ref[...], g_ref[...]).astype(BF16)

    o_ref[...] = _dot(h_scr[...], w_ref[...]).astype(o_ref.dtype)


def _norm_mm(x, g, w, tm, tn, name):
    M, K = x.shape
    N = w.shape[1]
    return pl.pallas_call(
        _norm_mm_kernel,
        grid=(M // tm, N // tn),
        in_specs=[
            pl.BlockSpec((tm, K), lambda m, n: (m, 0)),
            pl.BlockSpec((1, K), lambda m, n: (0, 0)),
            pl.BlockSpec((K, tn), lambda m, n: (0, n)),
        ],
        out_specs=pl.BlockSpec((tm, tn), lambda m, n: (m, n)),
        out_shape=jax.ShapeDtypeStruct((M, N), BF16),
        scratch_shapes=[pltpu.VMEM((tm, K), BF16)],
        compiler_params=_cparams(("parallel", "arbitrary")),
        name=name,
    )(x, g, w)


def _mem_router_kernel(x_ref, km_ref, vm_ref, wq_ref, wo_ref, g2_ref, g3_ref, w_hi, w_lo, b_ref,
                       x2_ref, h_ref, c_ref, n_ref):
    scale = MEM_HEAD_DIM ** -0.5
    hq = _rms(x_ref[...], g2_ref[...]).astype(BF16)
    heads = []
    for hd in range(MEM_HEADS):
        sl = slice(hd * MEM_HEAD_DIM, (hd + 1) * MEM_HEAD_DIM)
        q = _dot(hq, wq_ref[:, sl]).astype(BF16)
        s = _dot_nt(q, km_ref[:, sl]) * scale
        e = jnp.exp(s - jnp.max(s, axis=-1, keepdims=True))
        p = e / jnp.sum(e, axis=-1, keepdims=True)
        heads.append(_dot(p.astype(BF16), vm_ref[:, sl]).astype(BF16))
    o = jnp.concatenate(heads, axis=1)
    for n in range(D_MODEL // PROJ_COLS):
        sl = slice(n * PROJ_COLS, (n + 1) * PROJ_COLS)
        x2_ref[:, sl] = x_ref[:, sl] + _dot(o, wo_ref[:, sl])
    _route(x2_ref[...], g3_ref, w_hi, w_lo, b_ref, h_ref, c_ref, n_ref)


def _mem_router(x, k_mem, v_mem, wq, wo, g2, g3, w_hi, w_lo, b, tm):
    T = x.shape[0]
    M = k_mem.shape[0]
    row_block = lambda width: pl.BlockSpec((tm, width), lambda m: (m, 0))
    return pl.pallas_call(
        _mem_router_kernel,
        grid=(T // tm,),
        in_specs=[
            row_block(D_MODEL),
            _resident((M, D_MODEL)), _resident((M, D_MODEL)),
            _resident((D_MODEL, D_MODEL)), _resident((D_MODEL, D_MODEL)),
            _resident((1, D_MODEL)), _resident((1, D_MODEL)),
            _resident((D_MODEL, LANES)), _resident((D_MODEL, LANES)), _resident((1, LANES)),
        ],
        out_specs=[row_block(D_MODEL), row_block(D_MODEL), row_block(LANES),
                   pl.BlockSpec((1, 8, LANES), lambda m: (m, 0, 0))],
        out_shape=[
            jax.ShapeDtypeStruct((T, D_MODEL), F32),
            jax.ShapeDtypeStruct((T, D_MODEL), BF16),
            jax.ShapeDtypeStruct((T, LANES), F32),
            jax.ShapeDtypeStruct((T // tm, 8, LANES), F32),
        ],
        compiler_params=_cparams(("parallel",)),
        name="mem_attn_router",
    )(x, k_mem, v_mem, wq, wo, g2, g3, w_hi, w_lo, b)


def _route(x, g_ref, w_hi, w_lo, b_ref, h_ref, c_ref, n_ref):
    h = _rms(x, g_ref[...])
    h_ref[...] = h.astype(BF16)
    logits = _dot_xw3(h, w_hi[...], w_lo[...]) + b_ref[...]
    lane = lax.broadcasted_iota(jnp.int32, logits.shape, 1)
    big = jnp.int32(LANES)
    is_grp = (lane >= N_EXPERTS) & (lane < N_EXPERTS + N_GROUPS)
    lg = jnp.where(is_grp, logits, NEG)
    eg = jnp.where(is_grp, jnp.exp(lg - jnp.max(lg, axis=-1, keepdims=True)), 0.0)
    pg = eg / jnp.sum(eg, axis=-1, keepdims=True)
    pg_top = jnp.max(pg, axis=-1, keepdims=True)
    g_idx = jnp.min(jnp.where(is_grp & (pg == pg_top), lane - N_EXPERTS, big), axis=-1, keepdims=True)
    in_grp = (lane < N_EXPERTS) & ((lane >> 3) == g_idx)
    le = jnp.where(in_grp, logits, NEG)
    ee = jnp.where(in_grp, jnp.exp(le - jnp.max(le, axis=-1, keepdims=True)), 0.0)
    pe = jnp.where(in_grp, ee / jnp.sum(ee, axis=-1, keepdims=True), -1.0)
    p1 = jnp.max(pe, axis=-1, keepdims=True)
    hit1 = lane == jnp.min(jnp.where(pe == p1, lane, big), axis=-1, keepdims=True)
    pe2 = jnp.where(hit1, -1.0, pe)
    p2 = jnp.max(pe2, axis=-1, keepdims=True)
    hit2 = lane == jnp.min(jnp.where(pe2 == p2, lane, big), axis=-1, keepdims=True)
    denom = p1 + p2
    comb = jnp.where(hit1, pg_top * p1 / denom, 0.0) + jnp.where(hit2, pg_top * p2 / denom, 0.0)
    c_ref[...] = jnp.where(lane == GROUP_LANE, g_idx.astype(F32), comb)
    in_group = jnp.where(lane == g_idx, 1.0, 0.0)
    n_ref[0] = jnp.broadcast_to(jnp.sum(in_group, axis=0, keepdims=True), n_ref.shape[1:])


def _moe_kernel(meta_ref, h_ref, c_ref, wg_ref, wu_ref, wd_ref, y_ref, hs, cs, ys, slot_scr, perm, *, tm, sub):
    m = pl.program_id(0)
    e = pl.program_id(1)
    nslot = hs.shape[0]

    @pl.when(e == 0)
    def _():
        c = c_ref[...]
        lane = lax.broadcasted_iota(jnp.int32, c.shape, 1)
        one_hot = jnp.where(lane.astype(F32) == c[:, GROUP_LANE:GROUP_LANE + 1], 1.0, 0.0)
        earlier = jnp.where(lax.broadcasted_iota(jnp.int32, (tm, tm), 1)
                            < lax.broadcasted_iota(jnp.int32, (tm, tm), 0), 1.0, 0.0).astype(BF16)
        rank = _dot(earlier, one_hot.astype(BF16))
        start = jnp.zeros((1, LANES), F32)
        for g in range(N_GROUPS):
            start = jnp.where(lane[0:1] == g, (meta_ref[m * 2 * N_GROUPS + g] * sub).astype(F32), start)
        slot = jnp.sum(one_hot * (rank + start), axis=-1, keepdims=True)
        slot_scr[...] = jnp.broadcast_to(slot, slot_scr.shape)
        slot_row = slot_scr[...].T[0:1, :]
        perm[...] = jnp.where(lax.broadcasted_iota(jnp.int32, (nslot, tm), 0).astype(F32) == slot_row,
                              1.0, 0.0).astype(BF16)
        hs[...] = _dot(perm[...], h_ref[...]).astype(BF16)
        w_hi, w_lo = _split2(jnp.where(lane < N_EXPERTS, c, 0.0))
        cs[...] = _dot(perm[...], w_hi) + _dot(perm[...], w_lo)
        ys[...] = jnp.zeros(ys.shape, F32)

    grp = e // EXPERTS_PER_GROUP
    first = meta_ref[m * 2 * N_GROUPS + grp]
    count = meta_ref[m * 2 * N_GROUPS + N_GROUPS + grp]
    def ffn(start_block, n_blocks):
        n = n_blocks * sub
        rows = pl.ds(pl.multiple_of(start_block * sub, sub), n)
        x = hs[rows, :]
        hid = jax.nn.silu(_dot(x, wg_ref[0])) * _dot(x, wu_ref[0])
        lane_s = lax.broadcasted_iota(jnp.int32, (n, LANES), 1)
        c_e = jnp.sum(jnp.where(lane_s == e, cs[rows, :], 0.0), axis=-1, keepdims=True)
        ys[rows, :] += _dot((hid * c_e).astype(BF16), wd_ref[0])

    def quad(s, carry):
        ffn(first + 4 * s, 4)
        return carry

    quads = count // 4
    lax.fori_loop(0, quads, quad, 0)
    rest = count - 4 * quads

    @pl.when(rest >= 2)
    def _():
        ffn(first + 4 * quads, 2)

    @pl.when(rest % 2 == 1)
    def _():
        ffn(first + count - 1, 1)

    @pl.when(e == pl.num_programs(1) - 1)
    def _():
        back = jnp.where(lax.broadcasted_iota(jnp.int32, (tm, nslot), 1).astype(F32) == slot_scr[:, 0:1],
                         1.0, 0.0).astype(BF16)
        for n in range(D_MODEL // PROJ_COLS):
            sl = slice(n * PROJ_COLS, (n + 1) * PROJ_COLS)
            y_ref[:, sl] = _dot(back, ys[:, sl].astype(BF16)).astype(y_ref.dtype)


def _moe(h, comb, counts, wg, wu, wd, tm, sub):
    T = h.shape[0]
    nt = T // tm
    nslot = tm + N_GROUPS * sub
    cnt = counts[:, 0, :N_GROUPS].astype(jnp.int32).reshape(nt, -1, N_GROUPS).sum(axis=1)
    nblk = (cnt + sub - 1) // sub
    first = jnp.cumsum(nblk, axis=1) - nblk
    meta = jnp.concatenate([first, nblk], axis=1).reshape(-1)
    once = lambda shape: pl.BlockSpec(shape, lambda m, e, meta: (m, 0), pipeline_mode=pl.Buffered(1))
    kern = functools.partial(_moe_kernel, tm=tm, sub=sub)
    return pl.pallas_call(
        kern,
        grid_spec=pltpu.PrefetchScalarGridSpec(
            num_scalar_prefetch=1,
            grid=(nt, N_EXPERTS),
            in_specs=[
                once((tm, D_MODEL)),
                once((tm, LANES)),
                pl.BlockSpec((1, D_MODEL, EXPERT_FF), lambda m, e, meta: (e, 0, 0)),
                pl.BlockSpec((1, D_MODEL, EXPERT_FF), lambda m, e, meta: (e, 0, 0)),
                pl.BlockSpec((1, EXPERT_FF, D_MODEL), lambda m, e, meta: (e, 0, 0)),
            ],
            out_specs=pl.BlockSpec((tm, D_MODEL), lambda m, e, meta: (m, 0)),
            scratch_shapes=[
                pltpu.VMEM((nslot, D_MODEL), BF16),
                pltpu.VMEM((nslot, LANES), F32),
                pltpu.VMEM((nslot, D_MODEL), F32),
                pltpu.VMEM((tm, LANES), F32),
                pltpu.VMEM((nslot, tm), BF16),
            ],
        ),
        out_shape=jax.ShapeDtypeStruct((T, D_MODEL), BF16),
        compiler_params=_cparams(("parallel", "arbitrary")),
        name="moe_experts",
    )(meta, h, comb, wg, wu, wd)


def _final_norm_kernel(x_ref, y_ref, g_ref, o_ref):
    o_ref[...] = _rms(x_ref[...] + y_ref[...].astype(F32), g_ref[...])


def _final_norm(x, y, g, tm):
    T = x.shape[0]
    blk = pl.BlockSpec((tm, D_MODEL), lambda m: (m, 0))
    return pl.pallas_call(
        _final_norm_kernel,
        grid=(T // tm,),
        in_specs=[blk, blk, _resident((1, D_MODEL))],
        out_specs=blk,
        out_shape=jax.ShapeDtypeStruct((T, D_MODEL), F32),
        compiler_params=_cparams(("parallel",)),
        name="moe_residual_norm",
    )(x, y, g)


def _pack_w_in(w_in, rwkv_mu):
    nsa_cols = NSA_WIDTH + 6 * NSA_KV_WIDTH
    gl = w_in[:, nsa_cols:nsa_cols + 3 * NSA_HEADS]
    rw0 = nsa_cols + 3 * NSA_HEADS
    rkv = w_in[:, rw0:rw0 + 3 * RWKV_WIDTH + W_RANK + A_RANK]
    gd = w_in[:, rw0 + 3 * RWKV_WIDTH + W_RANK + A_RANK:]

    def pad(a, n):
        return jnp.pad(a, ((0, 0), (0, n - a.shape[1])))

    w = jnp.concatenate([w_in[:, :nsa_cols], rkv, pad(gd, 2 * LANES), pad(gl, LANES)], axis=1)
    mu = rwkv_mu.reshape(1, -1)
    mu_rkv = mu[:, :3 * RWKV_WIDTH + W_RANK + A_RANK]
    mu_gd = pad(mu[:, 3 * RWKV_WIDTH + W_RANK + A_RANK:], 2 * LANES)
    mu_t = jnp.concatenate([mu_rkv, mu_gd], axis=1).reshape(RW_TILES - 1, 1, LANES)
    return w.astype(BF16), mu_t


def kernel(x, mem, ln1_g, w_in, cmp_pos_k, cmp_w1_k, cmp_w2_k, cmp_pos_v, cmp_w1_v, cmp_w2_v, nsa_norm_g, rwkv_mu, rwkv_w0, rwkv_w_up, rwkv_a0, rwkv_a_up, rwkv_g_up, rwkv_k_k, rwkv_k_a, rwkv_r_k, rwkv_lnx_g, rwkv_lnx_b, w_out, ln_mem_g, ln2_g, wq_mem, wk_mem, wv_mem, wo_mem, ln3_g, router_group_w, router_group_b, router_expert_w, router_expert_b, moe_w_gate, moe_w_up, moe_w_down, lnf_g):
    B, T, _ = x.shape
    assert B == 1 and T % max(TM_PROJ, TM_MEM, TQ_SEL, TK_SEL, TB_RWKV, TM_MOE) == 0
    assert w_in.shape[0] == 1
    row = lambda a: a.reshape(1, -1)
    xs = x[0]
    for l in range(w_in.shape[0]):
        wp, mu_t = _pack_w_in(w_in[l], rwkv_mu[l])
        zn, zr, zc = _in_proj(xs, row(ln1_g[l]), wp, TM_PROJ)
        y_a = _nsa(zn, zr, zc, cmp_pos_k[l], cmp_w1_k[l], cmp_w2_k[l], cmp_pos_v[l], cmp_w1_v[l], cmp_w2_v[l],
                   TQ_NSA, TQ_SEL, TK_SEL)
        y_b = _rwkv(zr, mu_t, rwkv_w0[l], rwkv_w_up[l], rwkv_a0[l], rwkv_a_up[l], rwkv_g_up[l], rwkv_k_k[l],
                    rwkv_k_a[l], rwkv_r_k[l], rwkv_lnx_g[l], rwkv_lnx_b[l], TB_RWKV)
        xs = _mix_out(y_a, y_b, row(nsa_norm_g[l]), w_out[l].astype(BF16), xs, TM_MEM)
        m_tok = mem[0]
        k_mem = _norm_mm(m_tok, row(ln_mem_g[l]), wk_mem[l].astype(BF16), m_tok.shape[0], PROJ_COLS, "mem_k")
        v_mem = _norm_mm(m_tok, row(ln_mem_g[l]), wv_mem[l].astype(BF16), m_tok.shape[0], PROJ_COLS, "mem_v")
        w_r = jnp.pad(jnp.concatenate([router_expert_w[l], router_group_w[l]], axis=1),
                      ((0, 0), (0, LANES - N_EXPERTS - N_GROUPS)))
        b_r = jnp.pad(jnp.concatenate([router_expert_b[l], router_group_b[l]]), (0, LANES - N_EXPERTS - N_GROUPS))
        xs, h3, comb, counts = _mem_router(xs, k_mem, v_mem, wq_mem[l].astype(BF16), wo_mem[l].astype(BF16),
                                   row(ln2_g[l]), row(ln3_g[l]), *_hi_lo(w_r), row(b_r), TM_MEM)
        y_moe = _moe(h3, comb, counts, moe_w_gate[l].astype(BF16), moe_w_up[l].astype(BF16),
                     moe_w_down[l].astype(BF16), TM_MOE, SUB_MOE)
        xs = _final_norm(xs, y_moe, row(lnf_g), TM_PROJ)
    return xs[None]
```

```python
import functools

import numpy as np
import jax
import jax.numpy as jnp
from jax import lax
from jax.experimental import pallas as pl
from jax.experimental.pallas import tpu as pltpu

F32 = jnp.float32
BF16 = jnp.bfloat16

LANES = 128
D_MODEL = 2048
EPS = 1e-6
NSA_HEAD_DIM = 128
NSA_HEADS = 8
NSA_KV_GROUPS = 2
NSA_HPG = NSA_HEADS // NSA_KV_GROUPS
NSA_WIDTH = NSA_HEADS * NSA_HEAD_DIM
NSA_KV_WIDTH = NSA_KV_GROUPS * NSA_HEAD_DIM
CMP_BLOCK = 32
CMP_STRIDE = 16
SLC_BLOCK = 64
SLC_TOPK = 16
WINDOW = 512
NEG = -1e30
RWKV_HEAD_DIM = 64
RWKV_HEADS = 16
RWKV_WIDTH = RWKV_HEADS * RWKV_HEAD_DIM
RWKV_PAIRS = RWKV_WIDTH // LANES
W_RANK = 64
A_RANK = 64
G_RANK = 160
LNX_EPS = 64e-5
RWKV_CHUNK = 64
RWKV_CHUNK_GROUP = 4
MEM_HEADS = 4
MEM_HEAD_DIM = D_MODEL // MEM_HEADS
N_GROUPS = 4
EXPERTS_PER_GROUP = 8
N_EXPERTS = N_GROUPS * EXPERTS_PER_GROUP
EXPERT_FF = 256

NSA_TILES = 20
RW_TILES = 28
IN_TILE_BLOCK = 4

VMEM_LIMIT = 56 * 1024 * 1024

TM_PROJ = 256
TM_MEM = 512
TQ_NSA = 256
TQ_SEL = 512
TK_SEL = 512
TB_RWKV = 256
TM_MOE = 1024
SUB_MOE = 64
GROUP_LANE = 64


def _cparams(sem):
    return pltpu.CompilerParams(dimension_semantics=sem, vmem_limit_bytes=VMEM_LIMIT)


def _dot(a, b):
    return jnp.dot(a, b, preferred_element_type=F32)


def _dot_nt(a, b):
    return lax.dot_general(a, b, (((1,), (1,)), ((), ())), preferred_element_type=F32)


def _dot_tn(a, b):
    return lax.dot_general(a, b, (((0,), (0,)), ((), ())), preferred_element_type=F32)


def _split2(x):
    hi = x.astype(BF16)
    lo = (x - hi.astype(F32)).astype(BF16)
    return hi, lo


def _split3(x):
    hi = x.astype(BF16)
    r1 = x - hi.astype(F32)
    mid = r1.astype(BF16)
    lo = (r1 - mid.astype(F32)).astype(BF16)
    return hi, mid, lo


def _dot_x2(x, w):
    hi, lo = _split2(x)
    return _dot(hi, w) + _dot(lo, w)


def _dot_x3(x, w):
    hi, mid, lo = _split3(x)
    return _dot(hi, w) + _dot(mid, w) + _dot(lo, w)


def _dot_xw3(x, w_hi, w_lo):
    hi, lo = _split2(x)
    return _dot(hi, w_hi) + _dot(lo, w_hi) + _dot(hi, w_lo)


def _rms(x, g):
    ms = jnp.mean(x * x, axis=-1, keepdims=True)
    return x * lax.rsqrt(ms + EPS) * g


def _resident(shape):
    return pl.BlockSpec(shape, lambda *_: (0,) * len(shape), pipeline_mode=pl.Buffered(1))


CMP_TILES = 2 * NSA_KV_GROUPS


KS_TILE0 = NSA_HEADS + 2 * NSA_KV_GROUPS
SEL_ONES_ROWS = 16


def _in_proj_kernel(x_ref, g_ref, w_ref, zn_ref, zr_ref, zc_ref, ka_ref, vst_ref, vwt_ref, *, q_scale):
    tm = x_ref.shape[0]
    wb = ka_ref.shape[2] - NSA_HEAD_DIM
    h = _rms(x_ref[...], g_ref[...]).astype(BF16)
    nb = IN_TILE_BLOCK * LANES
    for n in range((NSA_TILES + RW_TILES) // IN_TILE_BLOCK):
        z = _dot(h, w_ref[:, n * nb:(n + 1) * nb])
        for t in range(IN_TILE_BLOCK):
            tile = n * IN_TILE_BLOCK + t
            zt = z[:, t * LANES:(t + 1) * LANES]
            if tile < NSA_HEADS:
                zn_ref[tile] = (zt * q_scale).astype(BF16)
            elif tile < NSA_TILES:
                zn_ref[tile] = zt.astype(BF16)
                if tile < NSA_HEADS + CMP_TILES:
                    zc_ref[tile - NSA_HEADS] = zt
                elif tile < KS_TILE0 + NSA_KV_GROUPS:
                    ka_ref[tile - KS_TILE0, :, :NSA_HEAD_DIM] = zt.astype(BF16)
                elif tile < KS_TILE0 + 2 * NSA_KV_GROUPS:
                    vst_ref[tile - KS_TILE0 - NSA_KV_GROUPS, :NSA_HEAD_DIM, :] = zt.T.astype(BF16)
                elif tile >= KS_TILE0 + 3 * NSA_KV_GROUPS:
                    vwt_ref[tile - KS_TILE0 - 3 * NSA_KV_GROUPS] = zt.T.astype(BF16)
            else:
                zr_ref[tile - NSA_TILES] = zt
    blk = ((pl.program_id(0) * tm + lax.broadcasted_iota(jnp.int32, (tm, wb), 0)) >> 6) & (wb - 1)
    one_hot = jnp.where(blk == lax.broadcasted_iota(jnp.int32, (tm, wb), 1), 1.0, 0.0).astype(BF16)
    for g in range(NSA_KV_GROUPS):
        ka_ref[g, :, NSA_HEAD_DIM:] = one_hot
        vst_ref[g, NSA_HEAD_DIM:, :] = jnp.ones((SEL_ONES_ROWS, tm), BF16)


def _in_proj(x, g, w, tm, wb):
    T = x.shape[0]
    assert wb & (wb - 1) == 0
    kern = functools.partial(_in_proj_kernel, q_scale=NSA_HEAD_DIM ** -0.5)
    tiles = lambda n: pl.BlockSpec((n, tm, LANES), lambda m: (0, m, 0))
    return pl.pallas_call(
        kern,
        grid=(T // tm,),
        in_specs=[
            pl.BlockSpec((tm, D_MODEL), lambda m: (m, 0)),
            _resident((1, D_MODEL)),
            _resident((D_MODEL, (NSA_TILES + RW_TILES) * LANES)),
        ],
        out_specs=[
            tiles(NSA_TILES), tiles(RW_TILES), tiles(CMP_TILES),
            pl.BlockSpec((NSA_KV_GROUPS, tm, NSA_HEAD_DIM + wb), lambda m: (0, m, 0)),
            pl.BlockSpec((NSA_KV_GROUPS, NSA_HEAD_DIM + SEL_ONES_ROWS, tm), lambda m: (0, 0, m)),
            pl.BlockSpec((NSA_KV_GROUPS, NSA_HEAD_DIM, tm), lambda m: (0, 0, m)),
        ],
        out_shape=[
            jax.ShapeDtypeStruct((NSA_TILES, T, LANES), BF16),
            jax.ShapeDtypeStruct((RW_TILES, T, LANES), F32),
            jax.ShapeDtypeStruct((CMP_TILES, T, LANES), F32),
            jax.ShapeDtypeStruct((NSA_KV_GROUPS, T, NSA_HEAD_DIM + wb), BF16),
            jax.ShapeDtypeStruct((NSA_KV_GROUPS, NSA_HEAD_DIM + SEL_ONES_ROWS, T), BF16),
            jax.ShapeDtypeStruct((NSA_KV_GROUPS, NSA_HEAD_DIM, T), BF16),
        ],
        compiler_params=_cparams(("parallel",)),
        name="in_proj",
    )(x, g, w)


def _compress_kernel(x_ref, pos_ref, w1_ref, w2_ref, o_ref):
    n = x_ref.shape[1] // CMP_STRIDE
    w1 = w1_ref[0]
    h_first = jnp.zeros((n, NSA_HEAD_DIM), F32)
    h_second = jnp.zeros((n, NSA_HEAD_DIM), F32)
    for t in range(CMP_STRIDE):
        x_t = x_ref[0, pl.ds(t, n, stride=CMP_STRIDE), :].astype(BF16)
        h_first += _dot(x_t, w1[t * NSA_HEAD_DIM:(t + 1) * NSA_HEAD_DIM])
        h_second += _dot(x_t, w1[(CMP_STRIDE + t) * NSA_HEAD_DIM:(CMP_STRIDE + t + 1) * NSA_HEAD_DIM])
    h_next = pltpu.roll(h_second, n - 1, axis=0)
    bias = _dot(pos_ref[0], w1)[0:1]
    hid = jax.nn.gelu(h_first + h_next + bias)
    o_ref[0] = _dot(hid.astype(BF16), w2_ref[0]).astype(o_ref.dtype)


def _compress(zc, pos, w1, w2):
    T = zc.shape[1]
    n = T // CMP_STRIDE
    width = CMP_BLOCK * NSA_HEAD_DIM
    return pl.pallas_call(
        _compress_kernel,
        grid=(4,),
        in_specs=[
            pl.BlockSpec((1, T, NSA_HEAD_DIM), lambda s: (s, 0, 0)),
            pl.BlockSpec((1, 8, width), lambda s: (s // 2, 0, 0)),
            pl.BlockSpec((1, width, NSA_HEAD_DIM), lambda s: (s // 2, 0, 0)),
            pl.BlockSpec((1, NSA_HEAD_DIM, NSA_HEAD_DIM), lambda s: (s // 2, 0, 0)),
        ],
        out_specs=pl.BlockSpec((1, n, NSA_HEAD_DIM), lambda s: (s, 0, 0)),
        out_shape=jax.ShapeDtypeStruct((4, n, NSA_HEAD_DIM), BF16),
        compiler_params=_cparams(("parallel",)),
        name="nsa_compress",
    )(zc, pos, w1, w2)


def _cmp_win_kernel(q_ref, kc_ref, lhs_ref, kw_ref, vwt_ref, oc_ref, ow_ref, sel_ref, s_scr, acc_scr,
                    *, tq, top_n, ch):
    q0 = pl.program_id(1) * tq
    cols = NSA_HPG * tq
    ncp = kc_ref.shape[1]
    ns = lhs_ref.shape[1] - NSA_HEAD_DIM - SEL_ONES_ROWS
    q_t = jnp.concatenate([q_ref[h].astype(F32).T.astype(BF16) for h in range(NSA_HPG)], axis=1)
    qpos = q0 + (lax.broadcasted_iota(jnp.int32, (1, cols), 1) & (tq - 1))

    def store_heads(o_ref, o_t):
        for h in range(NSA_HPG):
            o_ref[:, h * NSA_HEAD_DIM:(h + 1) * NSA_HEAD_DIM] = o_t[:, h * tq:(h + 1) * tq].T

    span = WINDOW + tq
    start = pl.multiple_of(jnp.maximum(q0 - WINDOW, 0), tq)
    kpos = start + lax.broadcasted_iota(jnp.int32, (span, 1), 0)
    in_window = (qpos - kpos).astype(jnp.uint32) < jnp.uint32(WINDOW)
    s_w = jnp.where(in_window, _dot(kw_ref[0, pl.ds(start, span), :], q_t), NEG)
    e_w = jnp.exp(s_w - jnp.max(s_w, axis=0, keepdims=True))
    p_w = e_w / jnp.sum(e_w, axis=0, keepdims=True)
    store_heads(ow_ref, _dot(vwt_ref[0, :, pl.ds(start, span)], p_w.astype(BF16)))

    visible = (q0 + tq - 1 - (CMP_BLOCK - 1)) // CMP_STRIDE + 1
    clean = jnp.maximum((q0 - (CMP_BLOCK - 1)) // CMP_STRIDE + 1, 0) // ch
    n_chunks = (visible + ch - 1) // ch

    def chunk_rows(c):
        return pl.ds(pl.multiple_of(c * ch, ch), ch)

    def score_chunk(c, m, masked):
        s = _dot(kc_ref[0, chunk_rows(c), :], q_t)
        if masked:
            cmp_end = (c * ch + lax.broadcasted_iota(jnp.int32, (ch, 1), 0)) * CMP_STRIDE + (CMP_BLOCK - 1)
            s = jnp.where(cmp_end <= qpos, s, NEG)
        s_scr[chunk_rows(c), :] = s
        return jnp.maximum(m, jnp.max(s, axis=0, keepdims=True))

    m = jnp.full((1, cols), NEG, F32)
    m = lax.fori_loop(0, clean, lambda c, m: score_chunk(c, m, False), m)
    m = lax.fori_loop(clean, n_chunks, lambda c, m: score_chunk(c, m, True), m)
    acc_scr[...] = jnp.zeros(acc_scr.shape, F32)

    def weigh_chunk(c, carry):
        e = jnp.exp(s_scr[chunk_rows(c), :] - m).astype(BF16)
        acc_scr[...] += _dot(lhs_ref[0, :, chunk_rows(c)], e)
        return carry

    lax.fori_loop(0, n_chunks, weigh_chunk, 0)
    acc = acc_scr[...]
    sees_any = jnp.where(qpos >= CMP_BLOCK - 1, 1.0, 0.0)
    r_inv = sees_any / jnp.maximum(acc[NSA_HEAD_DIM + ns:NSA_HEAD_DIM + ns + 1], 1e-30)
    store_heads(oc_ref, acc[:NSA_HEAD_DIM] * r_inv)
    weighted = acc[NSA_HEAD_DIM:NSA_HEAD_DIM + ns] * r_inv
    p_sel = weighted[:, 0:tq]
    for h in range(1, NSA_HPG):
        p_sel = p_sel + weighted[:, h * tq:(h + 1) * tq]
    blk = (q0 + lax.broadcasted_iota(jnp.int32, (1, tq), 1)) >> 6
    j = lax.broadcasted_iota(jnp.int32, (ns, tq), 0)
    future = j > blk
    forced = (j == 0) | (j == blk) | (j == blk - 1)
    score = jnp.where(future, -1.0, jnp.where(forced, 1e6, p_sel))
    sel = jnp.zeros((ns, tq), F32)
    for _ in range(top_n):
        best = jnp.max(score, axis=0, keepdims=True)
        idx = jnp.min(jnp.where(score == best, j, ns), axis=0, keepdims=True)
        hit = j == idx
        sel = jnp.where(hit, jnp.where(best >= 0.0, 1.0, 0.0), sel)
        score = jnp.where(hit, -2.0, score)
    sel_ref[0] = sel.astype(sel_ref.dtype)


CMP_CHUNK = 256


def _cmp_win(zn, kv_cmp, cmp_to_sel, vw_t, tq):
    T = zn.shape[1]
    ncp = kv_cmp.shape[1]
    ns = cmp_to_sel.shape[0]
    ch = min(CMP_CHUNK, ncp)
    assert ncp % ch == 0
    kw_tile0 = NSA_HEADS + 4 * NSA_KV_GROUPS
    lhs = jnp.concatenate([jnp.swapaxes(kv_cmp[NSA_KV_GROUPS:], 1, 2),
                           jnp.broadcast_to(cmp_to_sel, (NSA_KV_GROUPS, ns, ncp)),
                           jnp.ones((NSA_KV_GROUPS, SEL_ONES_ROWS, ncp), BF16)], axis=1)
    rows = lhs.shape[1]
    cols = NSA_HPG * tq
    kern = functools.partial(_cmp_win_kernel, tq=tq, top_n=min(SLC_TOPK, ns), ch=ch)
    wide = pl.BlockSpec((tq, NSA_HPG * NSA_HEAD_DIM), lambda g, i: (i, g))
    return pl.pallas_call(
        kern,
        grid=(NSA_KV_GROUPS, T // tq),
        in_specs=[
            pl.BlockSpec((NSA_HPG, tq, NSA_HEAD_DIM), lambda g, i: (g, i, 0)),
            pl.BlockSpec((1, ncp, NSA_HEAD_DIM), lambda g, i: (g, 0, 0)),
            pl.BlockSpec((1, rows, ncp), lambda g, i: (g, 0, 0)),
            pl.BlockSpec((1, T, NSA_HEAD_DIM), lambda g, i: (kw_tile0 + g, 0, 0)),
            pl.BlockSpec((1, NSA_HEAD_DIM, T), lambda g, i: (g, 0, 0)),
        ],
        out_specs=[wide, wide, pl.BlockSpec((1, ns, tq), lambda g, i: (g, 0, i))],
        out_shape=[
            jax.ShapeDtypeStruct((T, NSA_WIDTH), F32),
            jax.ShapeDtypeStruct((T, NSA_WIDTH), F32),
            jax.ShapeDtypeStruct((NSA_KV_GROUPS, ns, T), BF16),
        ],
        scratch_shapes=[pltpu.VMEM((ncp, cols), F32), pltpu.VMEM((rows, cols), F32)],
        compiler_params=_cparams(("parallel", "arbitrary")),
        name="nsa_cmp_window",
    )(zn, kv_cmp, lhs, zn, vw_t)


SEL_BUFFERS = 2


def _sel_attn_kernel(q_ref, k_ref, vt_ref, sel_ref, gl_ref, oc_ref, ow_ref, o_ref,
                     bias_scr, qa_scr, acc_scr, m_scr, *bufs, tq, tk, wb):
    s_bufs, p_bufs = bufs[:SEL_BUFFERS], bufs[SEL_BUFFERS:]
    q0 = pl.program_id(1) * tq
    cols = NSA_HPG * tq
    tpw = wb * SLC_BLOCK // tk
    last_tile = k_ref.shape[1] // tk - 1
    ns = sel_ref.shape[1]
    first_own = q0 // SLC_BLOCK
    not_chosen = (sel_ref[0].astype(F32) - 1.0) * (-NEG)
    for h in range(NSA_HPG):
        bias_scr[:, h * tq:(h + 1) * tq] = not_chosen
        qa_scr[:NSA_HEAD_DIM, h * tq:(h + 1) * tq] = q_ref[h].astype(F32).T.astype(BF16)

    def set_window(w):
        rows = pl.ds(pl.multiple_of(w * wb, wb), wb)
        before = w * wb + lax.broadcasted_iota(jnp.int32, (wb, 1), 0) < first_own
        qa_scr[NSA_HEAD_DIM:, :] = jnp.where(before, bias_scr[rows, :], NEG).astype(BF16)

    def scores(j):
        j = jnp.minimum(j, last_tile)
        return _dot(k_ref[0, pl.ds(pl.multiple_of(j * tk, tk), tk), :], qa_scr[...])

    def pv(j, p_ref):
        return _dot(vt_ref[0, :, pl.ds(pl.multiple_of(j * tk, tk), tk)], p_ref[...])

    def score_into(i, j):
        s = scores(j)
        s_bufs[i][...] = s
        m_scr[8 * (i + 1):8 * (i + 1) + 1, :] = jnp.max(s, axis=0, keepdims=True)

    def softmax_tile(i):
        m_old = m_scr[0:1, :]
        m_new = jnp.maximum(m_old, m_scr[8 * (i + 1):8 * (i + 1) + 1, :])
        m_scr[0:1, :] = m_new
        p_bufs[i][...] = jnp.exp((s_bufs[i][...] - m_new).astype(BF16))
        return jnp.exp(m_old - m_new)

    own = pl.ds(pl.multiple_of(q0, tq), tq)
    s = _dot(k_ref[0, own, :NSA_HEAD_DIM], qa_scr[:NSA_HEAD_DIM, :])
    s = jnp.concatenate([s[b * SLC_BLOCK:(b + 1) * SLC_BLOCK] + bias_scr[pl.ds(first_own + b, 1), :]
                         for b in range(tq // SLC_BLOCK)], axis=0)
    kpos = lax.broadcasted_iota(jnp.int32, (tq, 1), 0)
    qpos = lax.broadcasted_iota(jnp.int32, (1, cols), 1) & (tq - 1)
    s = jnp.where(kpos <= qpos, s, NEG)
    m_first = jnp.max(s, axis=0, keepdims=True)
    m_scr[0:1, :] = m_first
    acc_scr[...] = _dot(vt_ref[0, :, own], jnp.exp((s - m_first).astype(BF16)))

    n_tiles = (q0 + tk - 1) // tk

    nbuf = len(s_bufs)

    def window(w, carry):
        lo = w * tpw
        cnt = jnp.minimum(n_tiles - lo, tpw)
        set_window(w)
        score_into(0, lo)
        p_bufs[nbuf - 1][...] = jnp.zeros(p_bufs[nbuf - 1].shape, BF16)

        def group(t, alpha):
            first = lo + nbuf * t
            for i in range(nbuf):
                j = first + i
                score_into((i + 1) % nbuf, j + 1)
                acc_scr[...] = alpha * acc_scr[...] + pv(jnp.maximum(j - 1, 0), p_bufs[(i - 1) % nbuf])
                alpha = softmax_tile(i)
            return alpha

        groups = (cnt + nbuf - 1) // nbuf
        alpha_last = lax.fori_loop(0, groups, group, jnp.ones((1, cols), F32))
        acc_scr[...] = alpha_last * acc_scr[...] + pv(lo + nbuf * groups - 1, p_bufs[nbuf - 1])
        return carry

    lax.fori_loop(0, (n_tiles + tpw - 1) // tpw, window, 0)
    acc = acc_scr[...]
    o_t = acc[:NSA_HEAD_DIM] / acc[NSA_HEAD_DIM:NSA_HEAD_DIM + 1]
    gates = jax.nn.sigmoid(gl_ref[0])
    lane = lax.broadcasted_iota(jnp.int32, gates.shape, 1)
    for h in range(NSA_HPG):
        col = pl.program_id(0) * NSA_HPG + h

        def gate(branch):
            return jnp.sum(jnp.where(lane == branch * NSA_HEADS + col, gates, 0.0), axis=-1, keepdims=True)

        sl = slice(h * NSA_HEAD_DIM, (h + 1) * NSA_HEAD_DIM)
        o_ref[:, sl] = gate(0) * oc_ref[:, sl] + gate(1) * o_t[:, h * tq:(h + 1) * tq].T + gate(2) * ow_ref[:, sl]


def _sel_attn(zn, zr, k_aug, vt_aug, sel, o_c, o_w, tq, tk):
    T = zn.shape[1]
    ns = sel.shape[1]
    wb = k_aug.shape[2] - NSA_HEAD_DIM
    assert tk % tq == 0 and (wb * SLC_BLOCK) % (SEL_BUFFERS * tk) == 0 and ns % wb == 0 and T % tk == 0
    kern = functools.partial(_sel_attn_kernel, tq=tq, tk=tk, wb=wb)
    cols = NSA_HPG * tq
    wide = pl.BlockSpec((tq, NSA_HPG * NSA_HEAD_DIM), lambda g, i: (i, g))
    return pl.pallas_call(
        kern,
        grid=(NSA_KV_GROUPS, T // tq),
        in_specs=[
            pl.BlockSpec((NSA_HPG, tq, NSA_HEAD_DIM), lambda g, i: (g, i, 0)),
            pl.BlockSpec((1, T, NSA_HEAD_DIM + wb), lambda g, i: (g, 0, 0)),
            pl.BlockSpec((1, NSA_HEAD_DIM + SEL_ONES_ROWS, T), lambda g, i: (g, 0, 0)),
            pl.BlockSpec((1, ns, tq), lambda g, i: (g, 0, i)),
            pl.BlockSpec((1, tq, LANES), lambda g, i: (RW_TILES - 1, i, 0)),
            wide, wide,
        ],
        out_specs=wide,
        out_shape=jax.ShapeDtypeStruct((T, NSA_WIDTH), F32),
        scratch_shapes=[
            pltpu.VMEM((ns, cols), F32),
            pltpu.VMEM((NSA_HEAD_DIM + wb, cols), BF16),
            pltpu.VMEM((NSA_HEAD_DIM + SEL_ONES_ROWS, cols), F32),
            pltpu.VMEM((8 * (1 + SEL_BUFFERS), cols), F32),
        ] + [pltpu.VMEM((tk, cols), F32)] * SEL_BUFFERS + [pltpu.VMEM((tk, cols), BF16)] * SEL_BUFFERS,
        compiler_params=_cparams(("parallel", "arbitrary")),
        name="nsa_sel_attn",
    )(zn, k_aug, vt_aug, sel, zr, o_c, o_w)


def _cmp_to_sel_matrix(ncp, ns):
    cmp_start = np.arange(ncp)[:, None] * CMP_STRIDE
    sel_start = np.arange(ns)[None, :] * SLC_BLOCK
    overlap = np.minimum(cmp_start + CMP_BLOCK, sel_start + SLC_BLOCK) - np.maximum(cmp_start, sel_start)
    return jnp.asarray(np.clip(overlap, 0, None).astype(np.float32).T / CMP_BLOCK, dtype=BF16)


def _nsa(zn, zr, zc, k_aug, vt_aug, vw_t, cmp_pos_k, cmp_w1_k, cmp_w2_k, cmp_pos_v, cmp_w1_v, cmp_w2_v,
         tq, tq_sel, tk):
    T = zn.shape[1]
    width = CMP_BLOCK * NSA_HEAD_DIM
    pos = jnp.stack([cmp_pos_k.reshape(1, width), cmp_pos_v.reshape(1, width)])
    pos = jnp.broadcast_to(pos, (2, 8, width)).astype(BF16)
    w1 = jnp.stack([cmp_w1_k, cmp_w1_v]).astype(BF16)
    w2 = jnp.stack([cmp_w2_k, cmp_w2_v]).astype(BF16)
    kv_cmp = _compress(zc, pos, w1, w2)
    ns = T // SLC_BLOCK
    o_c, o_w, sel = _cmp_win(zn, kv_cmp, _cmp_to_sel_matrix(T // CMP_STRIDE, ns), vw_t, tq)
    return _sel_attn(zn, zr, k_aug, vt_aug, sel, o_c, o_w, tq_sel, tk)


def _softplus(y):
    return jnp.maximum(y, 0.0) + jnp.log(1.0 + jnp.exp(-jnp.abs(y)))


def _rwkv_kernel(z_ref, prev_ref, mu_ref, pv_ref, wup_ref, aup_ref, gup_ref, o_ref,
                 rt_s, at_s, kh_s, bh_s, kb_s, bb_s, v_s, gc_s, g_s, bonus_s, y_s, state_s, *, tb):
    C = RWKV_CHUNK
    step = pl.program_id(0)

    @pl.when(step == 0)
    def _():
        state_s[...] = jnp.zeros(state_s.shape, F32)

    has_prev = jnp.where(step > 0, 1.0, 0.0)
    row = lax.broadcasted_iota(jnp.int32, (tb, LANES), 0)

    def shifted_mix(t):
        z = z_ref[t]
        zp = jnp.where(row == 0, prev_ref[t, 7:8, :] * has_prev, pltpu.roll(z, 1, axis=0))
        return z + (zp - z) * mu_ref[t]

    ri = lax.broadcasted_iota(jnp.int32, (tb, tb), 0)
    ci = lax.broadcasted_iota(jnp.int32, (tb, tb), 1)
    tri = jnp.where(((ri >> 6) == (ci >> 6)) & (ci <= ri), 1.0, 0.0).astype(BF16)
    wide = 2 * LANES
    wr = lax.broadcasted_iota(jnp.int32, (wide, wide), 0)
    wc = lax.broadcasted_iota(jnp.int32, (wide, wide), 1)
    head_sum = jnp.where((wr >> 6) == (wc >> 6), 1.0, 0.0).astype(BF16)
    head_mean = jnp.where((wr >> 6) == (wc >> 6), 1.0 / RWKV_HEAD_DIM, 0.0).astype(BF16)

    wa = shifted_mix(3 * RWKV_PAIRS)
    dw = _dot(jnp.tanh(wa).astype(BF16), wup_ref[...])
    da = _dot(wa.astype(BF16), aup_ref[...])
    g0 = jax.nn.sigmoid(shifted_mix(3 * RWKV_PAIRS + 1)).astype(BF16)
    g1 = jax.nn.sigmoid(shifted_mix(3 * RWKV_PAIRS + 2)).astype(BF16)
    g_s[...] = _dot(g0, gup_ref[:LANES]) + _dot(g1, gup_ref[LANES:])

    def mixed(first_tile, q):
        return jnp.concatenate([shifted_mix(first_tile + 2 * q), shifted_mix(first_tile + 2 * q + 1)], axis=1)

    for q in range(RWKV_PAIRS // 2):
        sl = slice(q * wide, (q + 1) * wide)
        r = mixed(0, q)
        k = mixed(RWKV_PAIRS, q)
        v = mixed(2 * RWKV_PAIRS, q)
        w_log = -_softplus(-(pv_ref[0:1, sl] + dw[:, sl])) - 0.5
        lw = -jnp.exp(w_log)
        icl = jax.nn.sigmoid(pv_ref[1:2, sl] + da[:, sl])
        kk = k * pv_ref[2:3, sl]
        kk = kk * lax.rsqrt(jnp.maximum(_dot((kk * kk).astype(BF16), head_sum), 1e-24))
        k2 = k * (1.0 + (icl - 1.0) * pv_ref[3:4, sl])
        bonus_s[:, sl] = _dot((r * k2 * pv_ref[4:5, sl]).astype(BF16), head_sum) * v
        hi, mid, lo = _split3(lw)
        cum = _dot(tri, hi) + _dot(tri, mid) + _dot(tri, lo)
        b = kk * icl
        inv_decay = jnp.exp(-cum)
        to_end = []
        for c in range(tb // C):
            e_end = jnp.exp(cum[c * C + C - 1:c * C + C])
            gc_s[8 * c:8 * c + 1, sl] = e_end
            to_end.append(inv_decay[c * C:(c + 1) * C] * e_end)
        to_end = jnp.concatenate(to_end, axis=0)
        rt_s[:, sl] = (r * jnp.exp(cum)).astype(BF16)
        at_s[:, sl] = (-kk * jnp.exp(cum - lw)).astype(BF16)
        kh_s[:, sl] = (k2 * inv_decay).astype(BF16)
        bh_s[:, sl] = (b * inv_decay).astype(BF16)
        kb_s[:, sl] = (k2 * to_end).astype(BF16)
        bb_s[:, sl] = (b * to_end).astype(BF16)
        v_s[:, sl] = v.astype(BF16)

    hr = lax.broadcasted_iota(jnp.int32, (LANES, LANES), 0)
    hc = lax.broadcasted_iota(jnp.int32, (LANES, LANES), 1)
    same_head = (hr >> 6) == (hc >> 6)
    t_r = hr & (C - 1)
    t_c = hc & (C - 1)
    strict = same_head & (t_c < t_r)
    incl = same_head & (t_c <= t_r)
    eye = jnp.where(hr == hc, 1.0, 0.0)
    lane_head = lax.broadcasted_iota(jnp.int32, (C, LANES), 1) >> 6

    def stack(zc):
        return jnp.concatenate([jnp.where(lane_head == 0, zc, jnp.zeros_like(zc)),
                                jnp.where(lane_head == 1, zc, jnp.zeros_like(zc))], axis=0)

    def chunks(cc, carry):
        group = range(RWKV_CHUNK_GROUP)
        pairs = range(RWKV_PAIRS)
        units = [(d, p) for d in group for p in pairs]
        rows = [pl.ds(pl.multiple_of((cc * RWKV_CHUNK_GROUP + d) * C, C), C) for d in group]
        sls = [slice(p * LANES, (p + 1) * LANES) for p in pairs]
        lhs = {u: jnp.concatenate([stack(at_s[rows[u[0]], sls[u[1]]]), stack(rt_s[rows[u[0]], sls[u[1]]])], axis=0)
               for u in units}
        rhs = {u: jnp.concatenate([stack(kh_s[rows[u[0]], sls[u[1]]]), stack(bh_s[rows[u[0]], sls[u[1]]])], axis=0)
               for u in units}
        aa = {u: _dot_nt(lhs[u], rhs[u]) for u in units}
        a_ak = {u: jnp.where(strict, aa[u][:2 * C, :2 * C], 0.0).astype(BF16) for u in units}
        n_pow = {u: jnp.where(strict, aa[u][:2 * C, 2 * C:], 0.0) for u in units}
        a_r = {u: jnp.concatenate([jnp.where(incl, aa[u][2 * C:, :2 * C], 0.0).astype(BF16),
                                   jnp.where(incl, aa[u][2 * C:, 2 * C:], 0.0).astype(BF16)], axis=1) for u in units}
        t_inv = {u: eye + n_pow[u] for u in units}
        for _ in range(5):
            nb = {u: n_pow[u].astype(BF16) for u in units}
            n_pow = {u: _dot(nb[u], nb[u]) for u in units}
            t_inv = {u: t_inv[u] + _dot(t_inv[u].astype(BF16), n_pow[u].astype(BF16)) for u in units}
        t_inv = {u: t_inv[u].astype(BF16) for u in units}
        state = [state_s[p] for p in pairs]
        for d in group:
            xs = [_dot_nt(lhs[d, p], state[p].astype(BF16)) for p in pairs]
            v_c = [v_s[rows[d], sl] for sl in sls]
            v_st = [stack(v_c[p]) for p in pairs]
            av = [_dot(a_ak[d, p], v_st[p]) for p in pairs]
            sa = [_dot(t_inv[d, p], (xs[p][:2 * C] + av[p]).astype(BF16)) for p in pairs]
            ys = [xs[p][2 * C:] + _dot(a_r[d, p], jnp.concatenate([v_st[p], sa[p].astype(BF16)], axis=0))
                  for p in pairs]
            upd = [_dot_tn(jnp.concatenate([v_c[p], (sa[p][:C] + sa[p][C:]).astype(BF16)], axis=0),
                           jnp.concatenate([kb_s[rows[d], sls[p]], bb_s[rows[d], sls[p]]], axis=0)) for p in pairs]
            decay_row = pl.ds(pl.multiple_of((cc * RWKV_CHUNK_GROUP + d) * 8, 8), 1)
            for p in pairs:
                y_s[rows[d], sls[p]] = ys[p][:C] + ys[p][C:]
            state = [state[p] * gc_s[decay_row, sls[p]] + jnp.where(same_head, upd[p], 0.0) for p in pairs]
        for p in pairs:
            state_s[p] = state[p]
        return carry

    lax.fori_loop(0, tb // (C * RWKV_CHUNK_GROUP), chunks, 0)

    for q in range(RWKV_PAIRS // 2):
        sl = slice(q * wide, (q + 1) * wide)
        y = y_s[:, sl]
        d = y - _dot(y.astype(BF16), head_mean)
        var = _dot((d * d).astype(BF16), head_mean)
        yn = d * lax.rsqrt(var + LNX_EPS) * pv_ref[5:6, sl] + pv_ref[6:7, sl]
        o_ref[:, sl] = (yn + bonus_s[:, sl]) * g_s[:, sl]


def _hi_lo(w):
    hi = w.astype(BF16)
    return hi, (w - hi.astype(F32)).astype(BF16)


def _rwkv(zr, mu_t, w0, w_up, a0, a_up, g_up, k_k, k_a, r_k, lnx_g, lnx_b, tb):
    T = zr.shape[1]
    n_in = RW_TILES - 1
    pv = jnp.stack([w0, a0, k_k, k_a, r_k.reshape(-1), lnx_g, lnx_b, jnp.zeros_like(w0)])
    wup = jnp.pad(w_up, ((0, LANES - W_RANK), (0, 0))).astype(BF16)
    aup = jnp.pad(a_up, ((W_RANK, LANES - W_RANK - A_RANK), (0, 0))).astype(BF16)
    gup = jnp.pad(g_up, ((0, 2 * LANES - G_RANK), (0, 0))).astype(BF16)
    full = lambda shape: pl.BlockSpec(shape, lambda s: (0,) * len(shape))
    kern = functools.partial(_rwkv_kernel, tb=tb)
    bf = lambda: pltpu.VMEM((tb, RWKV_WIDTH), BF16)
    ff = lambda: pltpu.VMEM((tb, RWKV_WIDTH), F32)
    return pl.pallas_call(
        kern,
        grid=(T // tb,),
        in_specs=[
            pl.BlockSpec((n_in, tb, LANES), lambda s: (0, s, 0)),
            pl.BlockSpec((n_in, 8, LANES), lambda s: (0, jnp.maximum(s * (tb // 8) - 1, 0), 0)),
            full((n_in, 1, LANES)),
            full((8, RWKV_WIDTH)),
            full((LANES, RWKV_WIDTH)), full((LANES, RWKV_WIDTH)), full((2 * LANES, RWKV_WIDTH)),
        ],
        out_specs=pl.BlockSpec((tb, RWKV_WIDTH), lambda s: (s, 0)),
        out_shape=jax.ShapeDtypeStruct((T, RWKV_WIDTH), F32),
        scratch_shapes=[bf(), bf(), bf(), bf(), bf(), bf(), bf(),
                        pltpu.VMEM((8 * (tb // RWKV_CHUNK), RWKV_WIDTH), F32), ff(), ff(), ff(),
                        pltpu.VMEM((RWKV_PAIRS, LANES, LANES), F32)],
        compiler_params=_cparams(("arbitrary",)),
        name="rwkv7",
    )(zr, zr, mu_t, pv, wup, aup, gup)


PROJ_COLS = 512


def _mix_out_kernel(ya_ref, yb_ref, g_ref, w_ref, x_ref, o_ref):
    h = jnp.concatenate([_rms(ya_ref[...], g_ref[...]).astype(BF16), yb_ref[...].astype(BF16)], axis=1)
    for n in range(D_MODEL // PROJ_COLS):
        sl = slice(n * PROJ_COLS, (n + 1) * PROJ_COLS)
        o_ref[:, sl] = x_ref[:, sl] + _dot(h, w_ref[:, sl])


def _mix_out(y_a, y_b, g, w, x, tm):
    T = x.shape[0]
    return pl.pallas_call(
        _mix_out_kernel,
        grid=(T // tm,),
        in_specs=[
            pl.BlockSpec((tm, NSA_WIDTH), lambda m: (m, 0)),
            pl.BlockSpec((tm, RWKV_WIDTH), lambda m: (m, 0)),
            _resident((1, NSA_WIDTH)),
            _resident((NSA_WIDTH + RWKV_WIDTH, D_MODEL)),
            pl.BlockSpec((tm, D_MODEL), lambda m: (m, 0)),
        ],
        out_specs=pl.BlockSpec((tm, D_MODEL), lambda m: (m, 0)),
        out_shape=jax.ShapeDtypeStruct((T, D_MODEL), F32),
        compiler_params=_cparams(("parallel",)),
        name="mix_out_proj",
    )(y_a, y_b, g, w, x)


def _norm_mm_kernel(x_ref, g_ref, w_ref, o_ref, h_scr):
    @pl.when(pl.program_id(1) == 0)
    def _():
        h_scr[...] = _rms(x_ref[...], g_ref[...]).astype(BF16)

    o_ref[...] = _dot(h_scr[...], w_ref[...]).astype(o_ref.dtype)


def _norm_mm(x, g, w, tm, tn, name):
    M, K = x.shape
    N = w.shape[1]
    return pl.pallas_call(
        _norm_mm_kernel,
        grid=(M // tm, N // tn),
        in_specs=[
            pl.BlockSpec((tm, K), lambda m, n: (m, 0)),
            pl.BlockSpec((1, K), lambda m, n: (0, 0)),
            pl.BlockSpec((K, tn), lambda m, n: (0, n)),
        ],
        out_specs=pl.BlockSpec((tm, tn), lambda m, n: (m, n)),
        out_shape=jax.ShapeDtypeStruct((M, N), BF16),
        scratch_shapes=[pltpu.VMEM((tm, K), BF16)],
        compiler_params=_cparams(("parallel", "arbitrary")),
        name=name,
    )(x, g, w)


def _mem_router_kernel(x_ref, km_ref, vm_ref, wq_ref, wo_ref, g2_ref, g3_ref, w_hi, w_lo, b_ref,
                       x2_ref, h_ref, c_ref, n_ref):
    scale = MEM_HEAD_DIM ** -0.5
    hq = _rms(x_ref[...], g2_ref[...]).astype(BF16)
    heads = []
    for hd in range(MEM_HEADS):
        sl = slice(hd * MEM_HEAD_DIM, (hd + 1) * MEM_HEAD_DIM)
        q = _dot(hq, wq_ref[:, sl]).astype(BF16)
        s = _dot_nt(q, km_ref[:, sl]) * scale
        e = jnp.exp(s - jnp.max(s, axis=-1, keepdims=True))
        p = e / jnp.sum(e, axis=-1, keepdims=True)
        heads.append(_dot(p.astype(BF16), vm_ref[:, sl]).astype(BF16))
    o = jnp.concatenate(heads, axis=1)
    for n in range(D_MODEL // PROJ_COLS):
        sl = slice(n * PROJ_COLS, (n + 1) * PROJ_COLS)
        x2_ref[:, sl] = x_ref[:, sl] + _dot(o, wo_ref[:, sl])
    _route(x2_ref[...], g3_ref, w_hi, w_lo, b_ref, h_ref, c_ref, n_ref)


def _mem_router(x, k_mem, v_mem, wq, wo, g2, g3, w_hi, w_lo, b, tm):
    T = x.shape[0]
    M = k_mem.shape[0]
    row_block = lambda width: pl.BlockSpec((tm, width), lambda m: (m, 0))
    return pl.pallas_call(
        _mem_router_kernel,
        grid=(T // tm,),
        in_specs=[
            row_block(D_MODEL),
            _resident((M, D_MODEL)), _resident((M, D_MODEL)),
            _resident((D_MODEL, D_MODEL)), _resident((D_MODEL, D_MODEL)),
            _resident((1, D_MODEL)), _resident((1, D_MODEL)),
            _resident((D_MODEL, LANES)), _resident((D_MODEL, LANES)), _resident((1, LANES)),
        ],
        out_specs=[row_block(D_MODEL), row_block(D_MODEL), row_block(LANES),
                   pl.BlockSpec((1, 8, LANES), lambda m: (m, 0, 0))],
        out_shape=[
            jax.ShapeDtypeStruct((T, D_MODEL), F32),
            jax.ShapeDtypeStruct((T, D_MODEL), BF16),
            jax.ShapeDtypeStruct((T, LANES), F32),
            jax.ShapeDtypeStruct((T // tm, 8, LANES), F32),
        ],
        compiler_params=_cparams(("parallel",)),
        name="mem_attn_router",
    )(x, k_mem, v_mem, wq, wo, g2, g3, w_hi, w_lo, b)


def _route(x, g_ref, w_hi, w_lo, b_ref, h_ref, c_ref, n_ref):
    h = _rms(x, g_ref[...])
    h_ref[...] = h.astype(BF16)
    logits = _dot_xw3(h, w_hi[...], w_lo[...]) + b_ref[...]
    lane = lax.broadcasted_iota(jnp.int32, logits.shape, 1)
    big = jnp.int32(LANES)
    is_grp = (lane >= N_EXPERTS) & (lane < N_EXPERTS + N_GROUPS)
    lg = jnp.where(is_grp, logits, NEG)
    eg = jnp.where(is_grp, jnp.exp(lg - jnp.max(lg, axis=-1, keepdims=True)), 0.0)
    pg = eg / jnp.sum(eg, axis=-1, keepdims=True)
    pg_top = jnp.max(pg, axis=-1, keepdims=True)
    g_idx = jnp.min(jnp.where(is_grp & (pg == pg_top), lane - N_EXPERTS, big), axis=-1, keepdims=True)
    in_grp = (lane < N_EXPERTS) & ((lane >> 3) == g_idx)
    le = jnp.where(in_grp, logits, NEG)
    ee = jnp.where(in_grp, jnp.exp(le - jnp.max(le, axis=-1, keepdims=True)), 0.0)
    pe = jnp.where(in_grp, ee / jnp.sum(ee, axis=-1, keepdims=True), -1.0)
    p1 = jnp.max(pe, axis=-1, keepdims=True)
    hit1 = lane == jnp.min(jnp.where(pe == p1, lane, big), axis=-1, keepdims=True)
    pe2 = jnp.where(hit1, -1.0, pe)
    p2 = jnp.max(pe2, axis=-1, keepdims=True)
    hit2 = lane == jnp.min(jnp.where(pe2 == p2, lane, big), axis=-1, keepdims=True)
    denom = p1 + p2
    comb = jnp.where(hit1, pg_top * p1 / denom, 0.0) + jnp.where(hit2, pg_top * p2 / denom, 0.0)
    c_ref[...] = jnp.where(lane == GROUP_LANE, g_idx.astype(F32), comb)
    in_group = jnp.where(lane == g_idx, 1.0, 0.0)
    n_ref[0] = jnp.broadcast_to(jnp.sum(in_group, axis=0, keepdims=True), n_ref.shape[1:])


def _moe_kernel(meta_ref, h_ref, c_ref, wg_ref, wu_ref, wd_ref, y_ref, hs, cs, ys, slot_scr, perm, *, tm, sub):
    m = pl.program_id(0)
    e = pl.program_id(1)
    nslot = hs.shape[0]

    @pl.when(e == 0)
    def _():
        c = c_ref[...]
        lane = lax.broadcasted_iota(jnp.int32, c.shape, 1)
        one_hot = jnp.where(lane.astype(F32) == c[:, GROUP_LANE:GROUP_LANE + 1], 1.0, 0.0)
        earlier = jnp.where(lax.broadcasted_iota(jnp.int32, (tm, tm), 1)
                            < lax.broadcasted_iota(jnp.int32, (tm, tm), 0), 1.0, 0.0).astype(BF16)
        rank = _dot(earlier, one_hot.astype(BF16))
        start = jnp.zeros((1, LANES), F32)
        for g in range(N_GROUPS):
            start = jnp.where(lane[0:1] == g, (meta_ref[m * 2 * N_GROUPS + g] * sub).astype(F32), start)
        slot = jnp.sum(one_hot * (rank + start), axis=-1, keepdims=True)
        slot_scr[...] = jnp.broadcast_to(slot, slot_scr.shape)
        slot_row = slot_scr[...].T[0:1, :]
        perm[...] = jnp.where(lax.broadcasted_iota(jnp.int32, (nslot, tm), 0).astype(F32) == slot_row,
                              1.0, 0.0).astype(BF16)
        hs[...] = _dot(perm[...], h_ref[...]).astype(BF16)
        w_hi, w_lo = _split2(jnp.where(lane < N_EXPERTS, c, 0.0))
        cs[...] = _dot(perm[...], w_hi) + _dot(perm[...], w_lo)
        ys[...] = jnp.zeros(ys.shape, F32)

    grp = e // EXPERTS_PER_GROUP
    first = meta_ref[m * 2 * N_GROUPS + grp]
    count = meta_ref[m * 2 * N_GROUPS + N_GROUPS + grp]
    def ffn(start_block, n_blocks):
        n = n_blocks * sub
        rows = pl.ds(pl.multiple_of(start_block * sub, sub), n)
        x = hs[rows, :]
        hid = jax.nn.silu(_dot(x, wg_ref[0])) * _dot(x, wu_ref[0])
        lane_s = lax.broadcasted_iota(jnp.int32, (n, LANES), 1)
        c_e = jnp.sum(jnp.where(lane_s == e, cs[rows, :], 0.0), axis=-1, keepdims=True)
        ys[rows, :] += _dot((hid * c_e).astype(BF16), wd_ref[0])

    def quad(s, carry):
        ffn(first + 4 * s, 4)
        return carry

    quads = count // 4
    lax.fori_loop(0, quads, quad, 0)
    rest = count - 4 * quads

    @pl.when(rest >= 2)
    def _():
        ffn(first + 4 * quads, 2)

    @pl.when(rest % 2 == 1)
    def _():
        ffn(first + count - 1, 1)

    @pl.when(e == pl.num_programs(1) - 1)
    def _():
        back = jnp.where(lax.broadcasted_iota(jnp.int32, (tm, nslot), 1).astype(F32) == slot_scr[:, 0:1],
                         1.0, 0.0).astype(BF16)
        for n in range(D_MODEL // PROJ_COLS):
            sl = slice(n * PROJ_COLS, (n + 1) * PROJ_COLS)
            y_ref[:, sl] = _dot(back, ys[:, sl].astype(BF16)).astype(y_ref.dtype)


def _moe(h, comb, counts, wg, wu, wd, tm, sub):
    T = h.shape[0]
    nt = T // tm
    nslot = tm + N_GROUPS * sub
    cnt = counts[:, 0, :N_GROUPS].astype(jnp.int32).reshape(nt, -1, N_GROUPS).sum(axis=1)
    nblk = (cnt + sub - 1) // sub
    first = jnp.cumsum(nblk, axis=1) - nblk
    meta = jnp.concatenate([first, nblk], axis=1).reshape(-1)
    once = lambda shape: pl.BlockSpec(shape, lambda m, e, meta: (m, 0), pipeline_mode=pl.Buffered(1))
    kern = functools.partial(_moe_kernel, tm=tm, sub=sub)
    return pl.pallas_call(
        kern,
        grid_spec=pltpu.PrefetchScalarGridSpec(
            num_scalar_prefetch=1,
            grid=(nt, N_EXPERTS),
            in_specs=[
                once((tm, D_MODEL)),
                once((tm, LANES)),
                pl.BlockSpec((1, D_MODEL, EXPERT_FF), lambda m, e, meta: (e, 0, 0)),
                pl.BlockSpec((1, D_MODEL, EXPERT_FF), lambda m, e, meta: (e, 0, 0)),
                pl.BlockSpec((1, EXPERT_FF, D_MODEL), lambda m, e, meta: (e, 0, 0)),
            ],
            out_specs=pl.BlockSpec((tm, D_MODEL), lambda m, e, meta: (m, 0)),
            scratch_shapes=[
                pltpu.VMEM((nslot, D_MODEL), BF16),
                pltpu.VMEM((nslot, LANES), F32),
                pltpu.VMEM((nslot, D_MODEL), F32),
                pltpu.VMEM((tm, LANES), F32),
                pltpu.VMEM((nslot, tm), BF16),
            ],
        ),
        out_shape=jax.ShapeDtypeStruct((T, D_MODEL), BF16),
        compiler_params=_cparams(("parallel", "arbitrary")),
        name="moe_experts",
    )(meta, h, comb, wg, wu, wd)


def _final_norm_kernel(x_ref, y_ref, g_ref, o_ref):
    o_ref[...] = _rms(x_ref[...] + y_ref[...].astype(F32), g_ref[...])


def _final_norm(x, y, g, tm):
    T = x.shape[0]
    blk = pl.BlockSpec((tm, D_MODEL), lambda m: (m, 0))
    return pl.pallas_call(
        _final_norm_kernel,
        grid=(T // tm,),
        in_specs=[blk, blk, _resident((1, D_MODEL))],
        out_specs=blk,
        out_shape=jax.ShapeDtypeStruct((T, D_MODEL), F32),
        compiler_params=_cparams(("parallel",)),
        name="moe_residual_norm",
    )(x, y, g)


def _pack_w_in(w_in, rwkv_mu):
    nsa_cols = NSA_WIDTH + 6 * NSA_KV_WIDTH
    gl = w_in[:, nsa_cols:nsa_cols + 3 * NSA_HEADS]
    rw0 = nsa_cols + 3 * NSA_HEADS
    rkv = w_in[:, rw0:rw0 + 3 * RWKV_WIDTH + W_RANK + A_RANK]
    gd = w_in[:, rw0 + 3 * RWKV_WIDTH + W_RANK + A_RANK:]

    def pad(a, n):
        return jnp.pad(a, ((0, 0), (0, n - a.shape[1])))

    w = jnp.concatenate([w_in[:, :nsa_cols], rkv, pad(gd, 2 * LANES), pad(gl, LANES)], axis=1)
    mu = rwkv_mu.reshape(1, -1)
    mu_rkv = mu[:, :3 * RWKV_WIDTH + W_RANK + A_RANK]
    mu_gd = pad(mu[:, 3 * RWKV_WIDTH + W_RANK + A_RANK:], 2 * LANES)
    mu_t = jnp.concatenate([mu_rkv, mu_gd], axis=1).reshape(RW_TILES - 1, 1, LANES)
    return w.astype(BF16), mu_t


def kernel(x, mem, ln1_g, w_in, cmp_pos_k, cmp_w1_k, cmp_w2_k, cmp_pos_v, cmp_w1_v, cmp_w2_v, nsa_norm_g, rwkv_mu, rwkv_w0, rwkv_w_up, rwkv_a0, rwkv_a_up, rwkv_g_up, rwkv_k_k, rwkv_k_a, rwkv_r_k, rwkv_lnx_g, rwkv_lnx_b, w_out, ln_mem_g, ln2_g, wq_mem, wk_mem, wv_mem, wo_mem, ln3_g, router_group_w, router_group_b, router_expert_w, router_expert_b, moe_w_gate, moe_w_up, moe_w_down, lnf_g):
    B, T, _ = x.shape
    assert B == 1 and T % max(TM_PROJ, TM_MEM, TQ_SEL, TK_SEL, TB_RWKV, TM_MOE) == 0
    assert w_in.shape[0] == 1
    row = lambda a: a.reshape(1, -1)
    xs = x[0]
    for l in range(w_in.shape[0]):
        wp, mu_t = _pack_w_in(w_in[l], rwkv_mu[l])
        sel_window = min(LANES, T // SLC_BLOCK)
        zn, zr, zc, k_aug, vt_aug, vw_t = _in_proj(xs, row(ln1_g[l]), wp, TM_PROJ, sel_window)
        y_a = _nsa(zn, zr, zc, k_aug, vt_aug, vw_t, cmp_pos_k[l], cmp_w1_k[l], cmp_w2_k[l], cmp_pos_v[l], cmp_w1_v[l], cmp_w2_v[l],
                   TQ_NSA, TQ_SEL, TK_SEL)
        y_b = _rwkv(zr, mu_t, rwkv_w0[l], rwkv_w_up[l], rwkv_a0[l], rwkv_a_up[l], rwkv_g_up[l], rwkv_k_k[l],
                    rwkv_k_a[l], rwkv_r_k[l], rwkv_lnx_g[l], rwkv_lnx_b[l], TB_RWKV)
        xs = _mix_out(y_a, y_b, row(nsa_norm_g[l]), w_out[l].astype(BF16), xs, TM_MEM)
        m_tok = mem[0]
        k_mem = _norm_mm(m_tok, row(ln_mem_g[l]), wk_mem[l].astype(BF16), m_tok.shape[0], PROJ_COLS, "mem_k")
        v_mem = _norm_mm(m_tok, row(ln_mem_g[l]), wv_mem[l].astype(BF16), m_tok.shape[0], PROJ_COLS, "mem_v")
        w_r = jnp.pad(jnp.concatenate([router_expert_w[l], router_group_w[l]], axis=1),
                      ((0, 0), (0, LANES - N_EXPERTS - N_GROUPS)))
        b_r = jnp.pad(jnp.concatenate([router_expert_b[l], router_group_b[l]]), (0, LANES - N_EXPERTS - N_GROUPS))
        xs, h3, comb, counts = _mem_router(xs, k_mem, v_mem, wq_mem[l].astype(BF16), wo_mem[l].astype(BF16),
                                   row(ln2_g[l]), row(ln3_g[l]), *_hi_lo(w_r), row(b_r), TM_MEM)
        y_moe = _moe(h3, comb, counts, moe_w_gate[l].astype(BF16), moe_w_up[l].astype(BF16),
                     moe_w_down[l].astype(BF16), TM_MOE, SUB_MOE)
        xs = _final_norm(xs, y_moe, row(lnf_g), TM_PROJ)
    return xs[None]
```

```python
import functools

import numpy as np
import jax
import jax.numpy as jnp
from jax import lax
from jax.experimental import pallas as pl
from jax.experimental.pallas import tpu as pltpu

F32 = jnp.float32
BF16 = jnp.bfloat16

LANES = 128
D_MODEL = 2048
EPS = 1e-6
NSA_HEAD_DIM = 128
NSA_HEADS = 8
NSA_KV_GROUPS = 2
NSA_HPG = NSA_HEADS // NSA_KV_GROUPS
NSA_WIDTH = NSA_HEADS * NSA_HEAD_DIM
NSA_KV_WIDTH = NSA_KV_GROUPS * NSA_HEAD_DIM
CMP_BLOCK = 32
CMP_STRIDE = 16
SLC_BLOCK = 64
SLC_TOPK = 16
WINDOW = 512
NEG = -1e30
RWKV_HEAD_DIM = 64
RWKV_HEADS = 16
RWKV_WIDTH = RWKV_HEADS * RWKV_HEAD_DIM
RWKV_PAIRS = RWKV_WIDTH // LANES
W_RANK = 64
A_RANK = 64
G_RANK = 160
LNX_EPS = 64e-5
RWKV_CHUNK = 64
RWKV_CHUNK_GROUP = 4
MEM_HEADS = 4
MEM_HEAD_DIM = D_MODEL // MEM_HEADS
N_GROUPS = 4
EXPERTS_PER_GROUP = 8
N_EXPERTS = N_GROUPS * EXPERTS_PER_GROUP
EXPERT_FF = 256

NSA_TILES = 20
RW_TILES = 28
IN_TILE_BLOCK = 4

VMEM_LIMIT = 56 * 1024 * 1024

TM_PROJ = 256
TM_MEM = 512
TQ_NSA = 256
TQ_SEL = 512
TK_SEL = 512
TB_RWKV = 256
TM_MOE = 1024
SUB_MOE = 64
GROUP_LANE = 64


def _cparams(sem):
    return pltpu.CompilerParams(dimension_semantics=sem, vmem_limit_bytes=VMEM_LIMIT)


def _dot(a, b):
    return jnp.dot(a, b, preferred_element_type=F32)


def _dot_nt(a, b):
    return lax.dot_general(a, b, (((1,), (1,)), ((), ())), preferred_element_type=F32)


def _dot_tn(a, b):
    return lax.dot_general(a, b, (((0,), (0,)), ((), ())), preferred_element_type=F32)


def _split2(x):
    hi = x.astype(BF16)
    lo = (x - hi.astype(F32)).astype(BF16)
    return hi, lo


def _split3(x):
    hi = x.astype(BF16)
    r1 = x - hi.astype(F32)
    mid = r1.astype(BF16)
    lo = (r1 - mid.astype(F32)).astype(BF16)
    return hi, mid, lo


def _dot_x2(x, w):
    hi, lo = _split2(x)
    return _dot(hi, w) + _dot(lo, w)


def _dot_x3(x, w):
    hi, mid, lo = _split3(x)
    return _dot(hi, w) + _dot(mid, w) + _dot(lo, w)


def _dot_xw3(x, w_hi, w_lo):
    hi, lo = _split2(x)
    return _dot(hi, w_hi) + _dot(lo, w_hi) + _dot(hi, w_lo)


def _rms(x, g):
    ms = jnp.mean(x * x, axis=-1, keepdims=True)
    return x * lax.rsqrt(ms + EPS) * g


def _resident(shape):
    return pl.BlockSpec(shape, lambda *_: (0,) * len(shape), pipeline_mode=pl.Buffered(1))


CMP_TILES = 2 * NSA_KV_GROUPS


KS_TILE0 = NSA_HEADS + 2 * NSA_KV_GROUPS
SEL_ONES_ROWS = 16


def _in_proj_kernel(x_ref, g_ref, w_ref, zn_ref, zr_ref, zc_ref, ka_ref, vst_ref, vwt_ref, *, q_scale):
    tm = x_ref.shape[0]
    wb = ka_ref.shape[2] - NSA_HEAD_DIM
    h = _rms(x_ref[...], g_ref[...]).astype(BF16)
    nb = IN_TILE_BLOCK * LANES
    for n in range((NSA_TILES + RW_TILES) // IN_TILE_BLOCK):
        z = _dot(h, w_ref[:, n * nb:(n + 1) * nb])
        for t in range(IN_TILE_BLOCK):
            tile = n * IN_TILE_BLOCK + t
            zt = z[:, t * LANES:(t + 1) * LANES]
            if tile < NSA_HEADS:
                zn_ref[tile] = (zt * q_scale).astype(BF16)
            elif tile < NSA_TILES:
                zn_ref[tile] = zt.astype(BF16)
                if tile < NSA_HEADS + CMP_TILES:
                    zc_ref[tile - NSA_HEADS] = zt
                elif tile < KS_TILE0 + NSA_KV_GROUPS:
                    ka_ref[tile - KS_TILE0, :, :NSA_HEAD_DIM] = zt.astype(BF16)
                elif tile < KS_TILE0 + 2 * NSA_KV_GROUPS:
                    vst_ref[tile - KS_TILE0 - NSA_KV_GROUPS, :NSA_HEAD_DIM, :] = zt.T.astype(BF16)
                elif tile >= KS_TILE0 + 3 * NSA_KV_GROUPS:
                    vwt_ref[tile - KS_TILE0 - 3 * NSA_KV_GROUPS] = zt.T.astype(BF16)
            else:
                zr_ref[tile - NSA_TILES] = zt
    blk = ((pl.program_id(0) * tm + lax.broadcasted_iota(jnp.int32, (tm, wb), 0)) >> 6) & (wb - 1)
    one_hot = jnp.where(blk == lax.broadcasted_iota(jnp.int32, (tm, wb), 1), 1.0, 0.0).astype(BF16)
    for g in range(NSA_KV_GROUPS):
        ka_ref[g, :, NSA_HEAD_DIM:] = one_hot
        vst_ref[g, NSA_HEAD_DIM:, :] = jnp.ones((SEL_ONES_ROWS, tm), BF16)


def _in_proj(x, g, w, tm, wb):
    T = x.shape[0]
    assert wb & (wb - 1) == 0
    kern = functools.partial(_in_proj_kernel, q_scale=NSA_HEAD_DIM ** -0.5)
    tiles = lambda n: pl.BlockSpec((n, tm, LANES), lambda m: (0, m, 0))
    return pl.pallas_call(
        kern,
        grid=(T // tm,),
        in_specs=[
            pl.BlockSpec((tm, D_MODEL), lambda m: (m, 0)),
            _resident((1, D_MODEL)),
            _resident((D_MODEL, (NSA_TILES + RW_TILES) * LANES)),
        ],
        out_specs=[
            tiles(NSA_TILES), tiles(RW_TILES), tiles(CMP_TILES),
            pl.BlockSpec((NSA_KV_GROUPS, tm, NSA_HEAD_DIM + wb), lambda m: (0, m, 0)),
            pl.BlockSpec((NSA_KV_GROUPS, NSA_HEAD_DIM + SEL_ONES_ROWS, tm), lambda m: (0, 0, m)),
            pl.BlockSpec((NSA_KV_GROUPS, NSA_HEAD_DIM, tm), lambda m: (0, 0, m)),
        ],
        out_shape=[
            jax.ShapeDtypeStruct((NSA_TILES, T, LANES), BF16),
            jax.ShapeDtypeStruct((RW_TILES, T, LANES), F32),
            jax.ShapeDtypeStruct((CMP_TILES, T, LANES), F32),
            jax.ShapeDtypeStruct((NSA_KV_GROUPS, T, NSA_HEAD_DIM + wb), BF16),
            jax.ShapeDtypeStruct((NSA_KV_GROUPS, NSA_HEAD_DIM + SEL_ONES_ROWS, T), BF16),
            jax.ShapeDtypeStruct((NSA_KV_GROUPS, NSA_HEAD_DIM, T), BF16),
        ],
        compiler_params=_cparams(("parallel",)),
        name="in_proj",
    )(x, g, w)


def _compress_kernel(x_ref, pos_ref, w1_ref, w2_ref, o_ref):
    n = x_ref.shape[1] // CMP_STRIDE
    w1 = w1_ref[0]
    h_first = jnp.zeros((n, NSA_HEAD_DIM), F32)
    h_second = jnp.zeros((n, NSA_HEAD_DIM), F32)
    for t in range(CMP_STRIDE):
        x_t = x_ref[0, pl.ds(t, n, stride=CMP_STRIDE), :].astype(BF16)
        h_first += _dot(x_t, w1[t * NSA_HEAD_DIM:(t + 1) * NSA_HEAD_DIM])
        h_second += _dot(x_t, w1[(CMP_STRIDE + t) * NSA_HEAD_DIM:(CMP_STRIDE + t + 1) * NSA_HEAD_DIM])
    h_next = pltpu.roll(h_second, n - 1, axis=0)
    bias = _dot(pos_ref[0], w1)[0:1]
    hid = jax.nn.gelu(h_first + h_next + bias)
    o_ref[0] = _dot(hid.astype(BF16), w2_ref[0]).astype(o_ref.dtype)


def _compress(zc, pos, w1, w2):
    T = zc.shape[1]
    n = T // CMP_STRIDE
    width = CMP_BLOCK * NSA_HEAD_DIM
    return pl.pallas_call(
        _compress_kernel,
        grid=(4,),
        in_specs=[
            pl.BlockSpec((1, T, NSA_HEAD_DIM), lambda s: (s, 0, 0)),
            pl.BlockSpec((1, 8, width), lambda s: (s // 2, 0, 0)),
            pl.BlockSpec((1, width, NSA_HEAD_DIM), lambda s: (s // 2, 0, 0)),
            pl.BlockSpec((1, NSA_HEAD_DIM, NSA_HEAD_DIM), lambda s: (s // 2, 0, 0)),
        ],
        out_specs=pl.BlockSpec((1, n, NSA_HEAD_DIM), lambda s: (s, 0, 0)),
        out_shape=jax.ShapeDtypeStruct((4, n, NSA_HEAD_DIM), BF16),
        compiler_params=_cparams(("parallel",)),
        name="nsa_compress",
    )(zc, pos, w1, w2)


def _cmp_win_kernel(q_ref, kc_ref, lhs_ref, kw_ref, vwt_ref, oc_ref, ow_ref, sel_ref, s_scr, acc_scr,
                    *, tq, top_n, ch):
    q0 = pl.program_id(1) * tq
    cols = NSA_HPG * tq
    ncp = kc_ref.shape[1]
    ns = lhs_ref.shape[1] - NSA_HEAD_DIM - SEL_ONES_ROWS
    q_t = jnp.concatenate([q_ref[h].astype(F32).T.astype(BF16) for h in range(NSA_HPG)], axis=1)
    qpos = q0 + (lax.broadcasted_iota(jnp.int32, (1, cols), 1) & (tq - 1))

    def store_heads(o_ref, o_t):
        for h in range(NSA_HPG):
            o_ref[:, h * NSA_HEAD_DIM:(h + 1) * NSA_HEAD_DIM] = o_t[:, h * tq:(h + 1) * tq].T.astype(o_ref.dtype)

    span = WINDOW + tq
    start = pl.multiple_of(jnp.maximum(q0 - WINDOW, 0), tq)
    kpos = start + lax.broadcasted_iota(jnp.int32, (span, 1), 0)
    in_window = (qpos - kpos).astype(jnp.uint32) < jnp.uint32(WINDOW)
    s_w = jnp.where(in_window, _dot(kw_ref[0, pl.ds(start, span), :], q_t), NEG)
    e_w = jnp.exp(s_w - jnp.max(s_w, axis=0, keepdims=True))
    p_w = e_w / jnp.sum(e_w, axis=0, keepdims=True)
    store_heads(ow_ref, _dot(vwt_ref[0, :, pl.ds(start, span)], p_w.astype(BF16)))

    visible = (q0 + tq - 1 - (CMP_BLOCK - 1)) // CMP_STRIDE + 1
    clean = jnp.maximum((q0 - (CMP_BLOCK - 1)) // CMP_STRIDE + 1, 0) // ch
    n_chunks = (visible + ch - 1) // ch

    def chunk_rows(c):
        return pl.ds(pl.multiple_of(c * ch, ch), ch)

    def score_chunk(c, m, masked):
        s = _dot(kc_ref[0, chunk_rows(c), :], q_t)
        if masked:
            cmp_end = (c * ch + lax.broadcasted_iota(jnp.int32, (ch, 1), 0)) * CMP_STRIDE + (CMP_BLOCK - 1)
            s = jnp.where(cmp_end <= qpos, s, NEG)
        s_scr[chunk_rows(c), :] = s
        return jnp.maximum(m, jnp.max(s, axis=0, keepdims=True))

    m = jnp.full((1, cols), NEG, F32)
    m = lax.fori_loop(0, clean, lambda c, m: score_chunk(c, m, False), m)
    m = lax.fori_loop(clean, n_chunks, lambda c, m: score_chunk(c, m, True), m)
    acc_scr[...] = jnp.zeros(acc_scr.shape, F32)

    def weigh_chunk(c, carry):
        e = jnp.exp(s_scr[chunk_rows(c), :] - m).astype(BF16)
        acc_scr[...] += _dot(lhs_ref[0, :, chunk_rows(c)], e)
        return carry

    lax.fori_loop(0, n_chunks, weigh_chunk, 0)
    acc = acc_scr[...]
    sees_any = jnp.where(qpos >= CMP_BLOCK - 1, 1.0, 0.0)
    r_inv = sees_any / jnp.maximum(acc[NSA_HEAD_DIM + ns:NSA_HEAD_DIM + ns + 1], 1e-30)
    store_heads(oc_ref, acc[:NSA_HEAD_DIM] * r_inv)
    weighted = acc[NSA_HEAD_DIM:NSA_HEAD_DIM + ns] * r_inv
    p_sel = weighted[:, 0:tq]
    for h in range(1, NSA_HPG):
        p_sel = p_sel + weighted[:, h * tq:(h + 1) * tq]
    blk = (q0 + lax.broadcasted_iota(jnp.int32, (1, tq), 1)) >> 6
    j = lax.broadcasted_iota(jnp.int32, (ns, tq), 0)
    future = j > blk
    forced = (j == 0) | (j == blk) | (j == blk - 1)
    score = jnp.where(future, -1.0, jnp.where(forced, 1e6, p_sel))
    sel = jnp.zeros((ns, tq), F32)
    for _ in range(top_n):
        best = jnp.max(score, axis=0, keepdims=True)
        idx = jnp.min(jnp.where(score == best, j, ns), axis=0, keepdims=True)
        hit = j == idx
        sel = jnp.where(hit, jnp.where(best >= 0.0, 1.0, 0.0), sel)
        score = jnp.where(hit, -2.0, score)
    sel_ref[0] = sel.astype(sel_ref.dtype)


CMP_CHUNK = 256


def _cmp_win(zn, kv_cmp, cmp_to_sel, vw_t, tq):
    T = zn.shape[1]
    ncp = kv_cmp.shape[1]
    ns = cmp_to_sel.shape[0]
    ch = min(CMP_CHUNK, ncp)
    assert ncp % ch == 0
    kw_tile0 = NSA_HEADS + 4 * NSA_KV_GROUPS
    lhs = jnp.concatenate([jnp.swapaxes(kv_cmp[NSA_KV_GROUPS:], 1, 2),
                           jnp.broadcast_to(cmp_to_sel, (NSA_KV_GROUPS, ns, ncp)),
                           jnp.ones((NSA_KV_GROUPS, SEL_ONES_ROWS, ncp), BF16)], axis=1)
    rows = lhs.shape[1]
    cols = NSA_HPG * tq
    kern = functools.partial(_cmp_win_kernel, tq=tq, top_n=min(SLC_TOPK, ns), ch=ch)
    wide = pl.BlockSpec((tq, NSA_HPG * NSA_HEAD_DIM), lambda g, i: (i, g))
    return pl.pallas_call(
        kern,
        grid=(NSA_KV_GROUPS, T // tq),
        in_specs=[
            pl.BlockSpec((NSA_HPG, tq, NSA_HEAD_DIM), lambda g, i: (g, i, 0)),
            pl.BlockSpec((1, ncp, NSA_HEAD_DIM), lambda g, i: (g, 0, 0)),
            pl.BlockSpec((1, rows, ncp), lambda g, i: (g, 0, 0)),
            pl.BlockSpec((1, T, NSA_HEAD_DIM), lambda g, i: (kw_tile0 + g, 0, 0)),
            pl.BlockSpec((1, NSA_HEAD_DIM, T), lambda g, i: (g, 0, 0)),
        ],
        out_specs=[wide, wide, pl.BlockSpec((1, ns, tq), lambda g, i: (g, 0, i))],
        out_shape=[
            jax.ShapeDtypeStruct((T, NSA_WIDTH), BF16),
            jax.ShapeDtypeStruct((T, NSA_WIDTH), BF16),
            jax.ShapeDtypeStruct((NSA_KV_GROUPS, ns, T), BF16),
        ],
        scratch_shapes=[pltpu.VMEM((ncp, cols), F32), pltpu.VMEM((rows, cols), F32)],
        compiler_params=_cparams(("parallel", "arbitrary")),
        name="nsa_cmp_window",
    )(zn, kv_cmp, lhs, zn, vw_t)


SEL_BUFFERS = 2


def _sel_attn_kernel(q_ref, k_ref, vt_ref, sel_ref, gl_ref, oc_ref, ow_ref, o_ref,
                     bias_scr, qa_scr, acc_scr, m_scr, *bufs, tq, tk, wb):
    s_bufs, p_bufs = bufs[:SEL_BUFFERS], bufs[SEL_BUFFERS:]
    q0 = pl.program_id(1) * tq
    cols = NSA_HPG * tq
    tpw = wb * SLC_BLOCK // tk
    last_tile = k_ref.shape[1] // tk - 1
    ns = sel_ref.shape[1]
    first_own = q0 // SLC_BLOCK
    not_chosen = (sel_ref[0].astype(F32) - 1.0) * (-NEG)
    for h in range(NSA_HPG):
        bias_scr[:, h * tq:(h + 1) * tq] = not_chosen
        qa_scr[:NSA_HEAD_DIM, h * tq:(h + 1) * tq] = q_ref[h].astype(F32).T.astype(BF16)

    def set_window(w):
        rows = pl.ds(pl.multiple_of(w * wb, wb), wb)
        before = w * wb + lax.broadcasted_iota(jnp.int32, (wb, 1), 0) < first_own
        qa_scr[NSA_HEAD_DIM:, :] = jnp.where(before, bias_scr[rows, :], NEG).astype(BF16)

    def scores(j):
        j = jnp.minimum(j, last_tile)
        return _dot(k_ref[0, pl.ds(pl.multiple_of(j * tk, tk), tk), :], qa_scr[...])

    def pv(j, p_ref):
        return _dot(vt_ref[0, :, pl.ds(pl.multiple_of(j * tk, tk), tk)], p_ref[...])

    def score_into(i, j):
        s = scores(j)
        s_bufs[i][...] = s
        m_scr[8 * (i + 1):8 * (i + 1) + 1, :] = jnp.max(s, axis=0, keepdims=True)

    def softmax_tile(i):
        m_old = m_scr[0:1, :]
        m_new = jnp.maximum(m_old, m_scr[8 * (i + 1):8 * (i + 1) + 1, :])
        m_scr[0:1, :] = m_new
        p_bufs[i][...] = jnp.exp((s_bufs[i][...] - m_new).astype(BF16))
        return jnp.exp(m_old - m_new)

    own = pl.ds(pl.multiple_of(q0, tq), tq)
    s = _dot(k_ref[0, own, :NSA_HEAD_DIM], qa_scr[:NSA_HEAD_DIM, :])
    s = jnp.concatenate([s[b * SLC_BLOCK:(b + 1) * SLC_BLOCK] + bias_scr[pl.ds(first_own + b, 1), :]
                         for b in range(tq // SLC_BLOCK)], axis=0)
    kpos = lax.broadcasted_iota(jnp.int32, (tq, 1), 0)
    qpos = lax.broadcasted_iota(jnp.int32, (1, cols), 1) & (tq - 1)
    s = jnp.where(kpos <= qpos, s, NEG)
    m_first = jnp.max(s, axis=0, keepdims=True)
    m_scr[0:1, :] = m_first
    acc_scr[...] = _dot(vt_ref[0, :, own], jnp.exp((s - m_first).astype(BF16)))

    n_tiles = (q0 + tk - 1) // tk

    nbuf = len(s_bufs)

    def window(w, carry):
        lo = w * tpw
        cnt = jnp.minimum(n_tiles - lo, tpw)
        set_window(w)
        score_into(0, lo)
        p_bufs[nbuf - 1][...] = jnp.zeros(p_bufs[nbuf - 1].shape, BF16)

        def group(t, alpha):
            first = lo + nbuf * t
            for i in range(nbuf):
                j = first + i
                score_into((i + 1) % nbuf, j + 1)
                acc_scr[...] = alpha * acc_scr[...] + pv(jnp.maximum(j - 1, 0), p_bufs[(i - 1) % nbuf])
                alpha = softmax_tile(i)
            return alpha

        groups = (cnt + nbuf - 1) // nbuf
        alpha_last = lax.fori_loop(0, groups, group, jnp.ones((1, cols), F32))
        acc_scr[...] = alpha_last * acc_scr[...] + pv(lo + nbuf * groups - 1, p_bufs[nbuf - 1])
        return carry

    lax.fori_loop(0, (n_tiles + tpw - 1) // tpw, window, 0)
    acc = acc_scr[...]
    o_t = acc[:NSA_HEAD_DIM] / acc[NSA_HEAD_DIM:NSA_HEAD_DIM + 1]
    gates = jax.nn.sigmoid(gl_ref[0])
    lane = lax.broadcasted_iota(jnp.int32, gates.shape, 1)
    for h in range(NSA_HPG):
        col = pl.program_id(0) * NSA_HPG + h

        def gate(branch):
            return jnp.sum(jnp.where(lane == branch * NSA_HEADS + col, gates, 0.0), axis=-1, keepdims=True)

        sl = slice(h * NSA_HEAD_DIM, (h + 1) * NSA_HEAD_DIM)
        o_ref[:, sl] = (gate(0) * oc_ref[:, sl].astype(F32) + gate(1) * o_t[:, h * tq:(h + 1) * tq].T
                        + gate(2) * ow_ref[:, sl].astype(F32))


def _sel_attn(zn, zr, k_aug, vt_aug, sel, o_c, o_w, tq, tk):
    T = zn.shape[1]
    ns = sel.shape[1]
    wb = k_aug.shape[2] - NSA_HEAD_DIM
    assert tk % tq == 0 and (wb * SLC_BLOCK) % (SEL_BUFFERS * tk) == 0 and ns % wb == 0 and T % tk == 0
    kern = functools.partial(_sel_attn_kernel, tq=tq, tk=tk, wb=wb)
    cols = NSA_HPG * tq
    wide = pl.BlockSpec((tq, NSA_HPG * NSA_HEAD_DIM), lambda g, i: (i, g))
    return pl.pallas_call(
        kern,
        grid=(NSA_KV_GROUPS, T // tq),
        in_specs=[
            pl.BlockSpec((NSA_HPG, tq, NSA_HEAD_DIM), lambda g, i: (g, i, 0)),
            pl.BlockSpec((1, T, NSA_HEAD_DIM + wb), lambda g, i: (g, 0, 0)),
            pl.BlockSpec((1, NSA_HEAD_DIM + SEL_ONES_ROWS, T), lambda g, i: (g, 0, 0)),
            pl.BlockSpec((1, ns, tq), lambda g, i: (g, 0, i)),
            pl.BlockSpec((1, tq, LANES), lambda g, i: (RW_TILES - 1, i, 0)),
            wide, wide,
        ],
        out_specs=wide,
        out_shape=jax.ShapeDtypeStruct((T, NSA_WIDTH), F32),
        scratch_shapes=[
            pltpu.VMEM((ns, cols), F32),
            pltpu.VMEM((NSA_HEAD_DIM + wb, cols), BF16),
            pltpu.VMEM((NSA_HEAD_DIM + SEL_ONES_ROWS, cols), F32),
            pltpu.VMEM((8 * (1 + SEL_BUFFERS), cols), F32),
        ] + [pltpu.VMEM((tk, cols), F32)] * SEL_BUFFERS + [pltpu.VMEM((tk, cols), BF16)] * SEL_BUFFERS,
        compiler_params=_cparams(("parallel", "arbitrary")),
        name="nsa_sel_attn",
    )(zn, k_aug, vt_aug, sel, zr, o_c, o_w)


def _cmp_to_sel_matrix(ncp, ns):
    cmp_start = np.arange(ncp)[:, None] * CMP_STRIDE
    sel_start = np.arange(ns)[None, :] * SLC_BLOCK
    overlap = np.minimum(cmp_start + CMP_BLOCK, sel_start + SLC_BLOCK) - np.maximum(cmp_start, sel_start)
    return jnp.asarray(np.clip(overlap, 0, None).astype(np.float32).T / CMP_BLOCK, dtype=BF16)


def _nsa(zn, zr, zc, k_aug, vt_aug, vw_t, cmp_pos_k, cmp_w1_k, cmp_w2_k, cmp_pos_v, cmp_w1_v, cmp_w2_v,
         tq, tq_sel, tk):
    T = zn.shape[1]
    width = CMP_BLOCK * NSA_HEAD_DIM
    pos = jnp.stack([cmp_pos_k.reshape(1, width), cmp_pos_v.reshape(1, width)])
    pos = jnp.broadcast_to(pos, (2, 8, width)).astype(BF16)
    w1 = jnp.stack([cmp_w1_k, cmp_w1_v]).astype(BF16)
    w2 = jnp.stack([cmp_w2_k, cmp_w2_v]).astype(BF16)
    kv_cmp = _compress(zc, pos, w1, w2)
    ns = T // SLC_BLOCK
    o_c, o_w, sel = _cmp_win(zn, kv_cmp, _cmp_to_sel_matrix(T // CMP_STRIDE, ns), vw_t, tq)
    return _sel_attn(zn, zr, k_aug, vt_aug, sel, o_c, o_w, tq_sel, tk)


def _softplus(y):
    return jnp.maximum(y, 0.0) + jnp.log(1.0 + jnp.exp(-jnp.abs(y)))


def _rwkv_kernel(z_ref, prev_ref, mu_ref, pv_ref, wup_ref, aup_ref, gup_ref, o_ref,
                 rt_s, at_s, kh_s, bh_s, kb_s, bb_s, v_s, gc_s, g_s, bonus_s, y_s, state_s, *, tb):
    C = RWKV_CHUNK
    step = pl.program_id(0)

    @pl.when(step == 0)
    def _():
        state_s[...] = jnp.zeros(state_s.shape, F32)

    has_prev = jnp.where(step > 0, 1.0, 0.0)
    row = lax.broadcasted_iota(jnp.int32, (tb, LANES), 0)

    def shifted_mix(t):
        z = z_ref[t]
        zp = jnp.where(row == 0, prev_ref[t, 7:8, :] * has_prev, pltpu.roll(z, 1, axis=0))
        return z + (zp - z) * mu_ref[t]

    ri = lax.broadcasted_iota(jnp.int32, (tb, tb), 0)
    ci = lax.broadcasted_iota(jnp.int32, (tb, tb), 1)
    tri = jnp.where(((ri >> 6) == (ci >> 6)) & (ci <= ri), 1.0, 0.0).astype(BF16)
    wide = 2 * LANES
    wr = lax.broadcasted_iota(jnp.int32, (wide, wide), 0)
    wc = lax.broadcasted_iota(jnp.int32, (wide, wide), 1)
    head_sum = jnp.where((wr >> 6) == (wc >> 6), 1.0, 0.0).astype(BF16)
    head_mean = jnp.where((wr >> 6) == (wc >> 6), 1.0 / RWKV_HEAD_DIM, 0.0).astype(BF16)

    wa = shifted_mix(3 * RWKV_PAIRS)
    dw = _dot(jnp.tanh(wa).astype(BF16), wup_ref[...])
    da = _dot(wa.astype(BF16), aup_ref[...])
    g0 = jax.nn.sigmoid(shifted_mix(3 * RWKV_PAIRS + 1)).astype(BF16)
    g1 = jax.nn.sigmoid(shifted_mix(3 * RWKV_PAIRS + 2)).astype(BF16)
    g_s[...] = _dot(g0, gup_ref[:LANES]) + _dot(g1, gup_ref[LANES:])

    def mixed(first_tile, q):
        return jnp.concatenate([shifted_mix(first_tile + 2 * q), shifted_mix(first_tile + 2 * q + 1)], axis=1)

    for q in range(RWKV_PAIRS // 2):
        sl = slice(q * wide, (q + 1) * wide)
        r = mixed(0, q)
        k = mixed(RWKV_PAIRS, q)
        v = mixed(2 * RWKV_PAIRS, q)
        w_log = -_softplus(-(pv_ref[0:1, sl] + dw[:, sl])) - 0.5
        lw = -jnp.exp(w_log)
        icl = jax.nn.sigmoid(pv_ref[1:2, sl] + da[:, sl])
        kk = k * pv_ref[2:3, sl]
        kk = kk * lax.rsqrt(jnp.maximum(_dot((kk * kk).astype(BF16), head_sum), 1e-24))
        k2 = k * (1.0 + (icl - 1.0) * pv_ref[3:4, sl])
        bonus_s[:, sl] = _dot((r * k2 * pv_ref[4:5, sl]).astype(BF16), head_sum) * v
        hi, mid, lo = _split3(lw)
        cum = _dot(tri, hi) + _dot(tri, mid) + _dot(tri, lo)
        b = kk * icl
        inv_decay = jnp.exp(-cum)
        to_end = []
        for c in range(tb // C):
            e_end = jnp.exp(cum[c * C + C - 1:c * C + C])
            gc_s[8 * c:8 * c + 1, sl] = e_end
            to_end.append(inv_decay[c * C:(c + 1) * C] * e_end)
        to_end = jnp.concatenate(to_end, axis=0)
        rt_s[:, sl] = (r * jnp.exp(cum)).astype(BF16)
        at_s[:, sl] = (-kk * jnp.exp(cum - lw)).astype(BF16)
        kh_s[:, sl] = (k2 * inv_decay).astype(BF16)
        bh_s[:, sl] = (b * inv_decay).astype(BF16)
        kb_s[:, sl] = (k2 * to_end).astype(BF16)
        bb_s[:, sl] = (b * to_end).astype(BF16)
        v_s[:, sl] = v.astype(BF16)

    hr = lax.broadcasted_iota(jnp.int32, (LANES, LANES), 0)
    hc = lax.broadcasted_iota(jnp.int32, (LANES, LANES), 1)
    same_head = (hr >> 6) == (hc >> 6)
    t_r = hr & (C - 1)
    t_c = hc & (C - 1)
    strict = same_head & (t_c < t_r)
    incl = same_head & (t_c <= t_r)
    eye = jnp.where(hr == hc, 1.0, 0.0)
    lane_head = lax.broadcasted_iota(jnp.int32, (C, LANES), 1) >> 6

    def stack(zc):
        return jnp.concatenate([jnp.where(lane_head == 0, zc, jnp.zeros_like(zc)),
                                jnp.where(lane_head == 1, zc, jnp.zeros_like(zc))], axis=0)

    def chunks(cc, carry):
        group = range(RWKV_CHUNK_GROUP)
        pairs = range(RWKV_PAIRS)
        units = [(d, p) for d in group for p in pairs]
        rows = [pl.ds(pl.multiple_of((cc * RWKV_CHUNK_GROUP + d) * C, C), C) for d in group]
        sls = [slice(p * LANES, (p + 1) * LANES) for p in pairs]
        lhs = {u: jnp.concatenate([stack(at_s[rows[u[0]], sls[u[1]]]), stack(rt_s[rows[u[0]], sls[u[1]]])], axis=0)
               for u in units}
        rhs = {u: jnp.concatenate([stack(kh_s[rows[u[0]], sls[u[1]]]), stack(bh_s[rows[u[0]], sls[u[1]]])], axis=0)
               for u in units}
        aa = {u: _dot_nt(lhs[u], rhs[u]) for u in units}
        a_ak = {u: jnp.where(strict, aa[u][:2 * C, :2 * C], 0.0).astype(BF16) for u in units}
        n_pow = {u: jnp.where(strict, aa[u][:2 * C, 2 * C:], 0.0) for u in units}
        a_r = {u: jnp.concatenate([jnp.where(incl, aa[u][2 * C:, :2 * C], 0.0).astype(BF16),
                                   jnp.where(incl, aa[u][2 * C:, 2 * C:], 0.0).astype(BF16)], axis=1) for u in units}
        t_inv = {u: eye + n_pow[u] for u in units}
        for _ in range(5):
            nb = {u: n_pow[u].astype(BF16) for u in units}
            n_pow = {u: _dot(nb[u], nb[u]) for u in units}
            t_inv = {u: t_inv[u] + _dot(t_inv[u].astype(BF16), n_pow[u].astype(BF16)) for u in units}
        t_inv = {u: t_inv[u].astype(BF16) for u in units}
        state = [state_s[p] for p in pairs]
        for d in group:
            xs = [_dot_nt(lhs[d, p], state[p].astype(BF16)) for p in pairs]
            v_c = [v_s[rows[d], sl] for sl in sls]
            v_st = [stack(v_c[p]) for p in pairs]
            av = [_dot(a_ak[d, p], v_st[p]) for p in pairs]
            sa = [_dot(t_inv[d, p], (xs[p][:2 * C] + av[p]).astype(BF16)) for p in pairs]
            ys = [xs[p][2 * C:] + _dot(a_r[d, p], jnp.concatenate([v_st[p], sa[p].astype(BF16)], axis=0))
                  for p in pairs]
            upd = [_dot_tn(jnp.concatenate([v_c[p], (sa[p][:C] + sa[p][C:]).astype(BF16)], axis=0),
                           jnp.concatenate([kb_s[rows[d], sls[p]], bb_s[rows[d], sls[p]]], axis=0)) for p in pairs]
            decay_row = pl.ds(pl.multiple_of((cc * RWKV_CHUNK_GROUP + d) * 8, 8), 1)
            for p in pairs:
                y_s[rows[d], sls[p]] = ys[p][:C] + ys[p][C:]
            state = [state[p] * gc_s[decay_row, sls[p]] + jnp.where(same_head, upd[p], 0.0) for p in pairs]
        for p in pairs:
            state_s[p] = state[p]
        return carry

    lax.fori_loop(0, tb // (C * RWKV_CHUNK_GROUP), chunks, 0)

    for q in range(RWKV_PAIRS // 2):
        sl = slice(q * wide, (q + 1) * wide)
        y = y_s[:, sl]
        d = y - _dot(y.astype(BF16), head_mean)
        var = _dot((d * d).astype(BF16), head_mean)
        yn = d * lax.rsqrt(var + LNX_EPS) * pv_ref[5:6, sl] + pv_ref[6:7, sl]
        o_ref[:, sl] = ((yn + bonus_s[:, sl]) * g_s[:, sl]).astype(o_ref.dtype)


def _hi_lo(w):
    hi = w.astype(BF16)
    return hi, (w - hi.astype(F32)).astype(BF16)


def _rwkv(zr, mu_t, w0, w_up, a0, a_up, g_up, k_k, k_a, r_k, lnx_g, lnx_b, tb):
    T = zr.shape[1]
    n_in = RW_TILES - 1
    pv = jnp.stack([w0, a0, k_k, k_a, r_k.reshape(-1), lnx_g, lnx_b, jnp.zeros_like(w0)])
    wup = jnp.pad(w_up, ((0, LANES - W_RANK), (0, 0))).astype(BF16)
    aup = jnp.pad(a_up, ((W_RANK, LANES - W_RANK - A_RANK), (0, 0))).astype(BF16)
    gup = jnp.pad(g_up, ((0, 2 * LANES - G_RANK), (0, 0))).astype(BF16)
    full = lambda shape: pl.BlockSpec(shape, lambda s: (0,) * len(shape))
    kern = functools.partial(_rwkv_kernel, tb=tb)
    bf = lambda: pltpu.VMEM((tb, RWKV_WIDTH), BF16)
    ff = lambda: pltpu.VMEM((tb, RWKV_WIDTH), F32)
    return pl.pallas_call(
        kern,
        grid=(T // tb,),
        in_specs=[
            pl.BlockSpec((n_in, tb, LANES), lambda s: (0, s, 0)),
            pl.BlockSpec((n_in, 8, LANES), lambda s: (0, jnp.maximum(s * (tb // 8) - 1, 0), 0)),
            full((n_in, 1, LANES)),
            full((8, RWKV_WIDTH)),
            full((LANES, RWKV_WIDTH)), full((LANES, RWKV_WIDTH)), full((2 * LANES, RWKV_WIDTH)),
        ],
        out_specs=pl.BlockSpec((tb, RWKV_WIDTH), lambda s: (s, 0)),
        out_shape=jax.ShapeDtypeStruct((T, RWKV_WIDTH), BF16),
        scratch_shapes=[bf(), bf(), bf(), bf(), bf(), bf(), bf(),
                        pltpu.VMEM((8 * (tb // RWKV_CHUNK), RWKV_WIDTH), F32), ff(), ff(), ff(),
                        pltpu.VMEM((RWKV_PAIRS, LANES, LANES), F32)],
        compiler_params=_cparams(("arbitrary",)),
        name="rwkv7",
    )(zr, zr, mu_t, pv, wup, aup, gup)


PROJ_COLS = 512


def _mix_out_kernel(ya_ref, yb_ref, g_ref, w_ref, x_ref, o_ref):
    h = jnp.concatenate([_rms(ya_ref[...], g_ref[...]).astype(BF16), yb_ref[...].astype(BF16)], axis=1)
    for n in range(D_MODEL // PROJ_COLS):
        sl = slice(n * PROJ_COLS, (n + 1) * PROJ_COLS)
        o_ref[:, sl] = x_ref[:, sl] + _dot(h, w_ref[:, sl])


def _mix_out(y_a, y_b, g, w, x, tm):
    T = x.shape[0]
    return pl.pallas_call(
        _mix_out_kernel,
        grid=(T // tm,),
        in_specs=[
            pl.BlockSpec((tm, NSA_WIDTH), lambda m: (m, 0)),
            pl.BlockSpec((tm, RWKV_WIDTH), lambda m: (m, 0)),
            _resident((1, NSA_WIDTH)),
            _resident((NSA_WIDTH + RWKV_WIDTH, D_MODEL)),
            pl.BlockSpec((tm, D_MODEL), lambda m: (m, 0)),
        ],
        out_specs=pl.BlockSpec((tm, D_MODEL), lambda m: (m, 0)),
        out_shape=jax.ShapeDtypeStruct((T, D_MODEL), F32),
        compiler_params=_cparams(("parallel",)),
        name="mix_out_proj",
    )(y_a, y_b, g, w, x)


def _norm_mm_kernel(x_ref, g_ref, w_ref, o_ref, h_scr):
    @pl.when(pl.program_id(1) == 0)
    def _():
        h_scr[...] = _rms(x_ref[...], g_ref[...]).astype(BF16)

    o_ref[...] = _dot(h_scr[...], w_ref[...]).astype(o_ref.dtype)


def _norm_mm(x, g, w, tm, tn, name):
    M, K = x.shape
    N = w.shape[1]
    return pl.pallas_call(
        _norm_mm_kernel,
        grid=(M // tm, N // tn),
        in_specs=[
            pl.BlockSpec((tm, K), lambda m, n: (m, 0)),
            pl.BlockSpec((1, K), lambda m, n: (0, 0)),
            pl.BlockSpec((K, tn), lambda m, n: (0, n)),
        ],
        out_specs=pl.BlockSpec((tm, tn), lambda m, n: (m, n)),
        out_shape=jax.ShapeDtypeStruct((M, N), BF16),
        scratch_shapes=[pltpu.VMEM((tm, K), BF16)],
        compiler_params=_cparams(("parallel", "arbitrary")),
        name=name,
    )(x, g, w)


def _mem_router_kernel(x_ref, km_ref, vm_ref, wq_ref, wo_ref, g2_ref, g3_ref, w_hi, w_lo, b_ref,
                       x2_ref, h_ref, c_ref, n_ref):
    scale = MEM_HEAD_DIM ** -0.5
    hq = _rms(x_ref[...], g2_ref[...]).astype(BF16)
    heads = []
    for hd in range(MEM_HEADS):
        sl = slice(hd * MEM_HEAD_DIM, (hd + 1) * MEM_HEAD_DIM)
        q = _dot(hq, wq_ref[:, sl]).astype(BF16)
        s = _dot_nt(q, km_ref[:, sl]) * scale
        e = jnp.exp(s - jnp.max(s, axis=-1, keepdims=True))
        p = e / jnp.sum(e, axis=-1, keepdims=True)
        heads.append(_dot(p.astype(BF16), vm_ref[:, sl]).astype(BF16))
    o = jnp.concatenate(heads, axis=1)
    for n in range(D_MODEL // PROJ_COLS):
        sl = slice(n * PROJ_COLS, (n + 1) * PROJ_COLS)
        x2_ref[:, sl] = x_ref[:, sl] + _dot(o, wo_ref[:, sl])
    _route(x2_ref[...], g3_ref, w_hi, w_lo, b_ref, h_ref, c_ref, n_ref)


def _mem_router(x, k_mem, v_mem, wq, wo, g2, g3, w_hi, w_lo, b, tm):
    T = x.shape[0]
    M = k_mem.shape[0]
    row_block = lambda width: pl.BlockSpec((tm, width), lambda m: (m, 0))
    return pl.pallas_call(
        _mem_router_kernel,
        grid=(T // tm,),
        in_specs=[
            row_block(D_MODEL),
            _resident((M, D_MODEL)), _resident((M, D_MODEL)),
            _resident((D_MODEL, D_MODEL)), _resident((D_MODEL, D_MODEL)),
            _resident((1, D_MODEL)), _resident((1, D_MODEL)),
            _resident((D_MODEL, LANES)), _resident((D_MODEL, LANES)), _resident((1, LANES)),
        ],
        out_specs=[row_block(D_MODEL), row_block(D_MODEL), row_block(LANES),
                   pl.BlockSpec((1, 8, LANES), lambda m: (m, 0, 0))],
        out_shape=[
            jax.ShapeDtypeStruct((T, D_MODEL), F32),
            jax.ShapeDtypeStruct((T, D_MODEL), BF16),
            jax.ShapeDtypeStruct((T, LANES), F32),
            jax.ShapeDtypeStruct((T // tm, 8, LANES), F32),
        ],
        compiler_params=_cparams(("parallel",)),
        name="mem_attn_router",
    )(x, k_mem, v_mem, wq, wo, g2, g3, w_hi, w_lo, b)


def _route(x, g_ref, w_hi, w_lo, b_ref, h_ref, c_ref, n_ref):
    h = _rms(x, g_ref[...])
    h_ref[...] = h.astype(BF16)
    logits = _dot_xw3(h, w_hi[...], w_lo[...]) + b_ref[...]
    lane = lax.broadcasted_iota(jnp.int32, logits.shape, 1)
    big = jnp.int32(LANES)
    is_grp = (lane >= N_EXPERTS) & (lane < N_EXPERTS + N_GROUPS)
    lg = jnp.where(is_grp, logits, NEG)
    eg = jnp.where(is_grp, jnp.exp(lg - jnp.max(lg, axis=-1, keepdims=True)), 0.0)
    pg = eg / jnp.sum(eg, axis=-1, keepdims=True)
    pg_top = jnp.max(pg, axis=-1, keepdims=True)
    g_idx = jnp.min(jnp.where(is_grp & (pg == pg_top), lane - N_EXPERTS, big), axis=-1, keepdims=True)
    in_grp = (lane < N_EXPERTS) & ((lane >> 3) == g_idx)
    le = jnp.where(in_grp, logits, NEG)
    ee = jnp.where(in_grp, jnp.exp(le - jnp.max(le, axis=-1, keepdims=True)), 0.0)
    pe = jnp.where(in_grp, ee / jnp.sum(ee, axis=-1, keepdims=True), -1.0)
    p1 = jnp.max(pe, axis=-1, keepdims=True)
    hit1 = lane == jnp.min(jnp.where(pe == p1, lane, big), axis=-1, keepdims=True)
    pe2 = jnp.where(hit1, -1.0, pe)
    p2 = jnp.max(pe2, axis=-1, keepdims=True)
    hit2 = lane == jnp.min(jnp.where(pe2 == p2, lane, big), axis=-1, keepdims=True)
    denom = p1 + p2
    comb = jnp.where(hit1, pg_top * p1 / denom, 0.0) + jnp.where(hit2, pg_top * p2 / denom, 0.0)
    c_ref[...] = jnp.where(lane == GROUP_LANE, g_idx.astype(F32), comb)
    in_group = jnp.where(lane == g_idx, 1.0, 0.0)
    n_ref[0] = jnp.broadcast_to(jnp.sum(in_group, axis=0, keepdims=True), n_ref.shape[1:])


def _moe_kernel(meta_ref, h_ref, c_ref, wg_ref, wu_ref, wd_ref, y_ref, hs, cs, ys, slot_scr, perm, *, tm, sub):
    m = pl.program_id(0)
    e = pl.program_id(1)
    nslot = hs.shape[0]

    @pl.when(e == 0)
    def _():
        c = c_ref[...]
        lane = lax.broadcasted_iota(jnp.int32, c.shape, 1)
        one_hot = jnp.where(lane.astype(F32) == c[:, GROUP_LANE:GROUP_LANE + 1], 1.0, 0.0)
        earlier = jnp.where(lax.broadcasted_iota(jnp.int32, (tm, tm), 1)
                            < lax.broadcasted_iota(jnp.int32, (tm, tm), 0), 1.0, 0.0).astype(BF16)
        rank = _dot(earlier, one_hot.astype(BF16))
        start = jnp.zeros((1, LANES), F32)
        for g in range(N_GROUPS):
            start = jnp.where(lane[0:1] == g, (meta_ref[m * 2 * N_GROUPS + g] * sub).astype(F32), start)
        slot = jnp.sum(one_hot * (rank + start), axis=-1, keepdims=True)
        slot_scr[...] = jnp.broadcast_to(slot, slot_scr.shape)
        slot_row = slot_scr[...].T[0:1, :]
        perm[...] = jnp.where(lax.broadcasted_iota(jnp.int32, (nslot, tm), 0).astype(F32) == slot_row,
                              1.0, 0.0).astype(BF16)
        hs[...] = _dot(perm[...], h_ref[...]).astype(BF16)
        w_hi, w_lo = _split2(jnp.where(lane < N_EXPERTS, c, 0.0))
        cs[...] = _dot(perm[...], w_hi) + _dot(perm[...], w_lo)
        ys[...] = jnp.zeros(ys.shape, F32)

    grp = e // EXPERTS_PER_GROUP
    first = meta_ref[m * 2 * N_GROUPS + grp]
    count = meta_ref[m * 2 * N_GROUPS + N_GROUPS + grp]
    def ffn(start_block, n_blocks):
        n = n_blocks * sub
        rows = pl.ds(pl.multiple_of(start_block * sub, sub), n)
        x = hs[rows, :]
        hid = jax.nn.silu(_dot(x, wg_ref[0])) * _dot(x, wu_ref[0])
        lane_s = lax.broadcasted_iota(jnp.int32, (n, LANES), 1)
        c_e = jnp.sum(jnp.where(lane_s == e, cs[rows, :], 0.0), axis=-1, keepdims=True)
        ys[rows, :] += _dot((hid * c_e).astype(BF16), wd_ref[0])

    def quad(s, carry):
        ffn(first + 4 * s, 4)
        return carry

    quads = count // 4
    lax.fori_loop(0, quads, quad, 0)
    rest = count - 4 * quads

    @pl.when(rest >= 2)
    def _():
        ffn(first + 4 * quads, 2)

    @pl.when(rest % 2 == 1)
    def _():
        ffn(first + count - 1, 1)

    @pl.when(e == pl.num_programs(1) - 1)
    def _():
        back = jnp.where(lax.broadcasted_iota(jnp.int32, (tm, nslot), 1).astype(F32) == slot_scr[:, 0:1],
                         1.0, 0.0).astype(BF16)
        for n in range(D_MODEL // PROJ_COLS):
            sl = slice(n * PROJ_COLS, (n + 1) * PROJ_COLS)
            y_ref[:, sl] = _dot(back, ys[:, sl].astype(BF16)).astype(y_ref.dtype)


def _moe(h, comb, counts, wg, wu, wd, tm, sub):
    T = h.shape[0]
    nt = T // tm
    nslot = tm + N_GROUPS * sub
    cnt = counts[:, 0, :N_GROUPS].astype(jnp.int32).reshape(nt, -1, N_GROUPS).sum(axis=1)
    nblk = (cnt + sub - 1) // sub
    first = jnp.cumsum(nblk, axis=1) - nblk
    meta = jnp.concatenate([first, nblk], axis=1).reshape(-1)
    once = lambda shape: pl.BlockSpec(shape, lambda m, e, meta: (m, 0), pipeline_mode=pl.Buffered(1))
    kern = functools.partial(_moe_kernel, tm=tm, sub=sub)
    return pl.pallas_call(
        kern,
        grid_spec=pltpu.PrefetchScalarGridSpec(
            num_scalar_prefetch=1,
            grid=(nt, N_EXPERTS),
            in_specs=[
                once((tm, D_MODEL)),
                once((tm, LANES)),
                pl.BlockSpec((1, D_MODEL, EXPERT_FF), lambda m, e, meta: (e, 0, 0)),
                pl.BlockSpec((1, D_MODEL, EXPERT_FF), lambda m, e, meta: (e, 0, 0)),
                pl.BlockSpec((1, EXPERT_FF, D_MODEL), lambda m, e, meta: (e, 0, 0)),
            ],
            out_specs=pl.BlockSpec((tm, D_MODEL), lambda m, e, meta: (m, 0)),
            scratch_shapes=[
                pltpu.VMEM((nslot, D_MODEL), BF16),
                pltpu.VMEM((nslot, LANES), F32),
                pltpu.VMEM((nslot, D_MODEL), F32),
                pltpu.VMEM((tm, LANES), F32),
                pltpu.VMEM((nslot, tm), BF16),
            ],
        ),
        out_shape=jax.ShapeDtypeStruct((T, D_MODEL), BF16),
        compiler_params=_cparams(("parallel", "arbitrary")),
        name="moe_experts",
    )(meta, h, comb, wg, wu, wd)


def _final_norm_kernel(x_ref, y_ref, g_ref, o_ref):
    o_ref[...] = _rms(x_ref[...] + y_ref[...].astype(F32), g_ref[...])


def _final_norm(x, y, g, tm):
    T = x.shape[0]
    blk = pl.BlockSpec((tm, D_MODEL), lambda m: (m, 0))
    return pl.pallas_call(
        _final_norm_kernel,
        grid=(T // tm,),
        in_specs=[blk, blk, _resident((1, D_MODEL))],
        out_specs=blk,
        out_shape=jax.ShapeDtypeStruct((T, D_MODEL), F32),
        compiler_params=_cparams(("parallel",)),
        name="moe_residual_norm",
    )(x, y, g)


def _pack_w_in(w_in, rwkv_mu):
    nsa_cols = NSA_WIDTH + 6 * NSA_KV_WIDTH
    gl = w_in[:, nsa_cols:nsa_cols + 3 * NSA_HEADS]
    rw0 = nsa_cols + 3 * NSA_HEADS
    rkv = w_in[:, rw0:rw0 + 3 * RWKV_WIDTH + W_RANK + A_RANK]
    gd = w_in[:, rw0 + 3 * RWKV_WIDTH + W_RANK + A_RANK:]

    def pad(a, n):
        return jnp.pad(a, ((0, 0), (0, n - a.shape[1])))

    w = jnp.concatenate([w_in[:, :nsa_cols], rkv, pad(gd, 2 * LANES), pad(gl, LANES)], axis=1)
    mu = rwkv_mu.reshape(1, -1)
    mu_rkv = mu[:, :3 * RWKV_WIDTH + W_RANK + A_RANK]
    mu_gd = pad(mu[:, 3 * RWKV_WIDTH + W_RANK + A_RANK:], 2 * LANES)
    mu_t = jnp.concatenate([mu_rkv, mu_gd], axis=1).reshape(RW_TILES - 1, 1, LANES)
    return w.astype(BF16), mu_t


def kernel(x, mem, ln1_g, w_in, cmp_pos_k, cmp_w1_k, cmp_w2_k, cmp_pos_v, cmp_w1_v, cmp_w2_v, nsa_norm_g, rwkv_mu, rwkv_w0, rwkv_w_up, rwkv_a0, rwkv_a_up, rwkv_g_up, rwkv_k_k, rwkv_k_a, rwkv_r_k, rwkv_lnx_g, rwkv_lnx_b, w_out, ln_mem_g, ln2_g, wq_mem, wk_mem, wv_mem, wo_mem, ln3_g, router_group_w, router_group_b, router_expert_w, router_expert_b, moe_w_gate, moe_w_up, moe_w_down, lnf_g):
    B, T, _ = x.shape
    assert B == 1 and T % max(TM_PROJ, TM_MEM, TQ_SEL, TK_SEL, TB_RWKV, TM_MOE) == 0
    assert w_in.shape[0] == 1
    row = lambda a: a.reshape(1, -1)
    xs = x[0]
    for l in range(w_in.shape[0]):
        wp, mu_t = _pack_w_in(w_in[l], rwkv_mu[l])
        sel_window = min(LANES, T // SLC_BLOCK)
        zn, zr, zc, k_aug, vt_aug, vw_t = _in_proj(xs, row(ln1_g[l]), wp, TM_PROJ, sel_window)
        y_a = _nsa(zn, zr, zc, k_aug, vt_aug, vw_t, cmp_pos_k[l], cmp_w1_k[l], cmp_w2_k[l], cmp_pos_v[l], cmp_w1_v[l], cmp_w2_v[l],
                   TQ_NSA, TQ_SEL, TK_SEL)
        y_b = _rwkv(zr, mu_t, rwkv_w0[l], rwkv_w_up[l], rwkv_a0[l], rwkv_a_up[l], rwkv_g_up[l], rwkv_k_k[l],
                    rwkv_k_a[l], rwkv_r_k[l], rwkv_lnx_g[l], rwkv_lnx_b[l], TB_RWKV)
        xs = _mix_out(y_a, y_b, row(nsa_norm_g[l]), w_out[l].astype(BF16), xs, TM_MEM)
        m_tok = mem[0]
        k_mem = _norm_mm(m_tok, row(ln_mem_g[l]), wk_mem[l].astype(BF16), m_tok.shape[0], PROJ_COLS, "mem_k")
        v_mem = _norm_mm(m_tok, row(ln_mem_g[l]), wv_mem[l].astype(BF16), m_tok.shape[0], PROJ_COLS, "mem_v")
        w_r = jnp.pad(jnp.concatenate([router_expert_w[l], router_group_w[l]], axis=1),
                      ((0, 0), (0, LANES - N_EXPERTS - N_GROUPS)))
        b_r = jnp.pad(jnp.concatenate([router_expert_b[l], router_group_b[l]]), (0, LANES - N_EXPERTS - N_GROUPS))
        xs, h3, comb, counts = _mem_router(xs, k_mem, v_mem, wq_mem[l].astype(BF16), wo_mem[l].astype(BF16),
                                   row(ln2_g[l]), row(ln3_g[l]), *_hi_lo(w_r), row(b_r), TM_MEM)
        y_moe = _moe(h3, comb, counts, moe_w_gate[l].astype(BF16), moe_w_up[l].astype(BF16),
                     moe_w_down[l].astype(BF16), TM_MOE, SUB_MOE)
        xs = _final_norm(xs, y_moe, row(lnf_g), TM_PROJ)
    return xs[None]
```
